```python
import jax, jax.numpy as jnp
from jax import lax
import numpy as np

D_MODEL = 1024
BATCH = 8
SEQ = 2048
DEPTH = 2

GRID_W = 64
NA_HEADS = 4
NA_HEAD_DIM = 64
NA_KH = 8
NA_KW = 16
NA_QB = 16
MLA_HEADS = 4
MLA_Q_LORA = 256
MLA_KV_LORA = 128
MLA_NOPE = 64
MLA_ROPE = 32
MLA_V = 64
GQA_HEADS = 8
GQA_KV_HEADS = 2
GQA_HEAD_DIM = 64
N_BRANCHES = 3
FFN_HIDDEN = -(-8 * D_MODEL // (3 * 256)) * 256
Q_BLOCK = 128
ROPE_THETA = 10000.0
AXIAL_THETA = 10000.0
NORM_EPS = 1e-6
N_MOD = 6

kernel_name = "hybrid_na_mla_gqa_encoder"


def _in_splits():
    return ([NA_HEADS * NA_HEAD_DIM] * 3
            + [MLA_Q_LORA, MLA_KV_LORA, MLA_ROPE]
            + [GQA_HEADS * GQA_HEAD_DIM, GQA_KV_HEADS * GQA_HEAD_DIM, GQA_KV_HEADS * GQA_HEAD_DIM]
            + [D_MODEL] * N_BRANCHES)


def _in_cols():
    return int(sum(_in_splits()))


def rms_norm(x, g):
    xf = x.astype(jnp.float32)
    y = xf * lax.rsqrt(jnp.mean(xf * xf, axis=-1, keepdims=True) + NORM_EPS)
    return (y * g.astype(jnp.float32)).astype(x.dtype)


def rope(x, pos, theta):
    d = x.shape[-1]
    half = d // 2
    inv_freq = 1.0 / (theta ** (jnp.arange(half, dtype=jnp.float32) / half))
    ang = pos[:, None] * inv_freq[None, :]
    ang = ang.reshape((ang.shape[0],) + (1,) * (x.ndim - 3) + (half,))
    cos, sin = jnp.cos(ang), jnp.sin(ang)
    xf = x.astype(jnp.float32)
    x1, x2 = xf[..., :half], xf[..., half:]
    return jnp.concatenate([x1 * cos - x2 * sin, x2 * cos + x1 * sin], axis=-1).astype(x.dtype)


def axial_rope(x, row, col):
    h = x.shape[-1] // 2
    return jnp.concatenate([rope(x[..., :h], row, AXIAL_THETA), rope(x[..., h:], col, AXIAL_THETA)], axis=-1)


def blocked_attention(q, k, v, scale):
    B, S, KVH, G, dk = q.shape
    dv = v.shape[-1]
    nb = S // Q_BLOCK
    qb = jnp.moveaxis(q.reshape(B, nb, Q_BLOCK, KVH, G, dk), 1, 0)

    def one_block(qi):
        s = jnp.einsum('bqkgd,bskd->bkgqs', qi, k, preferred_element_type=jnp.float32) * scale
        p = jax.nn.softmax(s, axis=-1)
        return jnp.einsum('bkgqs,bskd->bqkgd', p.astype(v.dtype), v)

    o = lax.map(one_block, qb)
    return jnp.moveaxis(o, 0, 1).reshape(B, S, KVH * G * dv)


def neighborhood_attention(q, k, v, rpb):
    B, S, H, d = q.shape
    rows = S // GRID_W
    kh = min(NA_KH, rows)
    kw = NA_KW
    n_cb = GRID_W // NA_QB
    band = NA_QB + kw
    scale = d ** -0.5
    qg = q.reshape(B, rows, GRID_W, H, d)
    kg = k.reshape(B, rows, GRID_W, H, d)
    vg = v.reshape(B, rows, GRID_W, H, d)

    cb_start = jnp.clip(jnp.arange(n_cb) * NA_QB - kw // 2, 0, GRID_W - band)
    band_cols = cb_start[:, None] + jnp.arange(band)[None, :]
    q_cols = (jnp.arange(n_cb) * NA_QB)[:, None] + jnp.arange(NA_QB)[None, :]
    win_start = jnp.clip(q_cols - kw // 2, 0, GRID_W - kw)
    bc = band_cols[:, None, :]
    col_valid = (bc >= win_start[..., None]) & (bc < win_start[..., None] + kw)
    col_off = jnp.clip(bc - q_cols[..., None], -(kw - 1), kw - 1) + (kw - 1)
    rpb_cols = rpb[:, :, col_off]

    def one_row(r):
        rs = jnp.clip(r - kh // 2, 0, rows - kh)
        k_rows = lax.dynamic_slice_in_dim(kg, rs, kh, axis=1)
        v_rows = lax.dynamic_slice_in_dim(vg, rs, kh, axis=1)
        k_band = k_rows[:, :, band_cols]
        v_band = v_rows[:, :, band_cols]
        q_row = lax.dynamic_index_in_dim(qg, r, axis=1, keepdims=False).reshape(B, n_cb, NA_QB, H, d)
        s = jnp.einsum('bnqhd,bknjhd->bhnqkj', q_row, k_band, preferred_element_type=jnp.float32) * scale
        row_off = rs + jnp.arange(kh) - r + (NA_KH - 1)
        bias = jnp.transpose(rpb_cols[:, row_off], (0, 2, 3, 1, 4))
        s = s + bias[None].astype(jnp.float32)
        s = jnp.where(col_valid[None, None, :, :, None, :], s, -jnp.inf)
        p = jax.nn.softmax(s.reshape(B, H, n_cb, NA_QB, kh * band), axis=-1)
        p = p.reshape(B, H, n_cb, NA_QB, kh, band)
        o = jnp.einsum('bhnqkj,bknjhd->bnqhd', p.astype(v.dtype), v_band)
        return o.reshape(B, GRID_W, H, d)

    out = lax.map(one_row, jnp.arange(rows))
    return jnp.moveaxis(out, 0, 1).reshape(B, S, H * d)


def hybrid_layer(x, c, ada_w, ada_b, norm_mix_g, norm_ffn_g, w_in, na_rpb,
                 mla_q_norm_g, mla_kv_norm_g, mla_w_uq, mla_w_ukv, gqa_q_norm_g, gqa_k_norm_g,
                 w_br_na, w_br_mla, w_br_gqa, w_out, ffn_w_gate_up, ffn_w_down, pos_t, pos_row, pos_col):
    B, S, D = x.shape
    mod = jax.nn.silu(c) @ ada_w + ada_b
    sh_m, sc_m, g_m, sh_f, sc_f, g_f = [m[:, None, :] for m in jnp.split(mod, N_MOD, axis=-1)]

    h = rms_norm(x, norm_mix_g) * (1 + sc_m) + sh_m
    proj = h @ w_in
    idx = [int(i) for i in np.cumsum(_in_splits())[:-1]]
    (qa, ka, va, cq, ckv, kr, qc, kc, vc, ga, gb, gc) = jnp.split(proj, idx, axis=-1)

    o_na = neighborhood_attention(qa.reshape(B, S, NA_HEADS, NA_HEAD_DIM),
                                  ka.reshape(B, S, NA_HEADS, NA_HEAD_DIM),
                                  va.reshape(B, S, NA_HEADS, NA_HEAD_DIM), na_rpb)

    q_b = (rms_norm(cq, mla_q_norm_g) @ mla_w_uq).reshape(B, S, MLA_HEADS, MLA_NOPE + MLA_ROPE)
    q_nope, q_pe = q_b[..., :MLA_NOPE], rope(q_b[..., MLA_NOPE:], pos_t, ROPE_THETA)
    kv_b = (rms_norm(ckv, mla_kv_norm_g) @ mla_w_ukv).reshape(B, S, MLA_HEADS, MLA_NOPE + MLA_V)
    k_nope, v_b = kv_b[..., :MLA_NOPE], kv_b[..., MLA_NOPE:]
    k_pe = jnp.broadcast_to(rope(kr, pos_t, ROPE_THETA)[:, :, None, :], (B, S, MLA_HEADS, MLA_ROPE))
    q_full = jnp.concatenate([q_nope, q_pe], axis=-1)[:, :, :, None, :]
    k_full = jnp.concatenate([k_nope, k_pe], axis=-1)
    o_mla = blocked_attention(q_full, k_full, v_b, (MLA_NOPE + MLA_ROPE) ** -0.5)

    q_c = axial_rope(rms_norm(qc.reshape(B, S, GQA_HEADS, GQA_HEAD_DIM), gqa_q_norm_g), pos_row, pos_col)
    k_c = axial_rope(rms_norm(kc.reshape(B, S, GQA_KV_HEADS, GQA_HEAD_DIM), gqa_k_norm_g), pos_row, pos_col)
    v_c = vc.reshape(B, S, GQA_KV_HEADS, GQA_HEAD_DIM)
    q_c = q_c.reshape(B, S, GQA_KV_HEADS, GQA_HEADS // GQA_KV_HEADS, GQA_HEAD_DIM)
    o_gqa = blocked_attention(q_c, k_c, v_c, GQA_HEAD_DIM ** -0.5)

    merged = (jax.nn.sigmoid(ga) * (o_na @ w_br_na)
              + jax.nn.sigmoid(gb) * (o_mla @ w_br_mla)
              + jax.nn.sigmoid(gc) * (o_gqa @ w_br_gqa))
    x = x + g_m * (merged @ w_out)

    h = rms_norm(x, norm_ffn_g) * (1 + sc_f) + sh_f
    gate, up = jnp.split(h @ ffn_w_gate_up, 2, axis=-1)
    x = x + g_f * ((jax.nn.silu(gate) * up) @ ffn_w_down)
    return x


def setup_inputs(seed: int = 0) -> dict:
    key = jax.random.key(seed)
    ks = jax.random.split(key, 24)
    f32 = jnp.float32

    def w(k, shape, fan_in):
        return jax.random.normal(k, shape, f32) * fan_in ** -0.5

    def gain(k, shape):
        return 1.0 + 0.02 * jax.random.normal(k, shape, f32)

    L, D = DEPTH, D_MODEL
    return {
        "x": jax.random.normal(ks[0], (BATCH, SEQ, D), f32),
        "c": jax.random.normal(ks[1], (BATCH, D), f32),
        "ada_w": w(ks[2], (L, D, N_MOD * D), D),
        "ada_b": 0.02 * jax.random.normal(ks[3], (L, N_MOD * D), f32),
        "norm_mix_g": gain(ks[4], (L, D)),
        "norm_ffn_g": gain(ks[5], (L, D)),
        "w_in": w(ks[6], (L, D, _in_cols()), D),
        "na_rpb": 0.1 * jax.random.normal(ks[7], (L, NA_HEADS, 2 * NA_KH - 1, 2 * NA_KW - 1), f32),
        "mla_q_norm_g": gain(ks[8], (L, MLA_Q_LORA)),
        "mla_kv_norm_g": gain(ks[9], (L, MLA_KV_LORA)),
        "mla_w_uq": w(ks[10], (L, MLA_Q_LORA, MLA_HEADS * (MLA_NOPE + MLA_ROPE)), MLA_Q_LORA),
        "mla_w_ukv": w(ks[11], (L, MLA_KV_LORA, MLA_HEADS * (MLA_NOPE + MLA_V)), MLA_KV_LORA),
        "gqa_q_norm_g": gain(ks[12], (L, GQA_HEAD_DIM)),
        "gqa_k_norm_g": gain(ks[13], (L, GQA_HEAD_DIM)),
        "w_br_na": w(ks[14], (L, NA_HEADS * NA_HEAD_DIM, D), NA_HEADS * NA_HEAD_DIM),
        "w_br_mla": w(ks[15], (L, MLA_HEADS * MLA_V, D), MLA_HEADS * MLA_V),
        "w_br_gqa": w(ks[16], (L, GQA_HEADS * GQA_HEAD_DIM, D), GQA_HEADS * GQA_HEAD_DIM),
        "w_out": w(ks[17], (L, D, D), D),
        "ffn_w_gate_up": w(ks[18], (L, D, 2 * FFN_HIDDEN), D),
        "ffn_w_down": w(ks[19], (L, FFN_HIDDEN, D), FFN_HIDDEN),
        "final_norm_g": gain(ks[20], (D,)),
    }


def reference(x, c, ada_w, ada_b, norm_mix_g, norm_ffn_g, w_in, na_rpb, mla_q_norm_g, mla_kv_norm_g,
              mla_w_uq, mla_w_ukv, gqa_q_norm_g, gqa_k_norm_g, w_br_na, w_br_mla, w_br_gqa, w_out,
              ffn_w_gate_up, ffn_w_down, final_norm_g):
    S = x.shape[1]
    t = jnp.arange(S, dtype=jnp.int32)
    pos_t = t.astype(jnp.float32)
    pos_row = (t // GRID_W).astype(jnp.float32)
    pos_col = (t % GRID_W).astype(jnp.float32)
    for l in range(DEPTH):
        x = hybrid_layer(x, c, ada_w[l], ada_b[l], norm_mix_g[l], norm_ffn_g[l], w_in[l], na_rpb[l],
                         mla_q_norm_g[l], mla_kv_norm_g[l], mla_w_uq[l], mla_w_ukv[l],
                         gqa_q_norm_g[l], gqa_k_norm_g[l], w_br_na[l], w_br_mla[l], w_br_gqa[l], w_out[l],
                         ffn_w_gate_up[l], ffn_w_down[l], pos_t, pos_row, pos_col)
    return rms_norm(x, final_norm_g)
```

```python
import functools

import numpy as np
import jax
import jax.numpy as jnp
from jax import lax
from jax.experimental import pallas as pl
from jax.experimental.pallas import tpu as pltpu

D_MODEL = 1024
GRID_W = 64
NA_HEADS = 4
NA_HEAD_DIM = 64
NA_KH = 8
NA_KW = 16
MLA_HEADS = 4
MLA_Q_LORA = 256
MLA_KV_LORA = 128
MLA_NOPE = 64
MLA_ROPE = 32
MLA_V = 64
GQA_HEADS = 8
GQA_KV_HEADS = 2
GQA_HEAD_DIM = 64
FFN_HIDDEN = 2816
ROPE_THETA = 10000.0
AXIAL_THETA = 10000.0
NORM_EPS = 1e-6
N_MOD = 6
MASK_VALUE = -1e30

LANES = 128
VMEM_LIMIT = 56 * 1024 * 1024
TOKEN_TILE = 256
Q_TILE = 256
FFN_CHUNK = 1408

C_NA = 0
C_MLA = 768
C_GQA = 1280
C_GATE = 2048
C_TOTAL = 5120

F32 = jnp.float32
BF16 = jnp.bfloat16


def _dot(a, b):
    return jnp.dot(a, b, preferred_element_type=F32)


def _dot_nt(a, b):
    return lax.dot_general(a, b, (((1,), (1,)), ((), ())), preferred_element_type=F32)


def _rms(x):
    return x * lax.rsqrt(jnp.mean(x * x, axis=-1, keepdims=True) + NORM_EPS)


def _rope_slab(x, c, s_up, s_dn):
    return x * c + pltpu.roll(x, LANES - 16, 1) * s_up + pltpu.roll(x, 16, 1) * s_dn


def _sigmoid(x):
    return 1.0 / (1.0 + jnp.exp(-x))


def _ada_kernel(c_ref, w_ref, b_ref, o_ref):
    c = c_ref[...]
    s = (c * _sigmoid(c)).astype(BF16)
    o_ref[...] = _dot(s, w_ref[...].astype(BF16)) + b_ref[...]


def _ada_call(c, ada_w, ada_b):
    L, D, N = ada_w.shape
    B = c.shape[0]
    tn = 1536
    return pl.pallas_call(
        _ada_kernel,
        grid=(L, N // tn),
        in_specs=[pl.BlockSpec((B, D), lambda l, j: (0, 0)),
                  pl.BlockSpec((None, D, tn), lambda l, j: (l, 0, j)),
                  pl.BlockSpec((None, 1, tn), lambda l, j: (l, 0, j))],
        out_specs=pl.BlockSpec((None, B, tn), lambda l, j: (l, 0, j)),
        out_shape=jax.ShapeDtypeStruct((L, B, N), F32),
        compiler_params=pltpu.CompilerParams(dimension_semantics=("parallel", "parallel"),
                                             vmem_limit_bytes=VMEM_LIMIT),
        name="ada_mod",
    )(c, ada_w, ada_b.reshape(L, 1, N))


def _in_kernel(x_ref, mod_ref, g_ref, w_ref, qng_ref, kvng_ref, wuq_ref, wukv_ref, gqk_ref, bd_ref,
               mc_ref, mu_ref, md_ref, ac_ref, au_ref, ad_ref,
               qna_ref, kna_ref, vna_ref, qm_ref, km_ref, vm_ref, qg_ref, kg_ref, vg_ref, gate_ref):
    x = x_ref[...]
    h = _rms(x) * g_ref[...] * (1.0 + mod_ref[1:2, :]) + mod_ref[0:1, :]
    hb = h.astype(BF16)

    def proj(a, b):
        return _dot(hb, w_ref[:, a:b])

    p = proj(C_NA, C_NA + 768)
    qna_ref[...] = (p[:, :256] * NA_HEAD_DIM ** -0.5).astype(BF16)
    kna_ref[...] = p[:, 256:512].astype(BF16)
    vna_ref[...] = p[:, 512:768].astype(BF16)

    p = proj(C_MLA, C_MLA + 512)
    mc, mu, md = mc_ref[...], mu_ref[...], md_ref[...]
    nq = (_rms(p[:, :256]) * qng_ref[...]).astype(BF16)
    qb = _dot(nq, wuq_ref[...])
    mla_scale = (MLA_NOPE + MLA_ROPE) ** -0.5
    for s in range(MLA_HEADS):
        sl = slice(s * LANES, (s + 1) * LANES)
        qm_ref[:, sl] = (_rope_slab(qb[:, sl], mc, mu, md) * mla_scale).astype(BF16)
    nkv = (_rms(p[:, 256:384]) * kvng_ref[...]).astype(BF16)
    kvb = _dot(nkv, wukv_ref[...])
    kpe = _rope_slab(p[:, 384:512], mc, mu, md)
    for s in range(MLA_HEADS):
        sl = slice(s * LANES, (s + 1) * LANES)
        km_ref[:, sl] = (kvb[:, sl] + kpe).astype(BF16)
    vm_ref[...] = kvb[:, 512:768].astype(BF16)

    p = proj(C_GQA, C_GQA + 768)
    ac, au, ad = ac_ref[...], au_ref[...], ad_ref[...]
    bd = bd_ref[...]
    for s in range(5):
        sl = slice(s * LANES, (s + 1) * LANES)
        v = p[:, sl]
        sq = v * v
        hi = sq.astype(BF16)
        lo = (sq - hi.astype(F32)).astype(BF16)
        msq = _dot(hi, bd) + _dot(lo, bd)
        y = v * lax.rsqrt(msq + NORM_EPS) * gqk_ref[:, sl]
        r = _rope_slab(y, ac, au, ad)
        if s < 4:
            qg_ref[:, sl] = (r * GQA_HEAD_DIM ** -0.5).astype(BF16)
        else:
            kg_ref[...] = r.astype(BF16)
    vg_ref[...] = p[:, 640:768].astype(BF16)

    for j in range(3):
        g = proj(C_GATE + j * D_MODEL, C_GATE + (j + 1) * D_MODEL)
        gate_ref[:, j * D_MODEL:(j + 1) * D_MODEL] = _sigmoid(g).astype(BF16)


def _in_call(x, mod, g, w, qng, kvng, wuq, wukv, gqk, bd, mtabs, atabs):
    B, S, D = x.shape
    tm = TOKEN_TILE
    tok = lambda w_: pl.BlockSpec((None, tm, w_), lambda b, i: (b, i, 0))
    full = lambda a: pl.BlockSpec(a.shape, lambda b, i: (0,) * a.ndim)
    tab = pl.BlockSpec((tm, LANES), lambda b, i: (i, 0))
    widths = (256, 256, 256, 512, 512, 256, 512, 128, 128, 3 * D)
    return pl.pallas_call(
        _in_kernel,
        grid=(B, S // tm),
        in_specs=[tok(D), pl.BlockSpec((None, N_MOD, D), lambda b, i: (b, 0, 0)), full(g), full(w),
                  full(qng), full(kvng), full(wuq), full(wukv), full(gqk), full(bd)] + [tab] * 6,
        out_specs=[tok(w_) for w_ in widths],
        out_shape=[jax.ShapeDtypeStruct((B, S, w_), BF16) for w_ in widths],
        compiler_params=pltpu.CompilerParams(dimension_semantics=("parallel", "parallel"),
                                             vmem_limit_bytes=VMEM_LIMIT),
        name="in_proj",
    )(x, mod, g, w, qng, kvng, wuq, wukv, gqk, bd, *mtabs, *atabs)


def _na_kernel(q_ref, k_ref, v_ref, bias_ref, o_ref, *, rows):
    kh = NA_KH
    head_of_lane = lax.broadcasted_iota(jnp.int32, (GRID_W, NA_HEADS * NA_HEAD_DIM), 1) // NA_HEAD_DIM

    def body(r, carry):
        rs = jnp.clip(r - kh // 2, 0, rows - kh)
        q = q_ref[pl.ds(pl.multiple_of(r * GRID_W, GRID_W), GRID_W), :]
        kstart = pl.multiple_of(rs * GRID_W, GRID_W)
        kb = k_ref[pl.ds(kstart, kh * GRID_W), :]
        vb = v_ref[pl.ds(kstart, kh * GRID_W), :]
        qs = jnp.concatenate([jnp.where(head_of_lane == h, q, jnp.zeros_like(q)) for h in range(NA_HEADS)],
                             axis=0)
        s = _dot_nt(qs, kb) + bias_ref[r - rs]
        m = jnp.max(s, axis=-1, keepdims=True)
        p = jnp.exp(s - m)
        l = jnp.sum(p, axis=-1, keepdims=True)
        o = _dot(p.astype(BF16), vb) / l
        out = jnp.zeros((GRID_W, NA_HEADS * NA_HEAD_DIM), F32)
        for h in range(NA_HEADS):
            out = jnp.where(head_of_lane == h, o[h * GRID_W:(h + 1) * GRID_W, :], out)
        o_ref[pl.ds(pl.multiple_of(r * GRID_W, GRID_W), GRID_W), :] = out.astype(BF16)
        return carry

    lax.fori_loop(0, rows, body, 0)


def _na_call(q, k, v, bias):
    B, S, W = q.shape
    rows = S // GRID_W
    blk = pl.BlockSpec((None, S, W), lambda b: (b, 0, 0))
    return pl.pallas_call(
        functools.partial(_na_kernel, rows=rows),
        grid=(B,),
        in_specs=[blk, blk, blk, pl.BlockSpec(bias.shape, lambda b: (0, 0, 0))],
        out_specs=blk,
        out_shape=jax.ShapeDtypeStruct((B, S, W), BF16),
        compiler_params=pltpu.CompilerParams(dimension_semantics=("parallel",),
                                             vmem_limit_bytes=VMEM_LIMIT),
        name="na_attn",
    )(q, k, v, bias)


def _softmax_pv(q, k, v):
    s = _dot_nt(q, k)
    m = jnp.max(s, axis=-1, keepdims=True)
    p = jnp.exp(s - m)
    l = jnp.sum(p, axis=-1, keepdims=True)
    return _dot(p.astype(BF16), v) / l


def _mla_attn_kernel(q_ref, k_ref, v_ref, o_ref):
    v = v_ref[...]
    head_of_lane = lax.broadcasted_iota(jnp.int32, o_ref.shape, 1) // MLA_V
    out = jnp.zeros(o_ref.shape, F32)
    for h in range(MLA_HEADS):
        sl = slice(h * LANES, (h + 1) * LANES)
        o = _softmax_pv(q_ref[:, sl], k_ref[:, sl], v)
        out = jnp.where(head_of_lane == h, o, out)
    o_ref[...] = out.astype(BF16)


def _gqa_attn_kernel(q_ref, k_ref, v_ref, o_ref):
    k = k_ref[...]
    v = v_ref[...]
    low = lax.broadcasted_iota(jnp.int32, (q_ref.shape[0], LANES), 1) < GQA_HEAD_DIM
    for j in range(GQA_HEADS // GQA_KV_HEADS):
        sl = slice(j * LANES, (j + 1) * LANES)
        q = q_ref[:, sl]
        zero = jnp.zeros_like(q)
        o_lo = _softmax_pv(jnp.where(low, q, zero), k, v)
        o_hi = _softmax_pv(jnp.where(low, zero, q), k, v)
        o_ref[:, sl] = jnp.where(low, o_lo, o_hi).astype(BF16)


def _attn_call(kernel, q, k, v, out_width, name):
    B, S, _ = q.shape
    tq = Q_TILE
    return pl.pallas_call(
        kernel,
        grid=(B, S // tq),
        in_specs=[pl.BlockSpec((None, tq, q.shape[2]), lambda b, i: (b, i, 0)),
                  pl.BlockSpec((None, S, k.shape[2]), lambda b, i: (b, 0, 0)),
                  pl.BlockSpec((None, S, v.shape[2]), lambda b, i: (b, 0, 0))],
        out_specs=pl.BlockSpec((None, tq, out_width), lambda b, i: (b, i, 0)),
        out_shape=jax.ShapeDtypeStruct((B, S, out_width), BF16),
        compiler_params=pltpu.CompilerParams(dimension_semantics=("parallel", "parallel"),
                                             vmem_limit_bytes=VMEM_LIMIT),
        name=name,
    )(q, k, v)


def _merge_kernel(x_ref, mod_ref, ona_ref, omla_ref, ogqa_ref, gate_ref, wa_ref, wb_ref, wc_ref, wo_ref,
                  o_ref):
    D = D_MODEL
    m = (gate_ref[:, 0:D].astype(F32) * _dot(ona_ref[...], wa_ref[...])
         + gate_ref[:, D:2 * D].astype(F32) * _dot(omla_ref[...], wb_ref[...])
         + gate_ref[:, 2 * D:3 * D].astype(F32) * _dot(ogqa_ref[...], wc_ref[...]))
    y = _dot(m.astype(BF16), wo_ref[...])
    o_ref[...] = x_ref[...] + mod_ref[2:3, :] * y


def _merge_call(x, mod, ona, omla, ogqa, gates, wa, wb, wc, wo):
    B, S, D = x.shape
    tm = TOKEN_TILE
    tok = lambda a: pl.BlockSpec((None, tm, a.shape[2]), lambda b, i: (b, i, 0))
    full = lambda a: pl.BlockSpec(a.shape, lambda b, i: (0,) * a.ndim)
    return pl.pallas_call(
        _merge_kernel,
        grid=(B, S // tm),
        in_specs=[tok(x), pl.BlockSpec((None, N_MOD, D), lambda b, i: (b, 0, 0)), tok(ona), tok(omla),
                  tok(ogqa), tok(gates), full(wa), full(wb), full(wc), full(wo)],
        out_specs=tok(x),
        out_shape=jax.ShapeDtypeStruct(x.shape, F32),
        compiler_params=pltpu.CompilerParams(dimension_semantics=("parallel", "parallel"),
                                             vmem_limit_bytes=VMEM_LIMIT),
        name="merge_out",
    )(x, mod, ona, omla, ogqa, gates, wa, wb, wc, wo)


def _ffn_kernel(x_ref, mod_ref, g_ref, wgu_ref, wd_ref, fg_ref, o_ref, *, final_norm):
    x = x_ref[...]
    h = _rms(x) * g_ref[...] * (1.0 + mod_ref[4:5, :]) + mod_ref[3:4, :]
    hb = h.astype(BF16)
    acc = jnp.zeros(x.shape, F32)
    for c0 in range(0, FFN_HIDDEN, FFN_CHUNK):
        gate = _dot(hb, wgu_ref[:, c0:c0 + FFN_CHUNK])
        up = _dot(hb, wgu_ref[:, FFN_HIDDEN + c0:FFN_HIDDEN + c0 + FFN_CHUNK])
        act = (gate * _sigmoid(gate) * up).astype(BF16)
        acc = acc + _dot(act, wd_ref[c0:c0 + FFN_CHUNK, :])
    y = x + mod_ref[5:6, :] * acc
    if final_norm:
        y = _rms(y) * fg_ref[...]
    o_ref[...] = y


def _ffn_call(x, mod, g, wgu, wd, fg, final_norm):
    B, S, D = x.shape
    tm = TOKEN_TILE
    tok = pl.BlockSpec((None, tm, D), lambda b, i: (b, i, 0))
    full = lambda a: pl.BlockSpec(a.shape, lambda b, i: (0,) * a.ndim)
    return pl.pallas_call(
        functools.partial(_ffn_kernel, final_norm=final_norm),
        grid=(B, S // tm),
        in_specs=[tok, pl.BlockSpec((None, N_MOD, D), lambda b, i: (b, 0, 0)), full(g), full(wgu), full(wd),
                  full(fg)],
        out_specs=tok,
        out_shape=jax.ShapeDtypeStruct(x.shape, F32),
        compiler_params=pltpu.CompilerParams(dimension_semantics=("parallel", "parallel"),
                                             vmem_limit_bytes=VMEM_LIMIT),
        name="swiglu",
    )(x, mod, g, wgu, wd, fg)


def _rope_tables(S):
    t = jnp.arange(S, dtype=jnp.int32)
    pos_t = t.astype(F32)
    pos_row = (t // GRID_W).astype(F32)
    pos_col = (t % GRID_W).astype(F32)
    half = 16
    lane = np.arange(LANES)

    def tables(pos_of_lane, roped, first_half, inv_freq):
        ang = pos_of_lane * inv_freq[lane % half][None, :]
        cos, sin = jnp.cos(ang), jnp.sin(ang)
        c = jnp.where(roped[None, :], cos, 1.0)
        up = jnp.where((roped & first_half)[None, :], -sin, 0.0)
        dn = jnp.where((roped & ~first_half)[None, :], sin, 0.0)
        return c.astype(F32), up.astype(F32), dn.astype(F32)

    inv_t = 1.0 / (ROPE_THETA ** (jnp.arange(half, dtype=F32) / half))
    inv_a = 1.0 / (AXIAL_THETA ** (jnp.arange(half, dtype=F32) / half))
    m_roped = (lane >= MLA_NOPE) & (lane < MLA_NOPE + MLA_ROPE)
    m_first = ((lane - MLA_NOPE) % 32) < half
    mt = tables(jnp.broadcast_to(pos_t[:, None], (S, LANES)), m_roped, m_first, inv_t)
    a_first = (lane % 32) < half
    use_row = (lane % GRID_W) < 32
    pos_rc = jnp.where(use_row[None, :], pos_row[:, None], pos_col[:, None])
    at = tables(pos_rc, np.ones(LANES, bool), a_first, inv_a)
    return mt, at


def _na_bias_table(rpb):
    d = np.arange(NA_KH)[:, None, None, None, None]
    h = np.arange(NA_HEADS)[None, :, None, None, None]
    c = np.arange(GRID_W)[None, None, :, None, None]
    kk = np.arange(NA_KH)[None, None, None, :, None]
    kc = np.arange(GRID_W)[None, None, None, None, :]
    ro = kk - d + (NA_KH - 1)
    co = np.clip(kc - c, -(NA_KW - 1), NA_KW - 1) + (NA_KW - 1)
    ws = np.clip(c - NA_KW // 2, 0, GRID_W - NA_KW)
    valid = (kc >= ws) & (kc < ws + NA_KW)
    shape = (NA_KH, NA_HEADS, GRID_W, NA_KH, GRID_W)
    tbl = rpb[np.broadcast_to(h, shape), np.broadcast_to(ro, shape), np.broadcast_to(co, shape)]
    tbl = jnp.where(np.broadcast_to(valid, shape), tbl.astype(F32), MASK_VALUE)
    return tbl.reshape(NA_KH, NA_HEADS * GRID_W, NA_KH * GRID_W)


def _prep_w_in(w):
    z = lambda n: jnp.zeros((w.shape[0], n), w.dtype)
    qc0 = 1184
    qslabs = []
    for j in range(4):
        qslabs += [w[:, qc0 + 64 * j:qc0 + 64 * (j + 1)], w[:, qc0 + 64 * (4 + j):qc0 + 64 * (5 + j)]]
    cols = [w[:, 0:768], w[:, 768:1024], w[:, 1024:1152], z(64), w[:, 1152:1184], z(32)] + qslabs + [
        w[:, 1696:1824], w[:, 1824:1952], w[:, 1952:5024]]
    return jnp.concatenate(cols, axis=1).astype(BF16)


def _prep_w_uq(w):
    z = jnp.zeros((w.shape[0], 32), w.dtype)
    hd = MLA_NOPE + MLA_ROPE
    return jnp.concatenate([a for h in range(MLA_HEADS) for a in (w[:, hd * h:hd * (h + 1)], z)],
                           axis=1).astype(BF16)


def _prep_w_ukv(w):
    z = jnp.zeros((w.shape[0], 64), w.dtype)
    hd = MLA_NOPE + MLA_V
    ks = [a for h in range(MLA_HEADS) for a in (w[:, hd * h:hd * h + MLA_NOPE], z)]
    vs = [w[:, hd * h + MLA_NOPE:hd * (h + 1)] for h in range(MLA_HEADS)]
    return jnp.concatenate(ks + vs, axis=1).astype(BF16)


def _prep_w_br_gqa(w):
    rows = []
    for j in range(4):
        rows += [w[64 * j:64 * (j + 1)], w[64 * (4 + j):64 * (5 + j)]]
    return jnp.concatenate(rows, axis=0).astype(BF16)


def kernel(x, c, ada_w, ada_b, norm_mix_g, norm_ffn_g, w_in, na_rpb, mla_q_norm_g, mla_kv_norm_g, mla_w_uq,
           mla_w_ukv, gqa_q_norm_g, gqa_k_norm_g, w_br_na, w_br_mla, w_br_gqa, w_out, ffn_w_gate_up,
           ffn_w_down, final_norm_g):
    B, S, D = x.shape
    L = ada_w.shape[0]
    mod = _ada_call(c, ada_w, ada_b).reshape(L, B, N_MOD, D)
    mtabs, atabs = _rope_tables(S)
    lane = np.arange(LANES)
    bd = jnp.asarray((lane[:, None] // 64 == lane[None, :] // 64) / 64.0, BF16)
    fg = final_norm_g.reshape(1, D)
    for l in range(L):
        gqk = jnp.concatenate([jnp.tile(gqa_q_norm_g[l], GQA_HEADS), jnp.tile(gqa_k_norm_g[l], GQA_KV_HEADS)])
        qna, kna, vna, qm, km, vm, qg, kg, vg, gates = _in_call(
            x, mod[l], norm_mix_g[l].reshape(1, D), _prep_w_in(w_in[l]),
            mla_q_norm_g[l].reshape(1, -1), mla_kv_norm_g[l].reshape(1, -1),
            _prep_w_uq(mla_w_uq[l]), _prep_w_ukv(mla_w_ukv[l]), gqk.reshape(1, -1), bd, mtabs, atabs)
        o_na = _na_call(qna, kna, vna, _na_bias_table(na_rpb[l]))
        o_mla = _attn_call(_mla_attn_kernel, qm, km, vm, MLA_HEADS * MLA_V, "mla_attn")
        o_gqa = _attn_call(_gqa_attn_kernel, qg, kg, vg, GQA_HEADS * GQA_HEAD_DIM, "gqa_attn")
        x = _merge_call(x, mod[l], o_na, o_mla, o_gqa, gates, w_br_na[l].astype(BF16),
                        w_br_mla[l].astype(BF16), _prep_w_br_gqa(w_br_gqa[l]), w_out[l].astype(BF16))
        x = _ffn_call(x, mod[l], norm_ffn_g[l].reshape(1, D), ffn_w_gate_up[l].astype(BF16),
                      ffn_w_down[l].astype(BF16), fg, final_norm=(l == L - 1))
    return x
```

```python
import functools

import numpy as np
import jax
import jax.numpy as jnp
from jax import lax
from jax.experimental import pallas as pl
from jax.experimental.pallas import tpu as pltpu

D_MODEL = 1024
GRID_W = 64
NA_HEADS = 4
NA_HEAD_DIM = 64
NA_KH = 8
NA_KW = 16
MLA_HEADS = 4
MLA_Q_LORA = 256
MLA_KV_LORA = 128
MLA_NOPE = 64
MLA_ROPE = 32
MLA_V = 64
GQA_HEADS = 8
GQA_KV_HEADS = 2
GQA_HEAD_DIM = 64
FFN_HIDDEN = 2816
ROPE_THETA = 10000.0
AXIAL_THETA = 10000.0
NORM_EPS = 1e-6
N_MOD = 6
MASK_VALUE = -1e30

LANES = 128
VMEM_LIMIT = 56 * 1024 * 1024
TOKEN_TILE = 256
Q_TILE = 256
FFN_CHUNK = 1408

C_NA = 0
C_MLA = 768
C_GQA = 1280
C_GATE = 2048
C_TOTAL = 5120

F32 = jnp.float32
BF16 = jnp.bfloat16


def _dot(a, b):
    return jnp.dot(a, b, preferred_element_type=F32)


def _dot_nt(a, b):
    return lax.dot_general(a, b, (((1,), (1,)), ((), ())), preferred_element_type=F32)


def _rms(x):
    return x * lax.rsqrt(jnp.mean(x * x, axis=-1, keepdims=True) + NORM_EPS)


def _rope_slab(x, c, s_up, s_dn):
    return x * c + pltpu.roll(x, LANES - 16, 1) * s_up + pltpu.roll(x, 16, 1) * s_dn


def _sigmoid(x):
    return 1.0 / (1.0 + jnp.exp(-x))


def _ada_kernel(c_ref, w_ref, b_ref, o_ref):
    c = c_ref[...]
    s = (c * _sigmoid(c)).astype(BF16)
    o_ref[...] = _dot(s, w_ref[...].astype(BF16)) + b_ref[...]


def _ada_call(c, ada_w, ada_b):
    L, D, N = ada_w.shape
    B = c.shape[0]
    tn = 1536
    return pl.pallas_call(
        _ada_kernel,
        grid=(L, N // tn),
        in_specs=[pl.BlockSpec((B, D), lambda l, j: (0, 0)),
                  pl.BlockSpec((None, D, tn), lambda l, j: (l, 0, j)),
                  pl.BlockSpec((None, 1, tn), lambda l, j: (l, 0, j))],
        out_specs=pl.BlockSpec((None, B, tn), lambda l, j: (l, 0, j)),
        out_shape=jax.ShapeDtypeStruct((L, B, N), F32),
        compiler_params=pltpu.CompilerParams(dimension_semantics=("parallel", "parallel"),
                                             vmem_limit_bytes=VMEM_LIMIT),
        name="ada_mod",
    )(c, ada_w, ada_b.reshape(L, 1, N))


def _in_kernel(x_ref, mod_ref, g_ref, w_ref, qng_ref, kvng_ref, wuq_ref, wukv_ref, gqk_ref, bd_ref,
               mc_ref, mu_ref, md_ref, ac_ref, au_ref, ad_ref,
               qna_ref, kna_ref, vna_ref, qm_ref, km_ref, vm_ref, qg_ref, kg_ref, vg_ref, gate_ref):
    x = x_ref[...]
    h = _rms(x) * g_ref[...] * (1.0 + mod_ref[1:2, :]) + mod_ref[0:1, :]
    hb = h.astype(BF16)

    def proj(a, b):
        return _dot(hb, w_ref[:, a:b])

    p = proj(C_NA, C_NA + 768)
    qna_ref[...] = (p[:, :256] * NA_HEAD_DIM ** -0.5).astype(BF16)
    kna_ref[...] = p[:, 256:512].astype(BF16)
    vna_ref[...] = p[:, 512:768].astype(BF16)

    p = proj(C_MLA, C_MLA + 512)
    mc, mu, md = mc_ref[...], mu_ref[...], md_ref[...]
    nq = (_rms(p[:, :256]) * qng_ref[...]).astype(BF16)
    qb = _dot(nq, wuq_ref[...])
    mla_scale = (MLA_NOPE + MLA_ROPE) ** -0.5
    for s in range(MLA_HEADS):
        sl = slice(s * LANES, (s + 1) * LANES)
        qm_ref[:, sl] = (_rope_slab(qb[:, sl], mc, mu, md) * mla_scale).astype(BF16)
    nkv = (_rms(p[:, 256:384]) * kvng_ref[...]).astype(BF16)
    kvb = _dot(nkv, wukv_ref[...])
    kpe = _rope_slab(p[:, 384:512], mc, mu, md)
    for s in range(MLA_HEADS):
        sl = slice(s * LANES, (s + 1) * LANES)
        km_ref[:, sl] = (kvb[:, sl] + kpe).astype(BF16)
    vm_ref[...] = kvb[:, 512:768].astype(BF16)

    p = proj(C_GQA, C_GQA + 768)
    ac, au, ad = ac_ref[...], au_ref[...], ad_ref[...]
    bd = bd_ref[...]
    for s in range(5):
        sl = slice(s * LANES, (s + 1) * LANES)
        v = p[:, sl]
        sq = v * v
        hi = sq.astype(BF16)
        lo = (sq - hi.astype(F32)).astype(BF16)
        msq = _dot(hi, bd) + _dot(lo, bd)
        y = v * lax.rsqrt(msq + NORM_EPS) * gqk_ref[:, sl]
        r = _rope_slab(y, ac, au, ad)
        if s < 4:
            qg_ref[:, sl] = (r * GQA_HEAD_DIM ** -0.5).astype(BF16)
        else:
            kg_ref[...] = r.astype(BF16)
    vg_ref[...] = p[:, 640:768].astype(BF16)

    for j in range(3):
        g = proj(C_GATE + j * D_MODEL, C_GATE + (j + 1) * D_MODEL)
        gate_ref[:, j * D_MODEL:(j + 1) * D_MODEL] = _sigmoid(g).astype(BF16)


def _in_call(x, mod, g, w, qng, kvng, wuq, wukv, gqk, bd, mtabs, atabs):
    B, S, D = x.shape
    tm = TOKEN_TILE
    tok = lambda w_: pl.BlockSpec((None, tm, w_), lambda b, i: (b, i, 0))
    full = lambda a: pl.BlockSpec(a.shape, lambda b, i: (0,) * a.ndim)
    tab = pl.BlockSpec((tm, LANES), lambda b, i: (i, 0))
    widths = (256, 256, 256, 512, 512, 256, 512, 128, 128, 3 * D)
    return pl.pallas_call(
        _in_kernel,
        grid=(B, S // tm),
        in_specs=[tok(D), pl.BlockSpec((None, N_MOD, D), lambda b, i: (b, 0, 0)), full(g), full(w),
                  full(qng), full(kvng), full(wuq), full(wukv), full(gqk), full(bd)] + [tab] * 6,
        out_specs=[tok(w_) for w_ in widths],
        out_shape=[jax.ShapeDtypeStruct((B, S, w_), BF16) for w_ in widths],
        compiler_params=pltpu.CompilerParams(dimension_semantics=("parallel", "parallel"),
                                             vmem_limit_bytes=VMEM_LIMIT),
        name="in_proj",
    )(x, mod, g, w, qng, kvng, wuq, wukv, gqk, bd, *mtabs, *atabs)


def _na_kernel(q_ref, k_ref, v_ref, bias_ref, o_ref, *, rows):
    kh = NA_KH
    head_of_lane = lax.broadcasted_iota(jnp.int32, (GRID_W, NA_HEADS * NA_HEAD_DIM), 1) // NA_HEAD_DIM

    def body(r, carry):
        rs = jnp.clip(r - kh // 2, 0, rows - kh)
        q = q_ref[pl.ds(pl.multiple_of(r * GRID_W, GRID_W), GRID_W), :]
        kstart = pl.multiple_of(rs * GRID_W, GRID_W)
        kb = k_ref[pl.ds(kstart, kh * GRID_W), :]
        vb = v_ref[pl.ds(kstart, kh * GRID_W), :]
        qs = jnp.concatenate([jnp.where(head_of_lane == h, q, jnp.zeros_like(q)) for h in range(NA_HEADS)],
                             axis=0)
        s = _dot_nt(qs, kb) + bias_ref[r - rs]
        m = jnp.max(s, axis=-1, keepdims=True)
        p = jnp.exp(s - m)
        l = jnp.sum(p, axis=-1, keepdims=True)
        o = _dot(p.astype(BF16), vb) / l
        out = jnp.zeros((GRID_W, NA_HEADS * NA_HEAD_DIM), F32)
        for h in range(NA_HEADS):
            out = jnp.where(head_of_lane == h, o[h * GRID_W:(h + 1) * GRID_W, :], out)
        o_ref[pl.ds(pl.multiple_of(r * GRID_W, GRID_W), GRID_W), :] = out.astype(BF16)
        return carry

    lax.fori_loop(0, rows, body, 0)


def _na_call(q, k, v, bias):
    B, S, W = q.shape
    rows = S // GRID_W
    blk = pl.BlockSpec((None, S, W), lambda b: (b, 0, 0))
    return pl.pallas_call(
        functools.partial(_na_kernel, rows=rows),
        grid=(B,),
        in_specs=[blk, blk, blk, pl.BlockSpec(bias.shape, lambda b: (0, 0, 0))],
        out_specs=blk,
        out_shape=jax.ShapeDtypeStruct((B, S, W), BF16),
        compiler_params=pltpu.CompilerParams(dimension_semantics=("parallel",),
                                             vmem_limit_bytes=VMEM_LIMIT),
        name="na_attn",
    )(q, k, v, bias)


def _softmax_pv(q, k, v):
    s = _dot_nt(q, k)
    m = jnp.max(s, axis=-1, keepdims=True)
    p = jnp.exp(s - m)
    l = jnp.sum(p, axis=-1, keepdims=True)
    return _dot(p.astype(BF16), v) / l


def _mla_attn_kernel(q_ref, k_ref, v_ref, o_ref):
    v = v_ref[...]
    head_of_lane = lax.broadcasted_iota(jnp.int32, o_ref.shape, 1) // MLA_V
    out = jnp.zeros(o_ref.shape, F32)
    for h in range(MLA_HEADS):
        sl = slice(h * LANES, (h + 1) * LANES)
        o = _softmax_pv(q_ref[:, sl], k_ref[:, sl], v)
        out = jnp.where(head_of_lane == h, o, out)
    o_ref[...] = out.astype(BF16)


def _gqa_attn_kernel(q_ref, k_ref, v_ref, o_ref):
    k = k_ref[...]
    v = v_ref[...]
    low = lax.broadcasted_iota(jnp.int32, (q_ref.shape[0], LANES), 1) < GQA_HEAD_DIM
    for j in range(GQA_HEADS // GQA_KV_HEADS):
        sl = slice(j * LANES, (j + 1) * LANES)
        q = q_ref[:, sl]
        zero = jnp.zeros_like(q)
        o_lo = _softmax_pv(jnp.where(low, q, zero), k, v)
        o_hi = _softmax_pv(jnp.where(low, zero, q), k, v)
        o_ref[:, sl] = jnp.where(low, o_lo, o_hi).astype(BF16)


def _attn_call(kernel, q, k, v, out_width, name):
    B, S, _ = q.shape
    tq = Q_TILE
    return pl.pallas_call(
        kernel,
        grid=(B, S // tq),
        in_specs=[pl.BlockSpec((None, tq, q.shape[2]), lambda b, i: (b, i, 0)),
                  pl.BlockSpec((None, S, k.shape[2]), lambda b, i: (b, 0, 0)),
                  pl.BlockSpec((None, S, v.shape[2]), lambda b, i: (b, 0, 0))],
        out_specs=pl.BlockSpec((None, tq, out_width), lambda b, i: (b, i, 0)),
        out_shape=jax.ShapeDtypeStruct((B, S, out_width), BF16),
        compiler_params=pltpu.CompilerParams(dimension_semantics=("parallel", "parallel"),
                                             vmem_limit_bytes=VMEM_LIMIT),
        name=name,
    )(q, k, v)


def _merge_kernel(x_ref, mod_ref, ona_ref, omla_ref, ogqa_ref, gate_ref, wa_ref, wb_ref, wc_ref, wo_ref,
                  o_ref):
    D = D_MODEL
    m = (gate_ref[:, 0:D].astype(F32) * _dot(ona_ref[...], wa_ref[...])
         + gate_ref[:, D:2 * D].astype(F32) * _dot(omla_ref[...], wb_ref[...])
         + gate_ref[:, 2 * D:3 * D].astype(F32) * _dot(ogqa_ref[...], wc_ref[...]))
    y = _dot(m.astype(BF16), wo_ref[...])
    o_ref[...] = x_ref[...] + mod_ref[2:3, :] * y


def _merge_call(x, mod, ona, omla, ogqa, gates, wa, wb, wc, wo):
    B, S, D = x.shape
    tm = TOKEN_TILE
    tok = lambda a: pl.BlockSpec((None, tm, a.shape[2]), lambda b, i: (b, i, 0))
    full = lambda a: pl.BlockSpec(a.shape, lambda b, i: (0,) * a.ndim)
    return pl.pallas_call(
        _merge_kernel,
        grid=(B, S // tm),
        in_specs=[tok(x), pl.BlockSpec((None, N_MOD, D), lambda b, i: (b, 0, 0)), tok(ona), tok(omla),
                  tok(ogqa), tok(gates), full(wa), full(wb), full(wc), full(wo)],
        out_specs=tok(x),
        out_shape=jax.ShapeDtypeStruct(x.shape, F32),
        compiler_params=pltpu.CompilerParams(dimension_semantics=("parallel", "parallel"),
                                             vmem_limit_bytes=VMEM_LIMIT),
        name="merge_out",
    )(x, mod, ona, omla, ogqa, gates, wa, wb, wc, wo)


def _ffn_kernel(x_ref, mod_ref, g_ref, wgu_ref, wd_ref, fg_ref, o_ref, *, final_norm):
    x = x_ref[...]
    h = _rms(x) * g_ref[...] * (1.0 + mod_ref[4:5, :]) + mod_ref[3:4, :]
    hb = h.astype(BF16)
    acc = jnp.zeros(x.shape, F32)
    for c0 in range(0, FFN_HIDDEN, FFN_CHUNK):
        gate = _dot(hb, wgu_ref[:, c0:c0 + FFN_CHUNK])
        up = _dot(hb, wgu_ref[:, FFN_HIDDEN + c0:FFN_HIDDEN + c0 + FFN_CHUNK])
        act = (gate * _sigmoid(gate) * up).astype(BF16)
        acc = acc + _dot(act, wd_ref[c0:c0 + FFN_CHUNK, :])
    y = x + mod_ref[5:6, :] * acc
    if final_norm:
        y = _rms(y) * fg_ref[...]
    o_ref[...] = y


def _ffn_call(x, mod, g, wgu, wd, fg, final_norm):
    B, S, D = x.shape
    tm = TOKEN_TILE
    tok = pl.BlockSpec((None, tm, D), lambda b, i: (b, i, 0))
    full = lambda a: pl.BlockSpec(a.shape, lambda b, i: (0,) * a.ndim)
    return pl.pallas_call(
        functools.partial(_ffn_kernel, final_norm=final_norm),
        grid=(B, S // tm),
        in_specs=[tok, pl.BlockSpec((None, N_MOD, D), lambda b, i: (b, 0, 0)), full(g), full(wgu), full(wd),
                  full(fg)],
        out_specs=tok,
        out_shape=jax.ShapeDtypeStruct(x.shape, F32),
        compiler_params=pltpu.CompilerParams(dimension_semantics=("parallel", "parallel"),
                                             vmem_limit_bytes=VMEM_LIMIT),
        name="swiglu",
    )(x, mod, g, wgu, wd, fg)


def _rope_tables(S):
    t = jnp.arange(S, dtype=jnp.int32)
    pos_t = t.astype(F32)
    pos_row = (t // GRID_W).astype(F32)
    pos_col = (t % GRID_W).astype(F32)
    half = 16
    lane = np.arange(LANES)

    def tables(pos_of_lane, roped, first_half, inv_freq):
        ang = pos_of_lane * inv_freq[lane % half][None, :]
        cos, sin = jnp.cos(ang), jnp.sin(ang)
        c = jnp.where(roped[None, :], cos, 1.0)
        up = jnp.where((roped & first_half)[None, :], -sin, 0.0)
        dn = jnp.where((roped & ~first_half)[None, :], sin, 0.0)
        return c.astype(F32), up.astype(F32), dn.astype(F32)

    inv_t = 1.0 / (ROPE_THETA ** (jnp.arange(half, dtype=F32) / half))
    inv_a = 1.0 / (AXIAL_THETA ** (jnp.arange(half, dtype=F32) / half))
    m_roped = (lane >= MLA_NOPE) & (lane < MLA_NOPE + MLA_ROPE)
    m_first = ((lane - MLA_NOPE) % 32) < half
    mt = tables(jnp.broadcast_to(pos_t[:, None], (S, LANES)), m_roped, m_first, inv_t)
    a_first = (lane % 32) < half
    use_row = (lane % GRID_W) < 32
    pos_rc = jnp.where(use_row[None, :], pos_row[:, None], pos_col[:, None])
    at = tables(pos_rc, np.ones(LANES, bool), a_first, inv_a)
    return mt, at


def _na_bias_table(rpb):
    W = GRID_W
    pad = W - NA_KW
    ext = jnp.concatenate([jnp.repeat(rpb[..., :1], pad, -1), rpb, jnp.repeat(rpb[..., -1:], pad, -1)], -1)
    t = jnp.stack([ext[..., W - 1 - c:2 * W - 1 - c] for c in range(W)], axis=2)
    c = np.arange(W)[:, None, None]
    kc = np.arange(W)[None, None, :]
    ws = np.clip(c - NA_KW // 2, 0, W - NA_KW)
    valid = (kc >= ws) & (kc < ws + NA_KW)
    out = []
    for d in range(NA_KH):
        td = jnp.transpose(t[:, NA_KH - 1 - d:2 * NA_KH - 1 - d], (0, 2, 1, 3))
        out.append(jnp.where(valid[None], td.astype(F32), MASK_VALUE))
    return jnp.stack(out).reshape(NA_KH, NA_HEADS * W, NA_KH * W)


def _prep_w_in(w):
    z = lambda n: jnp.zeros((w.shape[0], n), w.dtype)
    qc0 = 1184
    qslabs = []
    for j in range(4):
        qslabs += [w[:, qc0 + 64 * j:qc0 + 64 * (j + 1)], w[:, qc0 + 64 * (4 + j):qc0 + 64 * (5 + j)]]
    cols = [w[:, 0:768], w[:, 768:1024], w[:, 1024:1152], z(64), w[:, 1152:1184], z(32)] + qslabs + [
        w[:, 1696:1824], w[:, 1824:1952], w[:, 1952:5024]]
    return jnp.concatenate(cols, axis=1).astype(BF16)


def _prep_w_uq(w):
    z = jnp.zeros((w.shape[0], 32), w.dtype)
    hd = MLA_NOPE + MLA_ROPE
    return jnp.concatenate([a for h in range(MLA_HEADS) for a in (w[:, hd * h:hd * (h + 1)], z)],
                           axis=1).astype(BF16)


def _prep_w_ukv(w):
    z = jnp.zeros((w.shape[0], 64), w.dtype)
    hd = MLA_NOPE + MLA_V
    ks = [a for h in range(MLA_HEADS) for a in (w[:, hd * h:hd * h + MLA_NOPE], z)]
    vs = [w[:, hd * h + MLA_NOPE:hd * (h + 1)] for h in range(MLA_HEADS)]
    return jnp.concatenate(ks + vs, axis=1).astype(BF16)


def _prep_w_br_gqa(w):
    rows = []
    for j in range(4):
        rows += [w[64 * j:64 * (j + 1)], w[64 * (4 + j):64 * (5 + j)]]
    return jnp.concatenate(rows, axis=0).astype(BF16)


def kernel(x, c, ada_w, ada_b, norm_mix_g, norm_ffn_g, w_in, na_rpb, mla_q_norm_g, mla_kv_norm_g, mla_w_uq,
           mla_w_ukv, gqa_q_norm_g, gqa_k_norm_g, w_br_na, w_br_mla, w_br_gqa, w_out, ffn_w_gate_up,
           ffn_w_down, final_norm_g):
    B, S, D = x.shape
    L = ada_w.shape[0]
    mod = _ada_call(c, ada_w, ada_b).reshape(L, B, N_MOD, D)
    mtabs, atabs = _rope_tables(S)
    lane = np.arange(LANES)
    bd = jnp.asarray((lane[:, None] // 64 == lane[None, :] // 64) / 64.0, BF16)
    fg = final_norm_g.reshape(1, D)
    for l in range(L):
        gqk = jnp.concatenate([jnp.tile(gqa_q_norm_g[l], GQA_HEADS), jnp.tile(gqa_k_norm_g[l], GQA_KV_HEADS)])
        qna, kna, vna, qm, km, vm, qg, kg, vg, gates = _in_call(
            x, mod[l], norm_mix_g[l].reshape(1, D), _prep_w_in(w_in[l]),
            mla_q_norm_g[l].reshape(1, -1), mla_kv_norm_g[l].reshape(1, -1),
            _prep_w_uq(mla_w_uq[l]), _prep_w_ukv(mla_w_ukv[l]), gqk.reshape(1, -1), bd, mtabs, atabs)
        o_na = _na_call(qna, kna, vna, _na_bias_table(na_rpb[l]))
        o_mla = _attn_call(_mla_attn_kernel, qm, km, vm, MLA_HEADS * MLA_V, "mla_attn")
        o_gqa = _attn_call(_gqa_attn_kernel, qg, kg, vg, GQA_HEADS * GQA_HEAD_DIM, "gqa_attn")
        x = _merge_call(x, mod[l], o_na, o_mla, o_gqa, gates, w_br_na[l].astype(BF16),
                        w_br_mla[l].astype(BF16), _prep_w_br_gqa(w_br_gqa[l]), w_out[l].astype(BF16))
        x = _ffn_call(x, mod[l], norm_ffn_g[l].reshape(1, D), ffn_w_gate_up[l].astype(BF16),
                      ffn_w_down[l].astype(BF16), fg, final_norm=(l == L - 1))
    return x
```

```python
import functools

import numpy as np
import jax
import jax.numpy as jnp
from jax import lax
from jax.experimental import pallas as pl
from jax.experimental.pallas import tpu as pltpu

D_MODEL = 1024
GRID_W = 64
NA_HEADS = 4
NA_HEAD_DIM = 64
NA_KH = 8
NA_KW = 16
MLA_HEADS = 4
MLA_Q_LORA = 256
MLA_KV_LORA = 128
MLA_NOPE = 64
MLA_ROPE = 32
MLA_V = 64
GQA_HEADS = 8
GQA_KV_HEADS = 2
GQA_HEAD_DIM = 64
FFN_HIDDEN = 2816
ROPE_THETA = 10000.0
AXIAL_THETA = 10000.0
NORM_EPS = 1e-6
N_MOD = 6
MASK_VALUE = -1e30
LOG2E = 1.4426950408889634

LANES = 128
VMEM_LIMIT = 56 * 1024 * 1024
TOKEN_TILE = 512
Q_TILE = 256
FFN_CHUNK = 1408

C_NA = 0
C_MLA = 768
C_GQA = 1280
C_GATE = 2048
C_TOTAL = 5120

F32 = jnp.float32
BF16 = jnp.bfloat16


def _dot(a, b):
    return jnp.dot(a, b, preferred_element_type=F32)


def _dot_nt(a, b):
    return lax.dot_general(a, b, (((1,), (1,)), ((), ())), preferred_element_type=F32)


def _rms(x):
    return x * lax.rsqrt(jnp.mean(x * x, axis=-1, keepdims=True) + NORM_EPS)


def _rope_slab(x, c, s_up, s_dn):
    return x * c + pltpu.roll(x, LANES - 16, 1) * s_up + pltpu.roll(x, 16, 1) * s_dn


def _sigmoid(x):
    return 1.0 / (1.0 + jnp.exp(-x))


def _ada_kernel(c_ref, w_ref, b_ref, o_ref):
    c = c_ref[...]
    s = (c * _sigmoid(c)).astype(BF16)
    o_ref[...] = _dot(s, w_ref[...].astype(BF16)) + b_ref[...]


def _ada_call(c, ada_w, ada_b):
    L, D, N = ada_w.shape
    B = c.shape[0]
    tn = 1536
    return pl.pallas_call(
        _ada_kernel,
        grid=(L, N // tn),
        in_specs=[pl.BlockSpec((B, D), lambda l, j: (0, 0)),
                  pl.BlockSpec((None, D, tn), lambda l, j: (l, 0, j)),
                  pl.BlockSpec((None, 1, tn), lambda l, j: (l, 0, j))],
        out_specs=pl.BlockSpec((None, B, tn), lambda l, j: (l, 0, j)),
        out_shape=jax.ShapeDtypeStruct((L, B, N), F32),
        compiler_params=pltpu.CompilerParams(dimension_semantics=("parallel", "parallel"),
                                             vmem_limit_bytes=VMEM_LIMIT),
        name="ada_mod",
    )(c, ada_w, ada_b.reshape(L, 1, N))


def _in_kernel(x_ref, mod_ref, g_ref, w_ref, qng_ref, kvng_ref, wuq_ref, wukv_ref, gqk_ref, bd_ref,
               mc_ref, mu_ref, md_ref, ac_ref, au_ref, ad_ref,
               qna_ref, kna_ref, vna_ref, qm_ref, km_ref, vm_ref, qg_ref, kg_ref, vg_ref, gate_ref):
    x = x_ref[...]
    h = _rms(x) * g_ref[...] * (1.0 + mod_ref[1:2, :]) + mod_ref[0:1, :]
    hb = h.astype(BF16)

    def proj(a, b):
        return _dot(hb, w_ref[:, a:b])

    p = proj(C_NA, C_NA + 768)
    qna_ref[...] = (p[:, :256] * NA_HEAD_DIM ** -0.5).astype(BF16)
    kna_ref[...] = p[:, 256:512].astype(BF16)
    vna_ref[...] = p[:, 512:768].astype(BF16)

    p = proj(C_MLA, C_MLA + 512)
    mc, mu, md = mc_ref[...], mu_ref[...], md_ref[...]
    nq = (_rms(p[:, :256]) * qng_ref[...]).astype(BF16)
    qb = _dot(nq, wuq_ref[...])
    mla_scale = (MLA_NOPE + MLA_ROPE) ** -0.5 * LOG2E
    for s in range(MLA_HEADS):
        sl = slice(s * LANES, (s + 1) * LANES)
        qm_ref[:, sl] = (_rope_slab(qb[:, sl], mc, mu, md) * mla_scale).astype(BF16)
    nkv = (_rms(p[:, 256:384]) * kvng_ref[...]).astype(BF16)
    kvb = _dot(nkv, wukv_ref[...])
    kpe = _rope_slab(p[:, 384:512], mc, mu, md)
    for s in range(MLA_HEADS):
        sl = slice(s * LANES, (s + 1) * LANES)
        km_ref[:, sl] = (kvb[:, sl] + kpe).astype(BF16)
    vm_ref[...] = kvb[:, 512:768].astype(BF16)

    p = proj(C_GQA, C_GQA + 768)
    ac, au, ad = ac_ref[...], au_ref[...], ad_ref[...]
    bd = bd_ref[...]
    for s in range(5):
        sl = slice(s * LANES, (s + 1) * LANES)
        v = p[:, sl]
        sq = v * v
        hi = sq.astype(BF16)
        lo = (sq - hi.astype(F32)).astype(BF16)
        msq = _dot(hi, bd) + _dot(lo, bd)
        y = v * lax.rsqrt(msq + NORM_EPS) * gqk_ref[:, sl]
        r = _rope_slab(y, ac, au, ad)
        if s < 4:
            qg_ref[:, sl] = (r * (GQA_HEAD_DIM ** -0.5 * LOG2E)).astype(BF16)
        else:
            kg_ref[...] = r.astype(BF16)
    vg_ref[...] = p[:, 640:768].astype(BF16)

    for j in range(3):
        g = proj(C_GATE + j * D_MODEL, C_GATE + (j + 1) * D_MODEL)
        gate_ref[:, j * D_MODEL:(j + 1) * D_MODEL] = _sigmoid(g).astype(BF16)


def _in_call(x, mod, g, w, qng, kvng, wuq, wukv, gqk, bd, mtabs, atabs):
    B, S, D = x.shape
    tm = TOKEN_TILE
    tok = lambda w_: pl.BlockSpec((None, tm, w_), lambda b, i: (b, i, 0))
    full = lambda a: pl.BlockSpec(a.shape, lambda b, i: (0,) * a.ndim)
    tab = pl.BlockSpec((tm, LANES), lambda b, i: (i, 0))
    widths = (256, 256, 256, 512, 512, 256, 512, 128, 128, 3 * D)
    return pl.pallas_call(
        _in_kernel,
        grid=(B, S // tm),
        in_specs=[tok(D), pl.BlockSpec((None, N_MOD, D), lambda b, i: (b, 0, 0)), full(g), full(w),
                  full(qng), full(kvng), full(wuq), full(wukv), full(gqk), full(bd)] + [tab] * 6,
        out_specs=[tok(w_) for w_ in widths],
        out_shape=[jax.ShapeDtypeStruct((B, S, w_), BF16) for w_ in widths],
        compiler_params=pltpu.CompilerParams(dimension_semantics=("parallel", "parallel"),
                                             vmem_limit_bytes=VMEM_LIMIT),
        name="in_proj",
    )(x, mod, g, w, qng, kvng, wuq, wukv, gqk, bd, *mtabs, *atabs)


def _na_kernel(q_ref, k_ref, v_ref, bias_ref, o_ref, *, rows):
    kh = NA_KH
    head_of_lane = lax.broadcasted_iota(jnp.int32, (GRID_W, NA_HEADS * NA_HEAD_DIM), 1) // NA_HEAD_DIM

    def body(r, carry):
        rs = jnp.clip(r - kh // 2, 0, rows - kh)
        q = q_ref[pl.ds(pl.multiple_of(r * GRID_W, GRID_W), GRID_W), :]
        kstart = pl.multiple_of(rs * GRID_W, GRID_W)
        kb = k_ref[pl.ds(kstart, kh * GRID_W), :]
        vb = v_ref[pl.ds(kstart, kh * GRID_W), :]
        qs = jnp.concatenate([jnp.where(head_of_lane == h, q, jnp.zeros_like(q)) for h in range(NA_HEADS)],
                             axis=0)
        s = _dot_nt(qs, kb) + bias_ref[r - rs]
        m = jnp.max(s, axis=-1, keepdims=True)
        p = jnp.exp(s - m)
        l = jnp.sum(p, axis=-1, keepdims=True)
        o = _dot(p.astype(BF16), vb) / l
        out = jnp.zeros((GRID_W, NA_HEADS * NA_HEAD_DIM), F32)
        for h in range(NA_HEADS):
            out = jnp.where(head_of_lane == h, o[h * GRID_W:(h + 1) * GRID_W, :], out)
        o_ref[pl.ds(pl.multiple_of(r * GRID_W, GRID_W), GRID_W), :] = out.astype(BF16)
        return carry

    lax.fori_loop(0, rows, body, 0, unroll=8)


def _na_call(q, k, v, bias):
    B, S, W = q.shape
    rows = S // GRID_W
    blk = pl.BlockSpec((None, S, W), lambda b: (b, 0, 0))
    return pl.pallas_call(
        functools.partial(_na_kernel, rows=rows),
        grid=(B,),
        in_specs=[blk, blk, blk, pl.BlockSpec(bias.shape, lambda b: (0, 0, 0))],
        out_specs=blk,
        out_shape=jax.ShapeDtypeStruct((B, S, W), BF16),
        compiler_params=pltpu.CompilerParams(dimension_semantics=("parallel",),
                                             vmem_limit_bytes=VMEM_LIMIT),
        name="na_attn",
    )(q, k, v, bias)


def _softmax_pv(q, k, v):
    s = _dot_nt(q, k)
    m = jnp.max(s, axis=-1, keepdims=True)
    p = jnp.exp2(s - m)
    l = jnp.sum(p, axis=-1, keepdims=True)
    return _dot(p.astype(BF16), v) / l


def _mla_attn_kernel(q_ref, k_ref, v_ref, o_ref):
    v = v_ref[...]
    head_of_lane = lax.broadcasted_iota(jnp.int32, o_ref.shape, 1) // MLA_V
    out = jnp.zeros(o_ref.shape, F32)
    for h in range(MLA_HEADS):
        sl = slice(h * LANES, (h + 1) * LANES)
        o = _softmax_pv(q_ref[:, sl], k_ref[:, sl], v)
        out = jnp.where(head_of_lane == h, o, out)
    o_ref[...] = out.astype(BF16)


def _gqa_attn_kernel(q_ref, k_ref, v_ref, o_ref):
    k = k_ref[...]
    v = v_ref[...]
    low = lax.broadcasted_iota(jnp.int32, (q_ref.shape[0], LANES), 1) < GQA_HEAD_DIM
    for j in range(GQA_HEADS // GQA_KV_HEADS):
        sl = slice(j * LANES, (j + 1) * LANES)
        q = q_ref[:, sl]
        zero = jnp.zeros_like(q)
        o_lo = _softmax_pv(jnp.where(low, q, zero), k, v)
        o_hi = _softmax_pv(jnp.where(low, zero, q), k, v)
        o_ref[:, sl] = jnp.where(low, o_lo, o_hi).astype(BF16)


def _attn_call(kernel, q, k, v, out_width, name):
    B, S, _ = q.shape
    tq = Q_TILE
    return pl.pallas_call(
        kernel,
        grid=(B, S // tq),
        in_specs=[pl.BlockSpec((None, tq, q.shape[2]), lambda b, i: (b, i, 0)),
                  pl.BlockSpec((None, S, k.shape[2]), lambda b, i: (b, 0, 0)),
                  pl.BlockSpec((None, S, v.shape[2]), lambda b, i: (b, 0, 0))],
        out_specs=pl.BlockSpec((None, tq, out_width), lambda b, i: (b, i, 0)),
        out_shape=jax.ShapeDtypeStruct((B, S, out_width), BF16),
        compiler_params=pltpu.CompilerParams(dimension_semantics=("parallel", "parallel"),
                                             vmem_limit_bytes=VMEM_LIMIT),
        name=name,
    )(q, k, v)


def _merge_kernel(x_ref, mod_ref, ona_ref, omla_ref, ogqa_ref, gate_ref, wa_ref, wb_ref, wc_ref, wo_ref,
                  o_ref):
    D = D_MODEL
    m = (gate_ref[:, 0:D].astype(F32) * _dot(ona_ref[...], wa_ref[...])
         + gate_ref[:, D:2 * D].astype(F32) * _dot(omla_ref[...], wb_ref[...])
         + gate_ref[:, 2 * D:3 * D].astype(F32) * _dot(ogqa_ref[...], wc_ref[...]))
    y = _dot(m.astype(BF16), wo_ref[...])
    o_ref[...] = x_ref[...] + mod_ref[2:3, :] * y


def _merge_call(x, mod, ona, omla, ogqa, gates, wa, wb, wc, wo):
    B, S, D = x.shape
    tm = TOKEN_TILE
    tok = lambda a: pl.BlockSpec((None, tm, a.shape[2]), lambda b, i: (b, i, 0))
    full = lambda a: pl.BlockSpec(a.shape, lambda b, i: (0,) * a.ndim)
    return pl.pallas_call(
        _merge_kernel,
        grid=(B, S // tm),
        in_specs=[tok(x), pl.BlockSpec((None, N_MOD, D), lambda b, i: (b, 0, 0)), tok(ona), tok(omla),
                  tok(ogqa), tok(gates), full(wa), full(wb), full(wc), full(wo)],
        out_specs=tok(x),
        out_shape=jax.ShapeDtypeStruct(x.shape, F32),
        compiler_params=pltpu.CompilerParams(dimension_semantics=("parallel", "parallel"),
                                             vmem_limit_bytes=VMEM_LIMIT),
        name="merge_out",
    )(x, mod, ona, omla, ogqa, gates, wa, wb, wc, wo)


def _ffn_kernel(x_ref, mod_ref, g_ref, wgu_ref, wd_ref, fg_ref, o_ref, *, final_norm):
    x = x_ref[...]
    h = _rms(x) * g_ref[...] * (1.0 + mod_ref[4:5, :]) + mod_ref[3:4, :]
    hb = h.astype(BF16)
    acc = jnp.zeros(x.shape, F32)
    for c0 in range(0, FFN_HIDDEN, FFN_CHUNK):
        gate = _dot(hb, wgu_ref[:, c0:c0 + FFN_CHUNK])
        up = _dot(hb, wgu_ref[:, FFN_HIDDEN + c0:FFN_HIDDEN + c0 + FFN_CHUNK])
        act = (gate * _sigmoid(gate) * up).astype(BF16)
        acc = acc + _dot(act, wd_ref[c0:c0 + FFN_CHUNK, :])
    y = x + mod_ref[5:6, :] * acc
    if final_norm:
        y = _rms(y) * fg_ref[...]
    o_ref[...] = y


def _ffn_call(x, mod, g, wgu, wd, fg, final_norm):
    B, S, D = x.shape
    tm = TOKEN_TILE
    tok = pl.BlockSpec((None, tm, D), lambda b, i: (b, i, 0))
    full = lambda a: pl.BlockSpec(a.shape, lambda b, i: (0,) * a.ndim)
    return pl.pallas_call(
        functools.partial(_ffn_kernel, final_norm=final_norm),
        grid=(B, S // tm),
        in_specs=[tok, pl.BlockSpec((None, N_MOD, D), lambda b, i: (b, 0, 0)), full(g), full(wgu), full(wd),
                  full(fg)],
        out_specs=tok,
        out_shape=jax.ShapeDtypeStruct(x.shape, F32),
        compiler_params=pltpu.CompilerParams(dimension_semantics=("parallel", "parallel"),
                                             vmem_limit_bytes=VMEM_LIMIT),
        name="swiglu",
    )(x, mod, g, wgu, wd, fg)


def _rope_tables(S):
    t = jnp.arange(S, dtype=jnp.int32)
    pos_t = t.astype(F32)
    pos_row = (t // GRID_W).astype(F32)
    pos_col = (t % GRID_W).astype(F32)
    half = 16
    lane = np.arange(LANES)

    def tables(pos_of_lane, roped, first_half, inv_freq):
        ang = pos_of_lane * inv_freq[lane % half][None, :]
        cos, sin = jnp.cos(ang), jnp.sin(ang)
        c = jnp.where(roped[None, :], cos, 1.0)
        up = jnp.where((roped & first_half)[None, :], -sin, 0.0)
        dn = jnp.where((roped & ~first_half)[None, :], sin, 0.0)
        return c.astype(F32), up.astype(F32), dn.astype(F32)

    inv_t = 1.0 / (ROPE_THETA ** (jnp.arange(half, dtype=F32) / half))
    inv_a = 1.0 / (AXIAL_THETA ** (jnp.arange(half, dtype=F32) / half))
    m_roped = (lane >= MLA_NOPE) & (lane < MLA_NOPE + MLA_ROPE)
    m_first = ((lane - MLA_NOPE) % 32) < half
    mt = tables(jnp.broadcast_to(pos_t[:, None], (S, LANES)), m_roped, m_first, inv_t)
    a_first = (lane % 32) < half
    use_row = (lane % GRID_W) < 32
    pos_rc = jnp.where(use_row[None, :], pos_row[:, None], pos_col[:, None])
    at = tables(pos_rc, np.ones(LANES, bool), a_first, inv_a)
    return mt, at


def _na_bias_table(rpb):
    W = GRID_W
    pad = W - NA_KW
    ext = jnp.concatenate([jnp.repeat(rpb[..., :1], pad, -1), rpb, jnp.repeat(rpb[..., -1:], pad, -1)], -1)
    t = jnp.stack([ext[..., W - 1 - c:2 * W - 1 - c] for c in range(W)], axis=2)
    c = np.arange(W)[:, None, None]
    kc = np.arange(W)[None, None, :]
    ws = np.clip(c - NA_KW // 2, 0, W - NA_KW)
    valid = (kc >= ws) & (kc < ws + NA_KW)
    out = []
    for d in range(NA_KH):
        td = jnp.transpose(t[:, NA_KH - 1 - d:2 * NA_KH - 1 - d], (0, 2, 1, 3))
        out.append(jnp.where(valid[None], td.astype(F32), MASK_VALUE))
    return jnp.stack(out).reshape(NA_KH, NA_HEADS * W, NA_KH * W)


def _prep_w_in(w):
    z = lambda n: jnp.zeros((w.shape[0], n), w.dtype)
    qc0 = 1184
    qslabs = []
    for j in range(4):
        qslabs += [w[:, qc0 + 64 * j:qc0 + 64 * (j + 1)], w[:, qc0 + 64 * (4 + j):qc0 + 64 * (5 + j)]]
    cols = [w[:, 0:768], w[:, 768:1024], w[:, 1024:1152], z(64), w[:, 1152:1184], z(32)] + qslabs + [
        w[:, 1696:1824], w[:, 1824:1952], w[:, 1952:5024]]
    return jnp.concatenate(cols, axis=1).astype(BF16)


def _prep_w_uq(w):
    z = jnp.zeros((w.shape[0], 32), w.dtype)
    hd = MLA_NOPE + MLA_ROPE
    return jnp.concatenate([a for h in range(MLA_HEADS) for a in (w[:, hd * h:hd * (h + 1)], z)],
                           axis=1).astype(BF16)


def _prep_w_ukv(w):
    z = jnp.zeros((w.shape[0], 64), w.dtype)
    hd = MLA_NOPE + MLA_V
    ks = [a for h in range(MLA_HEADS) for a in (w[:, hd * h:hd * h + MLA_NOPE], z)]
    vs = [w[:, hd * h + MLA_NOPE:hd * (h + 1)] for h in range(MLA_HEADS)]
    return jnp.concatenate(ks + vs, axis=1).astype(BF16)


def _prep_w_br_gqa(w):
    rows = []
    for j in range(4):
        rows += [w[64 * j:64 * (j + 1)], w[64 * (4 + j):64 * (5 + j)]]
    return jnp.concatenate(rows, axis=0).astype(BF16)


def kernel(x, c, ada_w, ada_b, norm_mix_g, norm_ffn_g, w_in, na_rpb, mla_q_norm_g, mla_kv_norm_g, mla_w_uq,
           mla_w_ukv, gqa_q_norm_g, gqa_k_norm_g, w_br_na, w_br_mla, w_br_gqa, w_out, ffn_w_gate_up,
           ffn_w_down, final_norm_g):
    B, S, D = x.shape
    L = ada_w.shape[0]
    mod = _ada_call(c, ada_w, ada_b).reshape(L, B, N_MOD, D)
    mtabs, atabs = _rope_tables(S)
    lane = np.arange(LANES)
    bd = jnp.asarray((lane[:, None] // 64 == lane[None, :] // 64) / 64.0, BF16)
    fg = final_norm_g.reshape(1, D)
    for l in range(L):
        gqk = jnp.concatenate([jnp.tile(gqa_q_norm_g[l], GQA_HEADS), jnp.tile(gqa_k_norm_g[l], GQA_KV_HEADS)])
        qna, kna, vna, qm, km, vm, qg, kg, vg, gates = _in_call(
            x, mod[l], norm_mix_g[l].reshape(1, D), _prep_w_in(w_in[l]),
            mla_q_norm_g[l].reshape(1, -1), mla_kv_norm_g[l].reshape(1, -1),
            _prep_w_uq(mla_w_uq[l]), _prep_w_ukv(mla_w_ukv[l]), gqk.reshape(1, -1), bd, mtabs, atabs)
        o_na = _na_call(qna, kna, vna, _na_bias_table(na_rpb[l]))
        o_mla = _attn_call(_mla_attn_kernel, qm, km, vm, MLA_HEADS * MLA_V, "mla_attn")
        o_gqa = _attn_call(_gqa_attn_kernel, qg, kg, vg, GQA_HEADS * GQA_HEAD_DIM, "gqa_attn")
        x = _merge_call(x, mod[l], o_na, o_mla, o_gqa, gates, w_br_na[l].astype(BF16),
                        w_br_mla[l].astype(BF16), _prep_w_br_gqa(w_br_gqa[l]), w_out[l].astype(BF16))
        x = _ffn_call(x, mod[l], norm_ffn_g[l].reshape(1, D), ffn_w_gate_up[l].astype(BF16),
                      ffn_w_down[l].astype(BF16), fg, final_norm=(l == L - 1))
    return x
```

```python
import functools

import numpy as np
import jax
import jax.numpy as jnp
from jax import lax
from jax.experimental import pallas as pl
from jax.experimental.pallas import tpu as pltpu

D_MODEL = 1024
GRID_W = 64
NA_HEADS = 4
NA_HEAD_DIM = 64
NA_KH = 8
NA_KW = 16
MLA_HEADS = 4
MLA_Q_LORA = 256
MLA_KV_LORA = 128
MLA_NOPE = 64
MLA_ROPE = 32
MLA_V = 64
GQA_HEADS = 8
GQA_KV_HEADS = 2
GQA_HEAD_DIM = 64
FFN_HIDDEN = 2816
ROPE_THETA = 10000.0
AXIAL_THETA = 10000.0
NORM_EPS = 1e-6
N_MOD = 6
MASK_VALUE = -1e30
LOG2E = 1.4426950408889634

LANES = 128
VMEM_LIMIT = 56 * 1024 * 1024
TOKEN_TILE = 512
Q_TILE = 256
FFN_CHUNK = 1408

C_NA = 0
C_MLA = 768
C_GQA = 1280
C_GATE = 2048
C_TOTAL = 5120

F32 = jnp.float32
BF16 = jnp.bfloat16


def _dot(a, b):
    return jnp.dot(a, b, preferred_element_type=F32)


def _dot_nt(a, b):
    return lax.dot_general(a, b, (((1,), (1,)), ((), ())), preferred_element_type=F32)


def _rms(x):
    return x * lax.rsqrt(jnp.mean(x * x, axis=-1, keepdims=True) + NORM_EPS)


def _rope_slab(x, c, s_up, s_dn):
    return x * c + pltpu.roll(x, LANES - 16, 1) * s_up + pltpu.roll(x, 16, 1) * s_dn


def _sigmoid(x):
    return 1.0 / (1.0 + jnp.exp(-x))


def _ada_kernel(c_ref, w_ref, b_ref, o_ref):
    c = c_ref[...]
    s = (c * _sigmoid(c)).astype(BF16)
    o_ref[...] = _dot(s, w_ref[...].astype(BF16)) + b_ref[...]


def _ada_call(c, ada_w, ada_b):
    L, D, N = ada_w.shape
    B = c.shape[0]
    tn = 1536
    return pl.pallas_call(
        _ada_kernel,
        grid=(L, N // tn),
        in_specs=[pl.BlockSpec((B, D), lambda l, j: (0, 0)),
                  pl.BlockSpec((None, D, tn), lambda l, j: (l, 0, j)),
                  pl.BlockSpec((None, 1, tn), lambda l, j: (l, 0, j))],
        out_specs=pl.BlockSpec((None, B, tn), lambda l, j: (l, 0, j)),
        out_shape=jax.ShapeDtypeStruct((L, B, N), F32),
        compiler_params=pltpu.CompilerParams(dimension_semantics=("parallel", "parallel"),
                                             vmem_limit_bytes=VMEM_LIMIT),
        name="ada_mod",
    )(c, ada_w, ada_b.reshape(L, 1, N))


def _in_kernel(x_ref, mod_ref, g_ref, w_ref, qng_ref, kvng_ref, wuq_ref, wukv_ref, gqk_ref, bd_ref,
               mc_ref, mu_ref, md_ref, ac_ref, au_ref, ad_ref,
               qna_ref, kna_ref, vna_ref, qm_ref, km_ref, vm_ref, qg_ref, kg_ref, vg_ref, gate_ref):
    x = x_ref[...]
    h = _rms(x) * g_ref[...] * (1.0 + mod_ref[1:2, :]) + mod_ref[0:1, :]
    hb = h.astype(BF16)

    def proj(a, b):
        return _dot(hb, w_ref[:, a:b])

    p = proj(C_NA, C_NA + 768)
    qna_ref[...] = (p[:, :256] * NA_HEAD_DIM ** -0.5).astype(BF16)
    kna_ref[...] = p[:, 256:512].astype(BF16)
    vna_ref[...] = p[:, 512:768].astype(BF16)

    p = proj(C_MLA, C_MLA + 512)
    mc, mu, md = mc_ref[...], mu_ref[...], md_ref[...]
    nq = (_rms(p[:, :256]) * qng_ref[...]).astype(BF16)
    qb = _dot(nq, wuq_ref[...])
    mla_scale = (MLA_NOPE + MLA_ROPE) ** -0.5 * LOG2E
    for s in range(MLA_HEADS):
        sl = slice(s * LANES, (s + 1) * LANES)
        qm_ref[:, sl] = (_rope_slab(qb[:, sl], mc, mu, md) * mla_scale).astype(BF16)
    nkv = (_rms(p[:, 256:384]) * kvng_ref[...]).astype(BF16)
    kvb = _dot(nkv, wukv_ref[...])
    kpe = _rope_slab(p[:, 384:512], mc, mu, md)
    for s in range(MLA_HEADS):
        sl = slice(s * LANES, (s + 1) * LANES)
        km_ref[:, sl] = (kvb[:, sl] + kpe).astype(BF16)
    vm_ref[...] = kvb[:, 512:768].astype(BF16)

    p = proj(C_GQA, C_GQA + 768)
    ac, au, ad = ac_ref[...], au_ref[...], ad_ref[...]
    bd = bd_ref[...]
    for s in range(5):
        sl = slice(s * LANES, (s + 1) * LANES)
        v = p[:, sl]
        sq = v * v
        hi = sq.astype(BF16)
        lo = (sq - hi.astype(F32)).astype(BF16)
        msq = _dot(hi, bd) + _dot(lo, bd)
        y = v * lax.rsqrt(msq + NORM_EPS) * gqk_ref[:, sl]
        r = _rope_slab(y, ac, au, ad)
        if s < 4:
            qg_ref[:, sl] = (r * (GQA_HEAD_DIM ** -0.5 * LOG2E)).astype(BF16)
        else:
            kg_ref[...] = r.astype(BF16)
    vg_ref[...] = p[:, 640:768].astype(BF16)

    for j in range(3):
        g = proj(C_GATE + j * D_MODEL, C_GATE + (j + 1) * D_MODEL)
        gate_ref[:, j * D_MODEL:(j + 1) * D_MODEL] = _sigmoid(g).astype(BF16)


def _in_call(x, mod, g, w, qng, kvng, wuq, wukv, gqk, bd, mtabs, atabs):
    B, S, D = x.shape
    tm = TOKEN_TILE
    tok = lambda w_: pl.BlockSpec((None, tm, w_), lambda b, i: (b, i, 0))
    full = lambda a: pl.BlockSpec(a.shape, lambda b, i: (0,) * a.ndim)
    tab = pl.BlockSpec((tm, LANES), lambda b, i: (i, 0))
    widths = (256, 256, 256, 512, 512, 256, 512, 128, 128, 3 * D)
    return pl.pallas_call(
        _in_kernel,
        grid=(B, S // tm),
        in_specs=[tok(D), pl.BlockSpec((None, N_MOD, D), lambda b, i: (b, 0, 0)), full(g), full(w),
                  full(qng), full(kvng), full(wuq), full(wukv), full(gqk), full(bd)] + [tab] * 6,
        out_specs=[tok(w_) for w_ in widths],
        out_shape=[jax.ShapeDtypeStruct((B, S, w_), BF16) for w_ in widths],
        compiler_params=pltpu.CompilerParams(dimension_semantics=("parallel", "parallel"),
                                             vmem_limit_bytes=VMEM_LIMIT),
        name="in_proj",
    )(x, mod, g, w, qng, kvng, wuq, wukv, gqk, bd, *mtabs, *atabs)


def _na_kernel(q_ref, k_ref, v_ref, bias_ref, o_ref, *, rows):
    kh = NA_KH
    head_of_lane = lax.broadcasted_iota(jnp.int32, (GRID_W, NA_HEADS * NA_HEAD_DIM), 1) // NA_HEAD_DIM

    def body(r, carry):
        rs = jnp.clip(r - kh // 2, 0, rows - kh)
        q = q_ref[pl.ds(pl.multiple_of(r * GRID_W, GRID_W), GRID_W), :]
        kstart = pl.multiple_of(rs * GRID_W, GRID_W)
        kb = k_ref[pl.ds(kstart, kh * GRID_W), :]
        vb = v_ref[pl.ds(kstart, kh * GRID_W), :]
        qs = jnp.concatenate([jnp.where(head_of_lane == h, q, jnp.zeros_like(q)) for h in range(NA_HEADS)],
                             axis=0)
        s = _dot_nt(qs, kb) + bias_ref[r - rs]
        m = jnp.max(s, axis=-1, keepdims=True)
        p = jnp.exp(s - m)
        l = jnp.sum(p, axis=-1, keepdims=True)
        o = _dot(p.astype(BF16), vb) / l
        out = jnp.zeros((GRID_W, NA_HEADS * NA_HEAD_DIM), F32)
        for h in range(NA_HEADS):
            out = jnp.where(head_of_lane == h, o[h * GRID_W:(h + 1) * GRID_W, :], out)
        o_ref[pl.ds(pl.multiple_of(r * GRID_W, GRID_W), GRID_W), :] = out.astype(BF16)
        return carry

    lax.fori_loop(0, rows, body, 0, unroll=8)


def _na_call(q, k, v, bias):
    B, S, W = q.shape
    rows = S // GRID_W
    blk = pl.BlockSpec((None, S, W), lambda b: (b, 0, 0))
    return pl.pallas_call(
        functools.partial(_na_kernel, rows=rows),
        grid=(B,),
        in_specs=[blk, blk, blk, pl.BlockSpec(bias.shape, lambda b: (0, 0, 0))],
        out_specs=blk,
        out_shape=jax.ShapeDtypeStruct((B, S, W), BF16),
        compiler_params=pltpu.CompilerParams(dimension_semantics=("parallel",),
                                             vmem_limit_bytes=VMEM_LIMIT),
        name="na_attn",
    )(q, k, v, bias)


L_MIN = 2.0 ** -60
MLA_BIAS_LANE = MLA_NOPE + MLA_ROPE


def _softmax_pv(q, k, v):
    s = _dot_nt(q, k)
    m = jnp.max(s, axis=-1, keepdims=True)
    p = jnp.exp2(s - m)
    l = jnp.sum(p, axis=-1, keepdims=True)
    return _dot(p.astype(BF16), v) / l


def _shifted_head(q_aug, k_aug, v_ext):
    p = jnp.exp2(_dot_nt(q_aug, k_aug)).astype(BF16)
    oe = _dot(p, v_ext)
    return oe[:, :LANES], oe[:, LANES:LANES + 1]


def _row_norm(x):
    xf = x.astype(F32)
    return jnp.sqrt(jnp.sum(xf * xf, axis=-1, keepdims=True))


def _max_row_norm(x, mask=None):
    xf = x.astype(F32)
    sq = xf * xf if mask is None else jnp.where(mask, xf * xf, 0.0)
    n = jnp.max(jnp.sum(sq, axis=-1, keepdims=True), axis=0, keepdims=True)
    return jnp.broadcast_to(jnp.sqrt(n), (1, LANES))


def _mla_exact(q_ref, k_ref, v_ref, o_ref):
    v = v_ref[...]
    head_of_lane = lax.broadcasted_iota(jnp.int32, o_ref.shape, 1) // MLA_V
    out = jnp.zeros(o_ref.shape, F32)
    for h in range(MLA_HEADS):
        sl = slice(h * LANES, (h + 1) * LANES)
        out = jnp.where(head_of_lane == h, _softmax_pv(q_ref[:, sl], k_ref[:, sl], v), out)
    o_ref[...] = out.astype(BF16)


def _mla_attn_kernel(q_ref, k_ref, v_ref, o_ref, kaug_ref, vext_ref, kmax_ref):
    S, tq = k_ref.shape[0], q_ref.shape[0]

    @pl.when(pl.program_id(1) == 0)
    def _():
        lane_s = lax.broadcasted_iota(jnp.int32, (S, LANES), 1)
        one0 = jnp.where(lane_s == 0, 1.0, 0.0).astype(BF16)
        for h in range(MLA_HEADS):
            sl = slice(h * LANES, (h + 1) * LANES)
            k = k_ref[:, sl]
            kaug_ref[:, sl] = jnp.where(lane_s == MLA_BIAS_LANE, jnp.ones_like(k), k)
            kmax_ref[h:h + 1, :] = _max_row_norm(k)
        for p in range(MLA_HEADS // 2):
            vext_ref[p, :, :LANES] = v_ref[:, p * LANES:(p + 1) * LANES]
            vext_ref[p, :, LANES:] = one0

    lane = lax.broadcasted_iota(jnp.int32, (tq, LANES), 1)
    low = lane < MLA_V
    lmin = None
    for p in range(MLA_HEADS // 2):
        outs = []
        for h in (2 * p, 2 * p + 1):
            sl = slice(h * LANES, (h + 1) * LANES)
            q = q_ref[:, sl]
            b = _row_norm(q) * kmax_ref[h:h + 1, 0:1]
            q_aug = jnp.where(lane == MLA_BIAS_LANE, (-b).astype(BF16), q)
            o, l = _shifted_head(q_aug, kaug_ref[:, sl], vext_ref[p])
            outs.append(o / l)
            lmin = l if lmin is None else jnp.minimum(lmin, l)
        o_ref[:, p * LANES:(p + 1) * LANES] = jnp.where(low, outs[0], outs[1]).astype(BF16)

    @pl.when(jnp.min(lmin) < L_MIN)
    def _():
        _mla_exact(q_ref, k_ref, v_ref, o_ref)


def _gqa_exact(q_ref, k_ref, v_ref, o_ref):
    k = k_ref[...]
    v = v_ref[...]
    low = lax.broadcasted_iota(jnp.int32, (q_ref.shape[0], LANES), 1) < GQA_HEAD_DIM
    for j in range(GQA_HEADS // GQA_KV_HEADS):
        sl = slice(j * LANES, (j + 1) * LANES)
        q = q_ref[:, sl]
        zero = jnp.zeros_like(q)
        o_lo = _softmax_pv(jnp.where(low, q, zero), k, v)
        o_hi = _softmax_pv(jnp.where(low, zero, q), k, v)
        o_ref[:, sl] = jnp.where(low, o_lo, o_hi).astype(BF16)


def _gqa_attn_kernel(q_ref, k_ref, v_ref, o_ref, kaug_ref, vext_ref, kmax_ref):
    S, tq = k_ref.shape[0], q_ref.shape[0]

    @pl.when(pl.program_id(1) == 0)
    def _():
        lane_s = lax.broadcasted_iota(jnp.int32, (S, LANES), 1)
        one0 = jnp.where(lane_s == 0, 1.0, 0.0).astype(BF16)
        k = k_ref[...]
        kaug_ref[:, :LANES] = k
        kaug_ref[:, LANES:] = one0
        vext_ref[:, :LANES] = v_ref[...]
        vext_ref[:, LANES:] = one0
        kmax_ref[0:1, :] = _max_row_norm(k, lane_s < GQA_HEAD_DIM)
        kmax_ref[1:2, :] = _max_row_norm(k, lane_s >= GQA_HEAD_DIM)

    lane = lax.broadcasted_iota(jnp.int32, (tq, LANES), 1)
    low = lane < GQA_HEAD_DIM
    k_aug = kaug_ref[...]
    v_ext = vext_ref[...]
    lmin = None
    for j in range(GQA_HEADS // GQA_KV_HEADS):
        sl = slice(j * LANES, (j + 1) * LANES)
        q = q_ref[:, sl]
        zero = jnp.zeros_like(q)
        outs = []
        for half in range(GQA_KV_HEADS):
            qm = jnp.where(low, q, zero) if half == 0 else jnp.where(low, zero, q)
            b = _row_norm(qm) * kmax_ref[half:half + 1, 0:1]
            bias = jnp.where(lane == 0, -b, 0.0).astype(BF16)
            o, l = _shifted_head(jnp.concatenate([qm, bias], axis=1), k_aug, v_ext)
            outs.append(o / l)
            lmin = l if lmin is None else jnp.minimum(lmin, l)
        o_ref[:, sl] = jnp.where(low, outs[0], outs[1]).astype(BF16)

    @pl.when(jnp.min(lmin) < L_MIN)
    def _():
        _gqa_exact(q_ref, k_ref, v_ref, o_ref)


def _attn_call(kernel, q, k, v, out_width, scratch, name):
    B, S, _ = q.shape
    tq = Q_TILE
    return pl.pallas_call(
        kernel,
        grid=(B, S // tq),
        in_specs=[pl.BlockSpec((None, tq, q.shape[2]), lambda b, i: (b, i, 0)),
                  pl.BlockSpec((None, S, k.shape[2]), lambda b, i: (b, 0, 0)),
                  pl.BlockSpec((None, S, v.shape[2]), lambda b, i: (b, 0, 0))],
        out_specs=pl.BlockSpec((None, tq, out_width), lambda b, i: (b, i, 0)),
        out_shape=jax.ShapeDtypeStruct((B, S, out_width), BF16),
        scratch_shapes=scratch + [pltpu.VMEM((8, LANES), F32)],
        compiler_params=pltpu.CompilerParams(dimension_semantics=("parallel", "arbitrary"),
                                             vmem_limit_bytes=VMEM_LIMIT),
        name=name,
    )(q, k, v)


def _merge_kernel(x_ref, mod_ref, ona_ref, omla_ref, ogqa_ref, gate_ref, wa_ref, wb_ref, wc_ref, wo_ref,
                  o_ref):
    D = D_MODEL
    m = (gate_ref[:, 0:D].astype(F32) * _dot(ona_ref[...], wa_ref[...])
         + gate_ref[:, D:2 * D].astype(F32) * _dot(omla_ref[...], wb_ref[...])
         + gate_ref[:, 2 * D:3 * D].astype(F32) * _dot(ogqa_ref[...], wc_ref[...]))
    y = _dot(m.astype(BF16), wo_ref[...])
    o_ref[...] = x_ref[...] + mod_ref[2:3, :] * y


def _merge_call(x, mod, ona, omla, ogqa, gates, wa, wb, wc, wo):
    B, S, D = x.shape
    tm = TOKEN_TILE
    tok = lambda a: pl.BlockSpec((None, tm, a.shape[2]), lambda b, i: (b, i, 0))
    full = lambda a: pl.BlockSpec(a.shape, lambda b, i: (0,) * a.ndim)
    return pl.pallas_call(
        _merge_kernel,
        grid=(B, S // tm),
        in_specs=[tok(x), pl.BlockSpec((None, N_MOD, D), lambda b, i: (b, 0, 0)), tok(ona), tok(omla),
                  tok(ogqa), tok(gates), full(wa), full(wb), full(wc), full(wo)],
        out_specs=tok(x),
        out_shape=jax.ShapeDtypeStruct(x.shape, F32),
        compiler_params=pltpu.CompilerParams(dimension_semantics=("parallel", "parallel"),
                                             vmem_limit_bytes=VMEM_LIMIT),
        name="merge_out",
    )(x, mod, ona, omla, ogqa, gates, wa, wb, wc, wo)


def _ffn_kernel(x_ref, mod_ref, g_ref, wgu_ref, wd_ref, fg_ref, o_ref, *, final_norm):
    x = x_ref[...]
    h = _rms(x) * g_ref[...] * (1.0 + mod_ref[4:5, :]) + mod_ref[3:4, :]
    hb = h.astype(BF16)
    acc = jnp.zeros(x.shape, F32)
    for c0 in range(0, FFN_HIDDEN, FFN_CHUNK):
        gate = _dot(hb, wgu_ref[:, c0:c0 + FFN_CHUNK])
        up = _dot(hb, wgu_ref[:, FFN_HIDDEN + c0:FFN_HIDDEN + c0 + FFN_CHUNK])
        act = (gate * _sigmoid(gate) * up).astype(BF16)
        acc = acc + _dot(act, wd_ref[c0:c0 + FFN_CHUNK, :])
    y = x + mod_ref[5:6, :] * acc
    if final_norm:
        y = _rms(y) * fg_ref[...]
    o_ref[...] = y


def _ffn_call(x, mod, g, wgu, wd, fg, final_norm):
    B, S, D = x.shape
    tm = TOKEN_TILE
    tok = pl.BlockSpec((None, tm, D), lambda b, i: (b, i, 0))
    full = lambda a: pl.BlockSpec(a.shape, lambda b, i: (0,) * a.ndim)
    return pl.pallas_call(
        functools.partial(_ffn_kernel, final_norm=final_norm),
        grid=(B, S // tm),
        in_specs=[tok, pl.BlockSpec((None, N_MOD, D), lambda b, i: (b, 0, 0)), full(g), full(wgu), full(wd),
                  full(fg)],
        out_specs=tok,
        out_shape=jax.ShapeDtypeStruct(x.shape, F32),
        compiler_params=pltpu.CompilerParams(dimension_semantics=("parallel", "parallel"),
                                             vmem_limit_bytes=VMEM_LIMIT),
        name="swiglu",
    )(x, mod, g, wgu, wd, fg)


def _rope_tables(S):
    t = jnp.arange(S, dtype=jnp.int32)
    pos_t = t.astype(F32)
    pos_row = (t // GRID_W).astype(F32)
    pos_col = (t % GRID_W).astype(F32)
    half = 16
    lane = np.arange(LANES)

    def tables(pos_of_lane, roped, first_half, inv_freq):
        ang = pos_of_lane * inv_freq[lane % half][None, :]
        cos, sin = jnp.cos(ang), jnp.sin(ang)
        c = jnp.where(roped[None, :], cos, 1.0)
        up = jnp.where((roped & first_half)[None, :], -sin, 0.0)
        dn = jnp.where((roped & ~first_half)[None, :], sin, 0.0)
        return c.astype(F32), up.astype(F32), dn.astype(F32)

    inv_t = 1.0 / (ROPE_THETA ** (jnp.arange(half, dtype=F32) / half))
    inv_a = 1.0 / (AXIAL_THETA ** (jnp.arange(half, dtype=F32) / half))
    m_roped = (lane >= MLA_NOPE) & (lane < MLA_NOPE + MLA_ROPE)
    m_first = ((lane - MLA_NOPE) % 32) < half
    mt = tables(jnp.broadcast_to(pos_t[:, None], (S, LANES)), m_roped, m_first, inv_t)
    a_first = (lane % 32) < half
    use_row = (lane % GRID_W) < 32
    pos_rc = jnp.where(use_row[None, :], pos_row[:, None], pos_col[:, None])
    at = tables(pos_rc, np.ones(LANES, bool), a_first, inv_a)
    return mt, at


def _na_bias_table(rpb):
    W = GRID_W
    pad = W - NA_KW
    ext = jnp.concatenate([jnp.repeat(rpb[..., :1], pad, -1), rpb, jnp.repeat(rpb[..., -1:], pad, -1)], -1)
    t = jnp.stack([ext[..., W - 1 - c:2 * W - 1 - c] for c in range(W)], axis=2)
    c = np.arange(W)[:, None, None]
    kc = np.arange(W)[None, None, :]
    ws = np.clip(c - NA_KW // 2, 0, W - NA_KW)
    valid = (kc >= ws) & (kc < ws + NA_KW)
    out = []
    for d in range(NA_KH):
        td = jnp.transpose(t[:, NA_KH - 1 - d:2 * NA_KH - 1 - d], (0, 2, 1, 3))
        out.append(jnp.where(valid[None], td.astype(F32), MASK_VALUE))
    return jnp.stack(out).reshape(NA_KH, NA_HEADS * W, NA_KH * W)


def _prep_w_in(w):
    z = lambda n: jnp.zeros((w.shape[0], n), w.dtype)
    qc0 = 1184
    qslabs = []
    for j in range(4):
        qslabs += [w[:, qc0 + 64 * j:qc0 + 64 * (j + 1)], w[:, qc0 + 64 * (4 + j):qc0 + 64 * (5 + j)]]
    cols = [w[:, 0:768], w[:, 768:1024], w[:, 1024:1152], z(64), w[:, 1152:1184], z(32)] + qslabs + [
        w[:, 1696:1824], w[:, 1824:1952], w[:, 1952:5024]]
    return jnp.concatenate(cols, axis=1).astype(BF16)


def _prep_w_uq(w):
    z = jnp.zeros((w.shape[0], 32), w.dtype)
    hd = MLA_NOPE + MLA_ROPE
    return jnp.concatenate([a for h in range(MLA_HEADS) for a in (w[:, hd * h:hd * (h + 1)], z)],
                           axis=1).astype(BF16)


def _prep_w_ukv(w):
    z = jnp.zeros((w.shape[0], 64), w.dtype)
    hd = MLA_NOPE + MLA_V
    ks = [a for h in range(MLA_HEADS) for a in (w[:, hd * h:hd * h + MLA_NOPE], z)]
    vs = [w[:, hd * h + MLA_NOPE:hd * (h + 1)] for h in range(MLA_HEADS)]
    return jnp.concatenate(ks + vs, axis=1).astype(BF16)


def _prep_w_br_gqa(w):
    rows = []
    for j in range(4):
        rows += [w[64 * j:64 * (j + 1)], w[64 * (4 + j):64 * (5 + j)]]
    return jnp.concatenate(rows, axis=0).astype(BF16)


def kernel(x, c, ada_w, ada_b, norm_mix_g, norm_ffn_g, w_in, na_rpb, mla_q_norm_g, mla_kv_norm_g, mla_w_uq,
           mla_w_ukv, gqa_q_norm_g, gqa_k_norm_g, w_br_na, w_br_mla, w_br_gqa, w_out, ffn_w_gate_up,
           ffn_w_down, final_norm_g):
    B, S, D = x.shape
    L = ada_w.shape[0]
    mod = _ada_call(c, ada_w, ada_b).reshape(L, B, N_MOD, D)
    mtabs, atabs = _rope_tables(S)
    lane = np.arange(LANES)
    bd = jnp.asarray((lane[:, None] // 64 == lane[None, :] // 64) / 64.0, BF16)
    fg = final_norm_g.reshape(1, D)
    for l in range(L):
        gqk = jnp.concatenate([jnp.tile(gqa_q_norm_g[l], GQA_HEADS), jnp.tile(gqa_k_norm_g[l], GQA_KV_HEADS)])
        qna, kna, vna, qm, km, vm, qg, kg, vg, gates = _in_call(
            x, mod[l], norm_mix_g[l].reshape(1, D), _prep_w_in(w_in[l]),
            mla_q_norm_g[l].reshape(1, -1), mla_kv_norm_g[l].reshape(1, -1),
            _prep_w_uq(mla_w_uq[l]), _prep_w_ukv(mla_w_ukv[l]), gqk.reshape(1, -1), bd, mtabs, atabs)
        o_na = _na_call(qna, kna, vna, _na_bias_table(na_rpb[l]))
        o_mla = _attn_call(_mla_attn_kernel, qm, km, vm, MLA_HEADS * MLA_V,
                           [pltpu.VMEM((S, MLA_HEADS * LANES), BF16),
                            pltpu.VMEM((MLA_HEADS // 2, S, 2 * LANES), BF16)], "mla_attn")
        o_gqa = _attn_call(_gqa_attn_kernel, qg, kg, vg, GQA_HEADS * GQA_HEAD_DIM,
                           [pltpu.VMEM((S, 2 * LANES), BF16), pltpu.VMEM((S, 2 * LANES), BF16)], "gqa_attn")
        x = _merge_call(x, mod[l], o_na, o_mla, o_gqa, gates, w_br_na[l].astype(BF16),
                        w_br_mla[l].astype(BF16), _prep_w_br_gqa(w_br_gqa[l]), w_out[l].astype(BF16))
        x = _ffn_call(x, mod[l], norm_ffn_g[l].reshape(1, D), ffn_w_gate_up[l].astype(BF16),
                      ffn_w_down[l].astype(BF16), fg, final_norm=(l == L - 1))
    return x
```

```python
import functools

import numpy as np
import jax
import jax.numpy as jnp
from jax import lax
from jax.experimental import pallas as pl
from jax.experimental.pallas import tpu as pltpu

D_MODEL = 1024
GRID_W = 64
NA_HEADS = 4
NA_HEAD_DIM = 64
NA_KH = 8
NA_KW = 16
MLA_HEADS = 4
MLA_Q_LORA = 256
MLA_KV_LORA = 128
MLA_NOPE = 64
MLA_ROPE = 32
MLA_V = 64
GQA_HEADS = 8
GQA_KV_HEADS = 2
GQA_HEAD_DIM = 64
FFN_HIDDEN = 2816
ROPE_THETA = 10000.0
AXIAL_THETA = 10000.0
NORM_EPS = 1e-6
N_MOD = 6
MASK_VALUE = -1e30
LOG2E = 1.4426950408889634

LANES = 128
VMEM_LIMIT = 56 * 1024 * 1024
TOKEN_TILE = 512
Q_TILE = 256
FFN_CHUNK = 1408

C_NA = 0
C_MLA = 768
C_GQA = 1280
C_GATE = 2048
C_TOTAL = 5120

F32 = jnp.float32
BF16 = jnp.bfloat16


def _dot(a, b):
    return jnp.dot(a, b, preferred_element_type=F32)


def _dot_nt(a, b):
    return lax.dot_general(a, b, (((1,), (1,)), ((), ())), preferred_element_type=F32)


def _rms(x):
    return x * lax.rsqrt(jnp.mean(x * x, axis=-1, keepdims=True) + NORM_EPS)


def _rope_slab(x, c, s_up, s_dn):
    return x * c + pltpu.roll(x, LANES - 16, 1) * s_up + pltpu.roll(x, 16, 1) * s_dn


def _sigmoid(x):
    return 1.0 / (1.0 + jnp.exp(-x))


def _layer_spec(a, l):
    return pl.BlockSpec((None,) + a.shape[1:], lambda *_: (l,) + (0,) * (a.ndim - 1))


def _mod_spec(mod, l):
    return pl.BlockSpec((None, None) + mod.shape[2:], lambda b, *_: (l, b, 0, 0))


def _ada_kernel(c_ref, w_ref, b_ref, o_ref):
    c = c_ref[...]
    s = (c * _sigmoid(c)).astype(BF16)
    o_ref[...] = _dot(s, w_ref[...].astype(BF16)) + b_ref[...]


def _ada_call(c, ada_w, ada_b):
    L, D, N = ada_w.shape
    B = c.shape[0]
    tn = 1536
    return pl.pallas_call(
        _ada_kernel,
        grid=(L, N // tn),
        in_specs=[pl.BlockSpec((B, D), lambda l, j: (0, 0)),
                  pl.BlockSpec((None, D, tn), lambda l, j: (l, 0, j)),
                  pl.BlockSpec((None, 1, tn), lambda l, j: (l, 0, j))],
        out_specs=pl.BlockSpec((None, B, tn), lambda l, j: (l, 0, j)),
        out_shape=jax.ShapeDtypeStruct((L, B, N), F32),
        compiler_params=pltpu.CompilerParams(dimension_semantics=("parallel", "parallel"),
                                             vmem_limit_bytes=VMEM_LIMIT),
        name="ada_mod",
    )(c, ada_w, ada_b.reshape(L, 1, N))


def _in_kernel(x_ref, mod_ref, g_ref, w_ref, qng_ref, kvng_ref, wuq_ref, wukv_ref, gqk_ref, bd_ref,
               mc_ref, mu_ref, md_ref, ac_ref, au_ref, ad_ref,
               qna_ref, kna_ref, vna_ref, qm_ref, km_ref, vm_ref, qg_ref, kg_ref, vg_ref, gate_ref):
    x = x_ref[...]
    h = _rms(x) * g_ref[...] * (1.0 + mod_ref[1:2, :]) + mod_ref[0:1, :]
    hb = h.astype(BF16)

    def proj(a, b):
        return _dot(hb, w_ref[:, a:b])

    p = proj(C_NA, C_NA + 768)
    qna_ref[...] = (p[:, :256] * NA_HEAD_DIM ** -0.5).astype(BF16)
    kna_ref[...] = p[:, 256:512].astype(BF16)
    vna_ref[...] = p[:, 512:768].astype(BF16)

    p = proj(C_MLA, C_MLA + 512)
    mc, mu, md = mc_ref[...], mu_ref[...], md_ref[...]
    nq = (_rms(p[:, :256]) * qng_ref[...]).astype(BF16)
    qb = _dot(nq, wuq_ref[...])
    mla_scale = (MLA_NOPE + MLA_ROPE) ** -0.5 * LOG2E
    for s in range(MLA_HEADS):
        sl = slice(s * LANES, (s + 1) * LANES)
        qm_ref[:, sl] = (_rope_slab(qb[:, sl], mc, mu, md) * mla_scale).astype(BF16)
    nkv = (_rms(p[:, 256:384]) * kvng_ref[...]).astype(BF16)
    kvb = _dot(nkv, wukv_ref[...])
    kpe = _rope_slab(p[:, 384:512], mc, mu, md)
    for s in range(MLA_HEADS):
        sl = slice(s * LANES, (s + 1) * LANES)
        km_ref[:, sl] = (kvb[:, sl] + kpe).astype(BF16)
    vm_ref[...] = kvb[:, 512:768].astype(BF16)

    p = proj(C_GQA, C_GQA + 768)
    ac, au, ad = ac_ref[...], au_ref[...], ad_ref[...]
    bd = bd_ref[...]
    for s in range(5):
        sl = slice(s * LANES, (s + 1) * LANES)
        v = p[:, sl]
        sq = v * v
        hi = sq.astype(BF16)
        lo = (sq - hi.astype(F32)).astype(BF16)
        msq = _dot(jnp.concatenate([hi, lo], axis=1), bd)
        y = v * lax.rsqrt(msq + NORM_EPS) * gqk_ref[:, sl]
        r = _rope_slab(y, ac, au, ad)
        if s < 4:
            qg_ref[:, sl] = (r * (GQA_HEAD_DIM ** -0.5 * LOG2E)).astype(BF16)
        else:
            kg_ref[...] = r.astype(BF16)
    vg_ref[...] = p[:, 640:768].astype(BF16)

    for j in range(3):
        g = proj(C_GATE + j * D_MODEL, C_GATE + (j + 1) * D_MODEL)
        gate_ref[:, j * D_MODEL:(j + 1) * D_MODEL] = _sigmoid(g).astype(BF16)


def _in_call(x, mod, l, g, w, qng, kvng, wuq, wukv, gqk, bd, mtabs, atabs):
    B, S, D = x.shape
    tm = TOKEN_TILE
    tok = lambda w_: pl.BlockSpec((None, tm, w_), lambda b, i: (b, i, 0))
    full = lambda a: pl.BlockSpec(a.shape, lambda b, i: (0,) * a.ndim)
    tab = pl.BlockSpec((tm, LANES), lambda b, i: (i, 0))
    widths = (256, 256, 256, 512, 512, 256, 512, 128, 128, 3 * D)
    return pl.pallas_call(
        _in_kernel,
        grid=(B, S // tm),
        in_specs=[tok(D), _mod_spec(mod, l)] + [_layer_spec(a, l) for a in (g, w, qng, kvng, wuq, wukv, gqk)]
        + [full(bd)] + [tab] * 6,
        out_specs=[tok(w_) for w_ in widths],
        out_shape=[jax.ShapeDtypeStruct((B, S, w_), BF16) for w_ in widths],
        compiler_params=pltpu.CompilerParams(dimension_semantics=("parallel", "parallel"),
                                             vmem_limit_bytes=VMEM_LIMIT),
        name="in_proj",
    )(x, mod, g, w, qng, kvng, wuq, wukv, gqk, bd, *mtabs, *atabs)


def _na_kernel(q_ref, k_ref, v_ref, bias_ref, o_ref, *, rows):
    kh = NA_KH
    head_of_lane = lax.broadcasted_iota(jnp.int32, (GRID_W, NA_HEADS * NA_HEAD_DIM), 1) // NA_HEAD_DIM

    def body(r, carry):
        rs = jnp.clip(r - kh // 2, 0, rows - kh)
        q = q_ref[pl.ds(pl.multiple_of(r * GRID_W, GRID_W), GRID_W), :]
        kstart = pl.multiple_of(rs * GRID_W, GRID_W)
        kb = k_ref[pl.ds(kstart, kh * GRID_W), :]
        vb = v_ref[pl.ds(kstart, kh * GRID_W), :]
        qs = jnp.concatenate([jnp.where(head_of_lane == h, q, jnp.zeros_like(q)) for h in range(NA_HEADS)],
                             axis=0)
        s = _dot_nt(qs, kb) + bias_ref[r - rs]
        m = jnp.max(s, axis=-1, keepdims=True)
        p = jnp.exp(s - m)
        l = jnp.sum(p, axis=-1, keepdims=True)
        o = _dot(p.astype(BF16), vb) / l
        out = jnp.zeros((GRID_W, NA_HEADS * NA_HEAD_DIM), F32)
        for h in range(NA_HEADS):
            out = jnp.where(head_of_lane == h, o[h * GRID_W:(h + 1) * GRID_W, :], out)
        o_ref[pl.ds(pl.multiple_of(r * GRID_W, GRID_W), GRID_W), :] = out.astype(BF16)
        return carry

    lax.fori_loop(0, rows, body, 0, unroll=8)


def _na_call(q, k, v, bias, l):
    B, S, W = q.shape
    rows = S // GRID_W
    blk = pl.BlockSpec((None, S, W), lambda b: (b, 0, 0))
    return pl.pallas_call(
        functools.partial(_na_kernel, rows=rows),
        grid=(B,),
        in_specs=[blk, blk, blk, _layer_spec(bias, l)],
        out_specs=blk,
        out_shape=jax.ShapeDtypeStruct((B, S, W), BF16),
        compiler_params=pltpu.CompilerParams(dimension_semantics=("parallel",),
                                             vmem_limit_bytes=VMEM_LIMIT),
        name="na_attn",
    )(q, k, v, bias)


L_MIN = 2.0 ** -60
MLA_BIAS_LANE = MLA_NOPE + MLA_ROPE


def _softmax_pv(q, k, v):
    s = _dot_nt(q, k)
    m = jnp.max(s, axis=-1, keepdims=True)
    p = jnp.exp2(s - m)
    l = jnp.sum(p, axis=-1, keepdims=True)
    return _dot(p.astype(BF16), v) / l


def _shifted_head(q_aug, k_aug, v_ext):
    p = jnp.exp2(_dot_nt(q_aug, k_aug)).astype(BF16)
    oe = _dot(p, v_ext)
    return oe[:, :LANES], oe[:, LANES:LANES + 1]


def _row_norm(x):
    xf = x.astype(F32)
    return jnp.sqrt(jnp.sum(xf * xf, axis=-1, keepdims=True))


def _max_row_norm(x, mask=None):
    xf = x.astype(F32)
    sq = xf * xf if mask is None else jnp.where(mask, xf * xf, 0.0)
    n = jnp.max(jnp.sum(sq, axis=-1, keepdims=True), axis=0, keepdims=True)
    return jnp.broadcast_to(jnp.sqrt(n), (1, LANES))


def _mla_exact(q_ref, k_ref, v_ref, o_ref):
    v = v_ref[...]
    head_of_lane = lax.broadcasted_iota(jnp.int32, o_ref.shape, 1) // MLA_V
    out = jnp.zeros(o_ref.shape, F32)
    for h in range(MLA_HEADS):
        sl = slice(h * LANES, (h + 1) * LANES)
        out = jnp.where(head_of_lane == h, _softmax_pv(q_ref[:, sl], k_ref[:, sl], v), out)
    o_ref[...] = out.astype(BF16)


def _mla_attn_kernel(q_ref, k_ref, v_ref, o_ref, kaug_ref, vext_ref, kmax_ref):
    S, tq = k_ref.shape[0], q_ref.shape[0]

    @pl.when(pl.program_id(1) == 0)
    def _():
        lane_s = lax.broadcasted_iota(jnp.int32, (S, LANES), 1)
        one0 = jnp.where(lane_s == 0, 1.0, 0.0).astype(BF16)
        for h in range(MLA_HEADS):
            sl = slice(h * LANES, (h + 1) * LANES)
            k = k_ref[:, sl]
            kaug_ref[:, sl] = jnp.where(lane_s == MLA_BIAS_LANE, jnp.ones_like(k), k)
            kmax_ref[h:h + 1, :] = _max_row_norm(k)
        for p in range(MLA_HEADS // 2):
            vext_ref[p, :, :LANES] = v_ref[:, p * LANES:(p + 1) * LANES]
            vext_ref[p, :, LANES:] = one0

    lane = lax.broadcasted_iota(jnp.int32, (tq, LANES), 1)
    low = lane < MLA_V
    lmin = None
    for p in range(MLA_HEADS // 2):
        outs = []
        for h in (2 * p, 2 * p + 1):
            sl = slice(h * LANES, (h + 1) * LANES)
            q = q_ref[:, sl]
            b = _row_norm(q) * kmax_ref[h:h + 1, 0:1]
            q_aug = jnp.where(lane == MLA_BIAS_LANE, (-b).astype(BF16), q)
            o, l = _shifted_head(q_aug, kaug_ref[:, sl], vext_ref[p])
            outs.append(o / l)
            lmin = l if lmin is None else jnp.minimum(lmin, l)
        o_ref[:, p * LANES:(p + 1) * LANES] = jnp.where(low, outs[0], outs[1]).astype(BF16)

    @pl.when(jnp.min(lmin) < L_MIN)
    def _():
        _mla_exact(q_ref, k_ref, v_ref, o_ref)


def _gqa_exact(q_ref, k_ref, v_ref, o_ref):
    k = k_ref[...]
    v = v_ref[...]
    low = lax.broadcasted_iota(jnp.int32, (q_ref.shape[0], LANES), 1) < GQA_HEAD_DIM
    for j in range(GQA_HEADS // GQA_KV_HEADS):
        sl = slice(j * LANES, (j + 1) * LANES)
        q = q_ref[:, sl]
        zero = jnp.zeros_like(q)
        o_lo = _softmax_pv(jnp.where(low, q, zero), k, v)
        o_hi = _softmax_pv(jnp.where(low, zero, q), k, v)
        o_ref[:, sl] = jnp.where(low, o_lo, o_hi).astype(BF16)


def _gqa_attn_kernel(q_ref, k_ref, v_ref, o_ref, kaug_ref, vext_ref, kmax_ref):
    S, tq = k_ref.shape[0], q_ref.shape[0]

    @pl.when(pl.program_id(1) == 0)
    def _():
        lane_s = lax.broadcasted_iota(jnp.int32, (S, LANES), 1)
        one0 = jnp.where(lane_s == 0, 1.0, 0.0).astype(BF16)
        k = k_ref[...]
        kaug_ref[:, :LANES] = k
        kaug_ref[:, LANES:] = one0
        vext_ref[:, :LANES] = v_ref[...]
        vext_ref[:, LANES:] = one0
        kmax_ref[0:1, :] = _max_row_norm(k, lane_s < GQA_HEAD_DIM)
        kmax_ref[1:2, :] = _max_row_norm(k, lane_s >= GQA_HEAD_DIM)

    lane = lax.broadcasted_iota(jnp.int32, (tq, LANES), 1)
    low = lane < GQA_HEAD_DIM
    k_aug = kaug_ref[...]
    v_ext = vext_ref[...]
    lmin = None
    for j in range(GQA_HEADS // GQA_KV_HEADS):
        sl = slice(j * LANES, (j + 1) * LANES)
        q = q_ref[:, sl]
        zero = jnp.zeros_like(q)
        outs = []
        for half in range(GQA_KV_HEADS):
            qm = jnp.where(low, q, zero) if half == 0 else jnp.where(low, zero, q)
            b = _row_norm(qm) * kmax_ref[half:half + 1, 0:1]
            bias = jnp.where(lane == 0, -b, 0.0).astype(BF16)
            o, l = _shifted_head(jnp.concatenate([qm, bias], axis=1), k_aug, v_ext)
            outs.append(o / l)
            lmin = l if lmin is None else jnp.minimum(lmin, l)
        o_ref[:, sl] = jnp.where(low, outs[0], outs[1]).astype(BF16)

    @pl.when(jnp.min(lmin) < L_MIN)
    def _():
        _gqa_exact(q_ref, k_ref, v_ref, o_ref)


def _attn_call(kernel, q, k, v, out_width, scratch, name):
    B, S, _ = q.shape
    tq = Q_TILE
    return pl.pallas_call(
        kernel,
        grid=(B, S // tq),
        in_specs=[pl.BlockSpec((None, tq, q.shape[2]), lambda b, i: (b, i, 0)),
                  pl.BlockSpec((None, S, k.shape[2]), lambda b, i: (b, 0, 0)),
                  pl.BlockSpec((None, S, v.shape[2]), lambda b, i: (b, 0, 0))],
        out_specs=pl.BlockSpec((None, tq, out_width), lambda b, i: (b, i, 0)),
        out_shape=jax.ShapeDtypeStruct((B, S, out_width), BF16),
        scratch_shapes=scratch + [pltpu.VMEM((8, LANES), F32)],
        compiler_params=pltpu.CompilerParams(dimension_semantics=("parallel", "arbitrary"),
                                             vmem_limit_bytes=VMEM_LIMIT),
        name=name,
    )(q, k, v)


def _merge_kernel(x_ref, mod_ref, ona_ref, omla_ref, ogqa_ref, gate_ref, wa_ref, wb_ref, wc_ref, wo_ref,
                  o_ref):
    D = D_MODEL
    m = (gate_ref[:, 0:D].astype(F32) * _dot(ona_ref[...], wa_ref[...])
         + gate_ref[:, D:2 * D].astype(F32) * _dot(omla_ref[...], wb_ref[...])
         + gate_ref[:, 2 * D:3 * D].astype(F32) * _dot(ogqa_ref[...], wc_ref[...]))
    y = _dot(m.astype(BF16), wo_ref[...])
    o_ref[...] = x_ref[...] + mod_ref[2:3, :] * y


def _merge_call(x, mod, l, ona, omla, ogqa, gates, wa, wb, wc, wo):
    B, S, D = x.shape
    tm = TOKEN_TILE
    tok = lambda a: pl.BlockSpec((None, tm, a.shape[2]), lambda b, i: (b, i, 0))
    full = lambda a: pl.BlockSpec(a.shape, lambda b, i: (0,) * a.ndim)
    return pl.pallas_call(
        _merge_kernel,
        grid=(B, S // tm),
        in_specs=[tok(x), _mod_spec(mod, l), tok(ona), tok(omla), tok(ogqa), tok(gates)]
        + [_layer_spec(a, l) for a in (wa, wb, wc, wo)],
        out_specs=tok(x),
        out_shape=jax.ShapeDtypeStruct(x.shape, F32),
        compiler_params=pltpu.CompilerParams(dimension_semantics=("parallel", "parallel"),
                                             vmem_limit_bytes=VMEM_LIMIT),
        name="merge_out",
    )(x, mod, ona, omla, ogqa, gates, wa, wb, wc, wo)


def _ffn_kernel(x_ref, mod_ref, g_ref, wgu_ref, wd_ref, fg_ref, o_ref, *, final_norm):
    x = x_ref[...]
    h = _rms(x) * g_ref[...] * (1.0 + mod_ref[4:5, :]) + mod_ref[3:4, :]
    hb = h.astype(BF16)
    acc = jnp.zeros(x.shape, F32)
    for c0 in range(0, FFN_HIDDEN, FFN_CHUNK):
        gate = _dot(hb, wgu_ref[:, c0:c0 + FFN_CHUNK])
        up = _dot(hb, wgu_ref[:, FFN_HIDDEN + c0:FFN_HIDDEN + c0 + FFN_CHUNK])
        act = (gate * _sigmoid(gate) * up).astype(BF16)
        acc = acc + _dot(act, wd_ref[c0:c0 + FFN_CHUNK, :])
    y = x + mod_ref[5:6, :] * acc
    if final_norm:
        y = _rms(y) * fg_ref[...]
    o_ref[...] = y


def _ffn_call(x, mod, l, g, wgu, wd, fg, final_norm):
    B, S, D = x.shape
    tm = TOKEN_TILE
    tok = pl.BlockSpec((None, tm, D), lambda b, i: (b, i, 0))
    full = lambda a: pl.BlockSpec(a.shape, lambda b, i: (0,) * a.ndim)
    return pl.pallas_call(
        functools.partial(_ffn_kernel, final_norm=final_norm),
        grid=(B, S // tm),
        in_specs=[tok, _mod_spec(mod, l), _layer_spec(g, l), _layer_spec(wgu, l), _layer_spec(wd, l), full(fg)],
        out_specs=tok,
        out_shape=jax.ShapeDtypeStruct(x.shape, F32),
        compiler_params=pltpu.CompilerParams(dimension_semantics=("parallel", "parallel"),
                                             vmem_limit_bytes=VMEM_LIMIT),
        name="swiglu",
    )(x, mod, g, wgu, wd, fg)


def _rope_tables(S):
    t = jnp.arange(S, dtype=jnp.int32)
    pos_t = t.astype(F32)
    pos_row = (t // GRID_W).astype(F32)
    pos_col = (t % GRID_W).astype(F32)
    half = 16
    lane = np.arange(LANES)

    def tables(pos_of_lane, roped, first_half, inv_freq):
        ang = pos_of_lane * inv_freq[lane % half][None, :]
        cos, sin = jnp.cos(ang), jnp.sin(ang)
        c = jnp.where(roped[None, :], cos, 1.0)
        up = jnp.where((roped & first_half)[None, :], -sin, 0.0)
        dn = jnp.where((roped & ~first_half)[None, :], sin, 0.0)
        return c.astype(F32), up.astype(F32), dn.astype(F32)

    inv_t = 1.0 / (ROPE_THETA ** (jnp.arange(half, dtype=F32) / half))
    inv_a = 1.0 / (AXIAL_THETA ** (jnp.arange(half, dtype=F32) / half))
    m_roped = (lane >= MLA_NOPE) & (lane < MLA_NOPE + MLA_ROPE)
    m_first = ((lane - MLA_NOPE) % 32) < half
    mt = tables(jnp.broadcast_to(pos_t[:, None], (S, LANES)), m_roped, m_first, inv_t)
    a_first = (lane % 32) < half
    use_row = (lane % GRID_W) < 32
    pos_rc = jnp.where(use_row[None, :], pos_row[:, None], pos_col[:, None])
    at = tables(pos_rc, np.ones(LANES, bool), a_first, inv_a)
    return mt, at


def _na_bias_table(rpb):
    L, W = rpb.shape[0], GRID_W
    pad = W - NA_KW
    ext = jnp.concatenate([jnp.repeat(rpb[..., :1], pad, -1), rpb, jnp.repeat(rpb[..., -1:], pad, -1)], -1)
    t = jnp.stack([ext[..., W - 1 - c:2 * W - 1 - c] for c in range(W)], axis=-2)
    c = np.arange(W)[:, None, None]
    kc = np.arange(W)[None, None, :]
    ws = np.clip(c - NA_KW // 2, 0, W - NA_KW)
    valid = (kc >= ws) & (kc < ws + NA_KW)
    out = []
    for d in range(NA_KH):
        td = jnp.transpose(t[:, :, NA_KH - 1 - d:2 * NA_KH - 1 - d], (0, 1, 3, 2, 4))
        out.append(jnp.where(valid, td.astype(F32), MASK_VALUE))
    return jnp.stack(out, axis=1).reshape(L, NA_KH, NA_HEADS * W, NA_KH * W)


def _prep_w_in(w):
    c = lambda a, b: w[..., a:b].astype(BF16)
    z = lambda n: jnp.zeros(w.shape[:-1] + (n,), BF16)
    qc0 = 1184
    qslabs = []
    for j in range(4):
        qslabs += [c(qc0 + 64 * j, qc0 + 64 * (j + 1)), c(qc0 + 64 * (4 + j), qc0 + 64 * (5 + j))]
    cols = [c(0, 1152), z(64), c(1152, 1184), z(32)] + qslabs + [c(1696, 5024)]
    return jnp.concatenate(cols, axis=-1)


def _prep_w_uq(w):
    z = jnp.zeros(w.shape[:-1] + (32,), BF16)
    hd = MLA_NOPE + MLA_ROPE
    return jnp.concatenate([a for h in range(MLA_HEADS) for a in (w[..., hd * h:hd * (h + 1)].astype(BF16), z)],
                           axis=-1)


def _prep_w_ukv(w):
    z = jnp.zeros(w.shape[:-1] + (64,), BF16)
    hd = MLA_NOPE + MLA_V
    ks = [a for h in range(MLA_HEADS) for a in (w[..., hd * h:hd * h + MLA_NOPE].astype(BF16), z)]
    vs = [w[..., hd * h + MLA_NOPE:hd * (h + 1)].astype(BF16) for h in range(MLA_HEADS)]
    return jnp.concatenate(ks + vs, axis=-1)


def _prep_w_br_gqa(w):
    rows = []
    for j in range(4):
        rows += [w[:, 64 * j:64 * (j + 1)], w[:, 64 * (4 + j):64 * (5 + j)]]
    return jnp.concatenate([r.astype(BF16) for r in rows], axis=1)


def kernel(x, c, ada_w, ada_b, norm_mix_g, norm_ffn_g, w_in, na_rpb, mla_q_norm_g, mla_kv_norm_g, mla_w_uq,
           mla_w_ukv, gqa_q_norm_g, gqa_k_norm_g, w_br_na, w_br_mla, w_br_gqa, w_out, ffn_w_gate_up,
           ffn_w_down, final_norm_g):
    B, S, D = x.shape
    L = ada_w.shape[0]
    mod = _ada_call(c, ada_w, ada_b).reshape(L, B, N_MOD, D)
    mtabs, atabs = _rope_tables(S)
    lane = np.arange(LANES)
    bd = jnp.asarray(np.tile((lane[:, None] // 64 == lane[None, :] // 64) / 64.0, (2, 1)), BF16)
    row = lambda a: a.reshape(L, 1, -1)
    gqk = jnp.concatenate([jnp.tile(gqa_q_norm_g, (1, GQA_HEADS)), jnp.tile(gqa_k_norm_g, (1, GQA_KV_HEADS))], 1)
    w_in_p, w_uq_p, w_ukv_p = _prep_w_in(w_in), _prep_w_uq(mla_w_uq), _prep_w_ukv(mla_w_ukv)
    bias = _na_bias_table(na_rpb)
    wa, wb, wc, wo = w_br_na.astype(BF16), w_br_mla.astype(BF16), _prep_w_br_gqa(w_br_gqa), w_out.astype(BF16)
    wgu, wd = ffn_w_gate_up.astype(BF16), ffn_w_down.astype(BF16)
    fg = final_norm_g.reshape(1, D)
    for l in range(L):
        qna, kna, vna, qm, km, vm, qg, kg, vg, gates = _in_call(
            x, mod, l, row(norm_mix_g), w_in_p, row(mla_q_norm_g), row(mla_kv_norm_g), w_uq_p, w_ukv_p,
            row(gqk), bd, mtabs, atabs)
        o_na = _na_call(qna, kna, vna, bias, l)
        o_mla = _attn_call(_mla_attn_kernel, qm, km, vm, MLA_HEADS * MLA_V,
                           [pltpu.VMEM((S, MLA_HEADS * LANES), BF16),
                            pltpu.VMEM((MLA_HEADS // 2, S, 2 * LANES), BF16)], "mla_attn")
        o_gqa = _attn_call(_gqa_attn_kernel, qg, kg, vg, GQA_HEADS * GQA_HEAD_DIM,
                           [pltpu.VMEM((S, 2 * LANES), BF16), pltpu.VMEM((S, 2 * LANES), BF16)], "gqa_attn")
        x = _merge_call(x, mod, l, o_na, o_mla, o_gqa, gates, wa, wb, wc, wo)
        x = _ffn_call(x, mod, l, row(norm_ffn_g), wgu, wd, fg, final_norm=(l == L - 1))
    return x
```

```python
import functools

import numpy as np
import jax
import jax.numpy as jnp
from jax import lax
from jax.experimental import pallas as pl
from jax.experimental.pallas import tpu as pltpu

D_MODEL = 1024
GRID_W = 64
NA_HEADS = 4
NA_HEAD_DIM = 64
NA_KH = 8
NA_KW = 16
MLA_HEADS = 4
MLA_Q_LORA = 256
MLA_KV_LORA = 128
MLA_NOPE = 64
MLA_ROPE = 32
MLA_V = 64
GQA_HEADS = 8
GQA_KV_HEADS = 2
GQA_HEAD_DIM = 64
FFN_HIDDEN = 2816
ROPE_THETA = 10000.0
AXIAL_THETA = 10000.0
NORM_EPS = 1e-6
N_MOD = 6
MASK_VALUE = -1e30
LOG2E = 1.4426950408889634

LANES = 128
VMEM_LIMIT = 56 * 1024 * 1024
TOKEN_TILE = 512
MLA_Q_TILE = 512
GQA_Q_TILE = 256
FFN_CHUNK = 1408

C_NA = 0
C_MLA = 768
C_GQA = 1280
C_GATE = 2048
C_TOTAL = 5120

F32 = jnp.float32
BF16 = jnp.bfloat16


def _dot(a, b):
    return jnp.dot(a, b, preferred_element_type=F32)


def _dot_nt(a, b):
    return lax.dot_general(a, b, (((1,), (1,)), ((), ())), preferred_element_type=F32)


def _rms(x):
    return x * lax.rsqrt(jnp.mean(x * x, axis=-1, keepdims=True) + NORM_EPS)


def _rope_slab(x, c, s_up, s_dn):
    return x * c + pltpu.roll(x, LANES - 16, 1) * s_up + pltpu.roll(x, 16, 1) * s_dn


def _sigmoid(x):
    return 1.0 / (1.0 + jnp.exp(-x))


def _layer_spec(a, l):
    return pl.BlockSpec((None,) + a.shape[1:], lambda *_: (l,) + (0,) * (a.ndim - 1))


def _mod_spec(mod, l):
    return pl.BlockSpec((None, None) + mod.shape[2:], lambda b, *_: (l, b, 0, 0))


def _ada_kernel(c_ref, w_ref, b_ref, o_ref):
    c = c_ref[...]
    s = (c * _sigmoid(c)).astype(BF16)
    o_ref[...] = _dot(s, w_ref[...].astype(BF16)) + b_ref[...]


def _ada_call(c, ada_w, ada_b):
    L, D, N = ada_w.shape
    B = c.shape[0]
    tn = 1536
    return pl.pallas_call(
        _ada_kernel,
        grid=(L, N // tn),
        in_specs=[pl.BlockSpec((B, D), lambda l, j: (0, 0)),
                  pl.BlockSpec((None, D, tn), lambda l, j: (l, 0, j)),
                  pl.BlockSpec((None, 1, tn), lambda l, j: (l, 0, j))],
        out_specs=pl.BlockSpec((None, B, tn), lambda l, j: (l, 0, j)),
        out_shape=jax.ShapeDtypeStruct((L, B, N), F32),
        compiler_params=pltpu.CompilerParams(dimension_semantics=("parallel", "parallel"),
                                             vmem_limit_bytes=VMEM_LIMIT),
        name="ada_mod",
    )(c, ada_w, ada_b.reshape(L, 1, N))


def _in_kernel(x_ref, mod_ref, g_ref, w_ref, qng_ref, kvng_ref, wuq_ref, wukv_ref, gqk_ref, bd_ref,
               mc_ref, mu_ref, md_ref, ac_ref, au_ref, ad_ref,
               qna_ref, kna_ref, vna_ref, qm_ref, km_ref, vm_ref, qg_ref, kg_ref, vg_ref, gate_ref):
    x = x_ref[...]
    h = _rms(x) * g_ref[...] * (1.0 + mod_ref[1:2, :]) + mod_ref[0:1, :]
    hb = h.astype(BF16)

    def proj(a, b):
        return _dot(hb, w_ref[:, a:b])

    p = proj(C_NA, C_NA + 768)
    qna_ref[...] = (p[:, :256] * NA_HEAD_DIM ** -0.5).astype(BF16)
    kna_ref[...] = p[:, 256:512].astype(BF16)
    vna_ref[...] = p[:, 512:768].astype(BF16)

    p = proj(C_MLA, C_MLA + 512)
    mc, mu, md = mc_ref[...], mu_ref[...], md_ref[...]
    nq = (_rms(p[:, :256]) * qng_ref[...]).astype(BF16)
    qb = _dot(nq, wuq_ref[...])
    mla_scale = (MLA_NOPE + MLA_ROPE) ** -0.5 * LOG2E
    for s in range(MLA_HEADS):
        sl = slice(s * LANES, (s + 1) * LANES)
        qm_ref[:, sl] = (_rope_slab(qb[:, sl], mc, mu, md) * mla_scale).astype(BF16)
    nkv = (_rms(p[:, 256:384]) * kvng_ref[...]).astype(BF16)
    kvb = _dot(nkv, wukv_ref[...])
    kpe = _rope_slab(p[:, 384:512], mc, mu, md)
    for s in range(MLA_HEADS):
        sl = slice(s * LANES, (s + 1) * LANES)
        km_ref[:, sl] = (kvb[:, sl] + kpe).astype(BF16)
    vm_ref[...] = kvb[:, 512:768].astype(BF16)

    p = proj(C_GQA, C_GQA + 768)
    ac, au, ad = ac_ref[...], au_ref[...], ad_ref[...]
    bd = bd_ref[...]
    for s in range(5):
        sl = slice(s * LANES, (s + 1) * LANES)
        v = p[:, sl]
        sq = v * v
        hi = sq.astype(BF16)
        lo = (sq - hi.astype(F32)).astype(BF16)
        msq = _dot(jnp.concatenate([hi, lo], axis=1), bd)
        y = v * lax.rsqrt(msq + NORM_EPS) * gqk_ref[:, sl]
        r = _rope_slab(y, ac, au, ad)
        if s < 4:
            qg_ref[:, sl] = (r * (GQA_HEAD_DIM ** -0.5 * LOG2E)).astype(BF16)
        else:
            kg_ref[...] = r.astype(BF16)
    vg_ref[...] = p[:, 640:768].astype(BF16)

    for j in range(3):
        g = proj(C_GATE + j * D_MODEL, C_GATE + (j + 1) * D_MODEL)
        gate_ref[:, j * D_MODEL:(j + 1) * D_MODEL] = _sigmoid(g).astype(BF16)


def _in_call(x, mod, l, g, w, qng, kvng, wuq, wukv, gqk, bd, mtabs, atabs):
    B, S, D = x.shape
    tm = TOKEN_TILE
    tok = lambda w_: pl.BlockSpec((None, tm, w_), lambda b, i: (b, i, 0))
    full = lambda a: pl.BlockSpec(a.shape, lambda b, i: (0,) * a.ndim)
    tab = pl.BlockSpec((tm, LANES), lambda b, i: (i, 0))
    widths = (256, 256, 256, 512, 512, 256, 512, 128, 128, 3 * D)
    return pl.pallas_call(
        _in_kernel,
        grid=(B, S // tm),
        in_specs=[tok(D), _mod_spec(mod, l)] + [_layer_spec(a, l) for a in (g, w, qng, kvng, wuq, wukv, gqk)]
        + [full(bd)] + [tab] * 6,
        out_specs=[tok(w_) for w_ in widths],
        out_shape=[jax.ShapeDtypeStruct((B, S, w_), BF16) for w_ in widths],
        compiler_params=pltpu.CompilerParams(dimension_semantics=("parallel", "parallel"),
                                             vmem_limit_bytes=VMEM_LIMIT),
        name="in_proj",
    )(x, mod, g, w, qng, kvng, wuq, wukv, gqk, bd, *mtabs, *atabs)


def _na_kernel(q_ref, k_ref, v_ref, bias_ref, o_ref, *, rows):
    kh = NA_KH
    head_of_lane = lax.broadcasted_iota(jnp.int32, (GRID_W, NA_HEADS * NA_HEAD_DIM), 1) // NA_HEAD_DIM

    def body(r, carry):
        rs = jnp.clip(r - kh // 2, 0, rows - kh)
        q = q_ref[pl.ds(pl.multiple_of(r * GRID_W, GRID_W), GRID_W), :]
        kstart = pl.multiple_of(rs * GRID_W, GRID_W)
        kb = k_ref[pl.ds(kstart, kh * GRID_W), :]
        vb = v_ref[pl.ds(kstart, kh * GRID_W), :]
        qs = jnp.concatenate([jnp.where(head_of_lane == h, q, jnp.zeros_like(q)) for h in range(NA_HEADS)],
                             axis=0)
        s = _dot_nt(qs, kb) + bias_ref[r - rs]
        m = jnp.max(s, axis=-1, keepdims=True)
        p = jnp.exp(s - m)
        l = jnp.sum(p, axis=-1, keepdims=True)
        o = _dot(p.astype(BF16), vb) / l
        out = jnp.zeros((GRID_W, NA_HEADS * NA_HEAD_DIM), F32)
        for h in range(NA_HEADS):
            out = jnp.where(head_of_lane == h, o[h * GRID_W:(h + 1) * GRID_W, :], out)
        o_ref[pl.ds(pl.multiple_of(r * GRID_W, GRID_W), GRID_W), :] = out.astype(BF16)
        return carry

    lax.fori_loop(0, rows, body, 0, unroll=8)


def _na_call(q, k, v, bias, l):
    B, S, W = q.shape
    rows = S // GRID_W
    blk = pl.BlockSpec((None, S, W), lambda b: (b, 0, 0))
    return pl.pallas_call(
        functools.partial(_na_kernel, rows=rows),
        grid=(B,),
        in_specs=[blk, blk, blk, _layer_spec(bias, l)],
        out_specs=blk,
        out_shape=jax.ShapeDtypeStruct((B, S, W), BF16),
        compiler_params=pltpu.CompilerParams(dimension_semantics=("parallel",),
                                             vmem_limit_bytes=VMEM_LIMIT),
        name="na_attn",
    )(q, k, v, bias)


L_MIN = 2.0 ** -60
MLA_BIAS_LANE = MLA_NOPE + MLA_ROPE


def _softmax_pv(q, k, v):
    s = _dot_nt(q, k)
    m = jnp.max(s, axis=-1, keepdims=True)
    p = jnp.exp2(s - m)
    l = jnp.sum(p, axis=-1, keepdims=True)
    return _dot(p.astype(BF16), v) / l


VT_ROWS = 80


def _shifted_head(q_aug, k_aug, vt_ext):
    pt = jnp.exp2(_dot_nt(k_aug, q_aug)).astype(BF16)
    ot = _dot(vt_ext, pt)
    hd = VT_ROWS - 16
    l = ot[hd:hd + 1, :]
    return ot[0:hd, :] / l, l


def _store_vt(vt_ref, idx, vt_rows):
    hd = VT_ROWS - 16
    S = vt_rows.shape[1]
    vt_ref[idx, 0:hd, :] = vt_rows.astype(BF16)
    first = lax.broadcasted_iota(jnp.int32, (16, S), 0) == 0
    vt_ref[idx, hd:VT_ROWS, :] = jnp.where(first, 1.0, 0.0).astype(BF16)


def _row_norm(x):
    xf = x.astype(F32)
    return jnp.sqrt(jnp.sum(xf * xf, axis=-1, keepdims=True))


def _max_row_norm(x, mask=None):
    xf = x.astype(F32)
    sq = xf * xf if mask is None else jnp.where(mask, xf * xf, 0.0)
    n = jnp.max(jnp.sum(sq, axis=-1, keepdims=True), axis=0, keepdims=True)
    return jnp.broadcast_to(jnp.sqrt(n), (1, LANES))


def _mla_exact(q_ref, k_ref, v_ref, o_ref):
    v = v_ref[...]
    head_of_lane = lax.broadcasted_iota(jnp.int32, o_ref.shape, 1) // MLA_V
    out = jnp.zeros(o_ref.shape, F32)
    for h in range(MLA_HEADS):
        sl = slice(h * LANES, (h + 1) * LANES)
        out = jnp.where(head_of_lane == h, _softmax_pv(q_ref[:, sl], k_ref[:, sl], v), out)
    o_ref[...] = out.astype(BF16)


def _mla_attn_kernel(q_ref, k_ref, v_ref, o_ref, kaug_ref, vt_ref, kmax_ref):
    S, tq = k_ref.shape[0], q_ref.shape[0]

    @pl.when(pl.program_id(1) == 0)
    def _():
        lane_s = lax.broadcasted_iota(jnp.int32, (S, LANES), 1)
        for h in range(MLA_HEADS):
            sl = slice(h * LANES, (h + 1) * LANES)
            k = k_ref[:, sl]
            kaug_ref[:, sl] = jnp.where(lane_s == MLA_BIAS_LANE, jnp.ones_like(k), k)
            kmax_ref[h:h + 1, :] = _max_row_norm(k)
        for p in range(MLA_HEADS // 2):
            vt = jnp.transpose(v_ref[:, p * LANES:(p + 1) * LANES].astype(F32))
            _store_vt(vt_ref, 2 * p, vt[0:MLA_V])
            _store_vt(vt_ref, 2 * p + 1, vt[MLA_V:2 * MLA_V])

    lane = lax.broadcasted_iota(jnp.int32, (tq, LANES), 1)
    lmin = None
    for p in range(MLA_HEADS // 2):
        outs = []
        for h in (2 * p, 2 * p + 1):
            sl = slice(h * LANES, (h + 1) * LANES)
            q = q_ref[:, sl]
            b = _row_norm(q) * kmax_ref[h:h + 1, 0:1]
            q_aug = jnp.where(lane == MLA_BIAS_LANE, (-b).astype(BF16), q)
            o, l = _shifted_head(q_aug, kaug_ref[:, sl], vt_ref[h])
            outs.append(o)
            lmin = l if lmin is None else jnp.minimum(lmin, l)
        o_ref[:, p * LANES:(p + 1) * LANES] = jnp.transpose(jnp.concatenate(outs, axis=0)).astype(BF16)

    @pl.when(jnp.min(lmin) < L_MIN)
    def _():
        _mla_exact(q_ref, k_ref, v_ref, o_ref)


def _gqa_exact(q_ref, k_ref, v_ref, o_ref):
    k = k_ref[...]
    v = v_ref[...]
    low = lax.broadcasted_iota(jnp.int32, (q_ref.shape[0], LANES), 1) < GQA_HEAD_DIM
    for j in range(GQA_HEADS // GQA_KV_HEADS):
        sl = slice(j * LANES, (j + 1) * LANES)
        q = q_ref[:, sl]
        zero = jnp.zeros_like(q)
        o_lo = _softmax_pv(jnp.where(low, q, zero), k, v)
        o_hi = _softmax_pv(jnp.where(low, zero, q), k, v)
        o_ref[:, sl] = jnp.where(low, o_lo, o_hi).astype(BF16)


def _gqa_attn_kernel(q_ref, k_ref, v_ref, o_ref, kaug_ref, vt_ref, kmax_ref):
    S, tq = k_ref.shape[0], q_ref.shape[0]

    @pl.when(pl.program_id(1) == 0)
    def _():
        lane_s = lax.broadcasted_iota(jnp.int32, (S, LANES), 1)
        k = k_ref[...]
        kaug_ref[:, :LANES] = k
        kaug_ref[:, LANES:] = jnp.where(lane_s == 0, 1.0, 0.0).astype(BF16)
        vt = jnp.transpose(v_ref[...].astype(F32))
        for g in range(GQA_KV_HEADS):
            _store_vt(vt_ref, g, vt[g * GQA_HEAD_DIM:(g + 1) * GQA_HEAD_DIM])
        kmax_ref[0:1, :] = _max_row_norm(k, lane_s < GQA_HEAD_DIM)
        kmax_ref[1:2, :] = _max_row_norm(k, lane_s >= GQA_HEAD_DIM)

    lane = lax.broadcasted_iota(jnp.int32, (tq, LANES), 1)
    low = lane < GQA_HEAD_DIM
    pairs = GQA_HEADS // GQA_KV_HEADS
    q_aug = [[], []]
    for j in range(pairs):
        q = q_ref[:, j * LANES:(j + 1) * LANES]
        zero = jnp.zeros_like(q)
        for half in range(GQA_KV_HEADS):
            qm = jnp.where(low, q, zero) if half == 0 else jnp.where(low, zero, q)
            b = _row_norm(qm) * kmax_ref[half:half + 1, 0:1]
            bias = jnp.where(lane == 0, -b, 0.0).astype(BF16)
            q_aug[half].append(jnp.concatenate([qm, bias], axis=1))
    pt = jnp.exp2(_dot_nt(kaug_ref[...], jnp.concatenate(q_aug[0] + q_aug[1], axis=0))).astype(BF16)
    n = pairs * tq
    outs, lmin = [], None
    for half in range(GQA_KV_HEADS):
        ot = _dot(vt_ref[half], pt[:, half * n:(half + 1) * n])
        l = ot[GQA_HEAD_DIM:GQA_HEAD_DIM + 1, :]
        outs.append(ot[0:GQA_HEAD_DIM, :] / l)
        lmin = l if lmin is None else jnp.minimum(lmin, l)
    for j in range(pairs):
        ot = jnp.concatenate([o[:, j * tq:(j + 1) * tq] for o in outs], axis=0)
        o_ref[:, j * LANES:(j + 1) * LANES] = jnp.transpose(ot).astype(BF16)

    @pl.when(jnp.min(lmin) < L_MIN)
    def _():
        _gqa_exact(q_ref, k_ref, v_ref, o_ref)


def _attn_call(kernel, q, k, v, out_width, tq, scratch, name):
    B, S, _ = q.shape
    return pl.pallas_call(
        kernel,
        grid=(B, S // tq),
        in_specs=[pl.BlockSpec((None, tq, q.shape[2]), lambda b, i: (b, i, 0)),
                  pl.BlockSpec((None, S, k.shape[2]), lambda b, i: (b, 0, 0)),
                  pl.BlockSpec((None, S, v.shape[2]), lambda b, i: (b, 0, 0))],
        out_specs=pl.BlockSpec((None, tq, out_width), lambda b, i: (b, i, 0)),
        out_shape=jax.ShapeDtypeStruct((B, S, out_width), BF16),
        scratch_shapes=scratch + [pltpu.VMEM((8, LANES), F32)],
        compiler_params=pltpu.CompilerParams(dimension_semantics=("parallel", "arbitrary"),
                                             vmem_limit_bytes=VMEM_LIMIT),
        name=name,
    )(q, k, v)


def _merge_kernel(x_ref, mod_ref, ona_ref, omla_ref, ogqa_ref, gate_ref, wa_ref, wb_ref, wc_ref, wo_ref,
                  o_ref):
    D = D_MODEL
    m = (gate_ref[:, 0:D].astype(F32) * _dot(ona_ref[...], wa_ref[...])
         + gate_ref[:, D:2 * D].astype(F32) * _dot(omla_ref[...], wb_ref[...])
         + gate_ref[:, 2 * D:3 * D].astype(F32) * _dot(ogqa_ref[...], wc_ref[...]))
    y = _dot(m.astype(BF16), wo_ref[...])
    o_ref[...] = x_ref[...] + mod_ref[2:3, :] * y


def _merge_call(x, mod, l, ona, omla, ogqa, gates, wa, wb, wc, wo):
    B, S, D = x.shape
    tm = TOKEN_TILE
    tok = lambda a: pl.BlockSpec((None, tm, a.shape[2]), lambda b, i: (b, i, 0))
    full = lambda a: pl.BlockSpec(a.shape, lambda b, i: (0,) * a.ndim)
    return pl.pallas_call(
        _merge_kernel,
        grid=(B, S // tm),
        in_specs=[tok(x), _mod_spec(mod, l), tok(ona), tok(omla), tok(ogqa), tok(gates)]
        + [_layer_spec(a, l) for a in (wa, wb, wc, wo)],
        out_specs=tok(x),
        out_shape=jax.ShapeDtypeStruct(x.shape, F32),
        compiler_params=pltpu.CompilerParams(dimension_semantics=("parallel", "parallel"),
                                             vmem_limit_bytes=VMEM_LIMIT),
        name="merge_out",
    )(x, mod, ona, omla, ogqa, gates, wa, wb, wc, wo)


def _ffn_kernel(x_ref, mod_ref, g_ref, wgu_ref, wd_ref, fg_ref, o_ref, *, final_norm):
    x = x_ref[...]
    h = _rms(x) * g_ref[...] * (1.0 + mod_ref[4:5, :]) + mod_ref[3:4, :]
    hb = h.astype(BF16)
    acc = jnp.zeros(x.shape, F32)
    for c0 in range(0, FFN_HIDDEN, FFN_CHUNK):
        gate = _dot(hb, wgu_ref[:, c0:c0 + FFN_CHUNK])
        up = _dot(hb, wgu_ref[:, FFN_HIDDEN + c0:FFN_HIDDEN + c0 + FFN_CHUNK])
        act = (gate * _sigmoid(gate) * up).astype(BF16)
        acc = acc + _dot(act, wd_ref[c0:c0 + FFN_CHUNK, :])
    y = x + mod_ref[5:6, :] * acc
    if final_norm:
        y = _rms(y) * fg_ref[...]
    o_ref[...] = y


def _ffn_call(x, mod, l, g, wgu, wd, fg, final_norm):
    B, S, D = x.shape
    tm = TOKEN_TILE
    tok = pl.BlockSpec((None, tm, D), lambda b, i: (b, i, 0))
    full = lambda a: pl.BlockSpec(a.shape, lambda b, i: (0,) * a.ndim)
    return pl.pallas_call(
        functools.partial(_ffn_kernel, final_norm=final_norm),
        grid=(B, S // tm),
        in_specs=[tok, _mod_spec(mod, l), _layer_spec(g, l), _layer_spec(wgu, l), _layer_spec(wd, l), full(fg)],
        out_specs=tok,
        out_shape=jax.ShapeDtypeStruct(x.shape, F32),
        compiler_params=pltpu.CompilerParams(dimension_semantics=("parallel", "parallel"),
                                             vmem_limit_bytes=VMEM_LIMIT),
        name="swiglu",
    )(x, mod, g, wgu, wd, fg)


def _rope_tables(S):
    t = jnp.arange(S, dtype=jnp.int32)
    pos_t = t.astype(F32)
    pos_row = (t // GRID_W).astype(F32)
    pos_col = (t % GRID_W).astype(F32)
    half = 16
    lane = np.arange(LANES)

    def tables(pos_of_lane, roped, first_half, inv_freq):
        ang = pos_of_lane * inv_freq[lane % half][None, :]
        cos, sin = jnp.cos(ang), jnp.sin(ang)
        c = jnp.where(roped[None, :], cos, 1.0)
        up = jnp.where((roped & first_half)[None, :], -sin, 0.0)
        dn = jnp.where((roped & ~first_half)[None, :], sin, 0.0)
        return c.astype(F32), up.astype(F32), dn.astype(F32)

    inv_t = 1.0 / (ROPE_THETA ** (jnp.arange(half, dtype=F32) / half))
    inv_a = 1.0 / (AXIAL_THETA ** (jnp.arange(half, dtype=F32) / half))
    m_roped = (lane >= MLA_NOPE) & (lane < MLA_NOPE + MLA_ROPE)
    m_first = ((lane - MLA_NOPE) % 32) < half
    mt = tables(jnp.broadcast_to(pos_t[:, None], (S, LANES)), m_roped, m_first, inv_t)
    a_first = (lane % 32) < half
    use_row = (lane % GRID_W) < 32
    pos_rc = jnp.where(use_row[None, :], pos_row[:, None], pos_col[:, None])
    at = tables(pos_rc, np.ones(LANES, bool), a_first, inv_a)
    return mt, at


def _na_bias_table(rpb):
    L, W = rpb.shape[0], GRID_W
    pad = W - NA_KW
    ext = jnp.concatenate([jnp.repeat(rpb[..., :1], pad, -1), rpb, jnp.repeat(rpb[..., -1:], pad, -1)], -1)
    t = jnp.stack([ext[..., W - 1 - c:2 * W - 1 - c] for c in range(W)], axis=-2)
    c = np.arange(W)[:, None, None]
    kc = np.arange(W)[None, None, :]
    ws = np.clip(c - NA_KW // 2, 0, W - NA_KW)
    valid = (kc >= ws) & (kc < ws + NA_KW)
    out = []
    for d in range(NA_KH):
        td = jnp.transpose(t[:, :, NA_KH - 1 - d:2 * NA_KH - 1 - d], (0, 1, 3, 2, 4))
        out.append(jnp.where(valid, td.astype(F32), MASK_VALUE))
    return jnp.stack(out, axis=1).reshape(L, NA_KH, NA_HEADS * W, NA_KH * W)


W_IN_KR = 1152
W_IN_QC = 1184
W_IN_COLS = 5024


def _w_in_prep_kernel(w_ref, o_ref):
    tk = w_ref.shape[0]
    z = lambda n: jnp.zeros((tk, n), F32)
    o_ref[:, 0:W_IN_KR] = w_ref[:, 0:W_IN_KR].astype(BF16)
    o_ref[:, C_MLA + 384:C_GQA] = jnp.concatenate(
        [z(MLA_NOPE), w_ref[:, W_IN_KR:W_IN_QC], z(LANES - MLA_NOPE - MLA_ROPE)], axis=1).astype(BF16)
    rest = w_ref[:, W_IN_QC:W_IN_COLS]
    hd, pairs = GQA_HEAD_DIM, GQA_HEADS // GQA_KV_HEADS
    for j in range(pairs):
        o_ref[:, C_GQA + LANES * j:C_GQA + LANES * (j + 1)] = jnp.concatenate(
            [rest[:, hd * j:hd * (j + 1)], rest[:, hd * (pairs + j):hd * (pairs + j + 1)]], axis=1).astype(BF16)
    o_ref[:, C_GQA + 512:C_TOTAL] = rest[:, 512:].astype(BF16)


def _prep_w_in(w):
    L, D, N = w.shape
    tk = 256
    return pl.pallas_call(
        _w_in_prep_kernel,
        grid=(L, D // tk),
        in_specs=[pl.BlockSpec((None, tk, N), lambda l, i: (l, i, 0))],
        out_specs=pl.BlockSpec((None, tk, C_TOTAL), lambda l, i: (l, i, 0)),
        out_shape=jax.ShapeDtypeStruct((L, D, C_TOTAL), BF16),
        compiler_params=pltpu.CompilerParams(dimension_semantics=("parallel", "parallel"),
                                             vmem_limit_bytes=VMEM_LIMIT),
        name="w_in_prep",
    )(w)


def _prep_w_uq(w):
    z = jnp.zeros(w.shape[:-1] + (32,), BF16)
    hd = MLA_NOPE + MLA_ROPE
    return jnp.concatenate([a for h in range(MLA_HEADS) for a in (w[..., hd * h:hd * (h + 1)].astype(BF16), z)],
                           axis=-1)


def _prep_w_ukv(w):
    z = jnp.zeros(w.shape[:-1] + (64,), BF16)
    hd = MLA_NOPE + MLA_V
    ks = [a for h in range(MLA_HEADS) for a in (w[..., hd * h:hd * h + MLA_NOPE].astype(BF16), z)]
    vs = [w[..., hd * h + MLA_NOPE:hd * (h + 1)].astype(BF16) for h in range(MLA_HEADS)]
    return jnp.concatenate(ks + vs, axis=-1)


def _prep_w_br_gqa(w):
    rows = []
    for j in range(4):
        rows += [w[:, 64 * j:64 * (j + 1)], w[:, 64 * (4 + j):64 * (5 + j)]]
    return jnp.concatenate([r.astype(BF16) for r in rows], axis=1)


def kernel(x, c, ada_w, ada_b, norm_mix_g, norm_ffn_g, w_in, na_rpb, mla_q_norm_g, mla_kv_norm_g, mla_w_uq,
           mla_w_ukv, gqa_q_norm_g, gqa_k_norm_g, w_br_na, w_br_mla, w_br_gqa, w_out, ffn_w_gate_up,
           ffn_w_down, final_norm_g):
    B, S, D = x.shape
    L = ada_w.shape[0]
    mod = _ada_call(c, ada_w, ada_b).reshape(L, B, N_MOD, D)
    mtabs, atabs = _rope_tables(S)
    lane = np.arange(LANES)
    bd = jnp.asarray(np.tile((lane[:, None] // 64 == lane[None, :] // 64) / 64.0, (2, 1)), BF16)
    row = lambda a: a.reshape(L, 1, -1)
    gqk = jnp.concatenate([jnp.tile(gqa_q_norm_g, (1, GQA_HEADS)), jnp.tile(gqa_k_norm_g, (1, GQA_KV_HEADS))], 1)
    w_in_p, w_uq_p, w_ukv_p = _prep_w_in(w_in), _prep_w_uq(mla_w_uq), _prep_w_ukv(mla_w_ukv)
    bias = _na_bias_table(na_rpb)
    wa, wb, wc, wo = w_br_na.astype(BF16), w_br_mla.astype(BF16), _prep_w_br_gqa(w_br_gqa), w_out.astype(BF16)
    wgu, wd = ffn_w_gate_up.astype(BF16), ffn_w_down.astype(BF16)
    fg = final_norm_g.reshape(1, D)
    for l in range(L):
        qna, kna, vna, qm, km, vm, qg, kg, vg, gates = _in_call(
            x, mod, l, row(norm_mix_g), w_in_p, row(mla_q_norm_g), row(mla_kv_norm_g), w_uq_p, w_ukv_p,
            row(gqk), bd, mtabs, atabs)
        o_na = _na_call(qna, kna, vna, bias, l)
        o_mla = _attn_call(_mla_attn_kernel, qm, km, vm, MLA_HEADS * MLA_V, MLA_Q_TILE,
                           [pltpu.VMEM((S, MLA_HEADS * LANES), BF16),
                            pltpu.VMEM((MLA_HEADS, VT_ROWS, S), BF16)], "mla_attn")
        o_gqa = _attn_call(_gqa_attn_kernel, qg, kg, vg, GQA_HEADS * GQA_HEAD_DIM, GQA_Q_TILE,
                           [pltpu.VMEM((S, 2 * LANES), BF16), pltpu.VMEM((GQA_KV_HEADS, VT_ROWS, S), BF16)],
                           "gqa_attn")
        x = _merge_call(x, mod, l, o_na, o_mla, o_gqa, gates, wa, wb, wc, wo)
        x = _ffn_call(x, mod, l, row(norm_ffn_g), wgu, wd, fg, final_norm=(l == L - 1))
    return x
```

```python
import functools

import numpy as np
import jax
import jax.numpy as jnp
from jax import lax
from jax.experimental import pallas as pl
from jax.experimental.pallas import tpu as pltpu

D_MODEL = 1024
GRID_W = 64
NA_HEADS = 4
NA_HEAD_DIM = 64
NA_KH = 8
NA_KW = 16
MLA_HEADS = 4
MLA_Q_LORA = 256
MLA_KV_LORA = 128
MLA_NOPE = 64
MLA_ROPE = 32
MLA_V = 64
GQA_HEADS = 8
GQA_KV_HEADS = 2
GQA_HEAD_DIM = 64
FFN_HIDDEN = 2816
ROPE_THETA = 10000.0
AXIAL_THETA = 10000.0
NORM_EPS = 1e-6
N_MOD = 6
MASK_VALUE = -1e30
LOG2E = 1.4426950408889634

LANES = 128
VMEM_LIMIT = 56 * 1024 * 1024
TOKEN_TILE = 512
MXU_TILE = 256
MLA_Q_TILE = 512
GQA_Q_TILE = 256
_FFN_SPLIT = (FFN_HIDDEN // MXU_TILE + 1) // 2 * MXU_TILE
FFN_CHUNKS = ((0, _FFN_SPLIT), (_FFN_SPLIT, FFN_HIDDEN))

C_NA = 0
C_MLA = 768
C_GQA = 1280
C_GATE = 2048
C_TOTAL = 5120

F32 = jnp.float32
BF16 = jnp.bfloat16


def _dot(a, b):
    return jnp.dot(a, b, preferred_element_type=F32)


def _dot_nt(a, b):
    return lax.dot_general(a, b, (((1,), (1,)), ((), ())), preferred_element_type=F32)


def _rms(x):
    return x * lax.rsqrt(jnp.mean(x * x, axis=-1, keepdims=True) + NORM_EPS)


def _rope_slab(x, c, s_up, s_dn):
    return x * c + pltpu.roll(x, LANES - 16, 1) * s_up + pltpu.roll(x, 16, 1) * s_dn


def _sigmoid(x):
    return 1.0 / (1.0 + jnp.exp(-x))


def _layer_spec(a, l):
    return pl.BlockSpec((None,) + a.shape[1:], lambda *_: (l,) + (0,) * (a.ndim - 1))


def _mod_spec(mod, l):
    return pl.BlockSpec((None, None) + mod.shape[2:], lambda b, *_: (l, b, 0, 0))


def _ada_kernel(c_ref, w_ref, b_ref, o_ref):
    c = c_ref[...]
    s = (c * _sigmoid(c)).astype(BF16)
    o_ref[...] = _dot(s, w_ref[...].astype(BF16)) + b_ref[...]


def _ada_call(c, ada_w, ada_b):
    L, D, N = ada_w.shape
    B = c.shape[0]
    tn = 1536
    return pl.pallas_call(
        _ada_kernel,
        grid=(L, N // tn),
        in_specs=[pl.BlockSpec((B, D), lambda l, j: (0, 0)),
                  pl.BlockSpec((None, D, tn), lambda l, j: (l, 0, j)),
                  pl.BlockSpec((None, 1, tn), lambda l, j: (l, 0, j))],
        out_specs=pl.BlockSpec((None, B, tn), lambda l, j: (l, 0, j)),
        out_shape=jax.ShapeDtypeStruct((L, B, N), F32),
        compiler_params=pltpu.CompilerParams(dimension_semantics=("parallel", "parallel"),
                                             vmem_limit_bytes=VMEM_LIMIT),
        name="ada_mod",
    )(c, ada_w, ada_b.reshape(L, 1, N))


def _in_kernel(x_ref, mod_ref, g_ref, w_ref, qng_ref, kvng_ref, wuq_ref, wukv_ref, gqk_ref, bd_ref,
               mc_ref, mu_ref, md_ref, ac_ref, au_ref, ad_ref,
               qna_ref, kna_ref, vna_ref, qm_ref, km_ref, vm_ref, qg_ref, kg_ref, vg_ref, gate_ref):
    x = x_ref[...]
    h = _rms(x) * g_ref[...] * (1.0 + mod_ref[1:2, :]) + mod_ref[0:1, :]
    hb = h.astype(BF16)

    def proj(a, b):
        return _dot(hb, w_ref[:, a:b])

    p = proj(C_NA, C_NA + 768)
    qna_ref[...] = (p[:, :256] * NA_HEAD_DIM ** -0.5).astype(BF16)
    kna_ref[...] = p[:, 256:512].astype(BF16)
    vna_ref[...] = p[:, 512:768].astype(BF16)

    p = proj(C_MLA, C_MLA + 512)
    mc, mu, md = mc_ref[...], mu_ref[...], md_ref[...]
    nq = (_rms(p[:, :256]) * qng_ref[...]).astype(BF16)
    qb = _dot(nq, wuq_ref[...])
    mla_scale = (MLA_NOPE + MLA_ROPE) ** -0.5 * LOG2E
    for s in range(MLA_HEADS):
        sl = slice(s * LANES, (s + 1) * LANES)
        qm_ref[:, sl] = (_rope_slab(qb[:, sl], mc, mu, md) * mla_scale).astype(BF16)
    nkv = (_rms(p[:, 256:384]) * kvng_ref[...]).astype(BF16)
    kvb = _dot(nkv, wukv_ref[...])
    kpe = _rope_slab(p[:, 384:512], mc, mu, md)
    for s in range(MLA_HEADS):
        sl = slice(s * LANES, (s + 1) * LANES)
        km_ref[:, sl] = (kvb[:, sl] + kpe).astype(BF16)
    vm_ref[...] = kvb[:, 512:768].astype(BF16)

    p = proj(C_GQA, C_GQA + 768)
    ac, au, ad = ac_ref[...], au_ref[...], ad_ref[...]
    bd = bd_ref[...]
    for s in range(5):
        sl = slice(s * LANES, (s + 1) * LANES)
        v = p[:, sl]
        sq = v * v
        hi = sq.astype(BF16)
        lo = (sq - hi.astype(F32)).astype(BF16)
        msq = _dot(jnp.concatenate([hi, lo], axis=1), bd)
        y = v * lax.rsqrt(msq + NORM_EPS) * gqk_ref[:, sl]
        r = _rope_slab(y, ac, au, ad)
        if s < 4:
            qg_ref[:, sl] = (r * (GQA_HEAD_DIM ** -0.5 * LOG2E)).astype(BF16)
        else:
            kg_ref[...] = r.astype(BF16)
    vg_ref[...] = p[:, 640:768].astype(BF16)

    for j in range(3):
        g = proj(C_GATE + j * D_MODEL, C_GATE + (j + 1) * D_MODEL)
        gate_ref[:, j * D_MODEL:(j + 1) * D_MODEL] = _sigmoid(g).astype(BF16)


def _in_call(x, mod, l, g, w, qng, kvng, wuq, wukv, gqk, bd, mtabs, atabs):
    B, S, D = x.shape
    tm = TOKEN_TILE
    tok = lambda w_: pl.BlockSpec((None, tm, w_), lambda b, i: (b, i, 0))
    full = lambda a: pl.BlockSpec(a.shape, lambda b, i: (0,) * a.ndim)
    tab = pl.BlockSpec((tm, LANES), lambda b, i: (i, 0))
    widths = (256, 256, 256, 512, 512, 256, 512, 128, 128, 3 * D)
    return pl.pallas_call(
        _in_kernel,
        grid=(B, S // tm),
        in_specs=[tok(D), _mod_spec(mod, l)] + [_layer_spec(a, l) for a in (g, w, qng, kvng, wuq, wukv, gqk)]
        + [full(bd)] + [tab] * 6,
        out_specs=[tok(w_) for w_ in widths],
        out_shape=[jax.ShapeDtypeStruct((B, S, w_), BF16) for w_ in widths],
        compiler_params=pltpu.CompilerParams(dimension_semantics=("parallel", "parallel"),
                                             vmem_limit_bytes=VMEM_LIMIT),
        name="in_proj",
    )(x, mod, g, w, qng, kvng, wuq, wukv, gqk, bd, *mtabs, *atabs)


def _na_kernel(q_ref, k_ref, v_ref, bias_ref, o_ref, *, rows):
    kh = NA_KH
    head_of_lane = lax.broadcasted_iota(jnp.int32, (GRID_W, NA_HEADS * NA_HEAD_DIM), 1) // NA_HEAD_DIM

    def body(r, carry):
        rs = jnp.clip(r - kh // 2, 0, rows - kh)
        q = q_ref[pl.ds(pl.multiple_of(r * GRID_W, GRID_W), GRID_W), :]
        kstart = pl.multiple_of(rs * GRID_W, GRID_W)
        kb = k_ref[pl.ds(kstart, kh * GRID_W), :]
        vb = v_ref[pl.ds(kstart, kh * GRID_W), :]
        qs = jnp.concatenate([jnp.where(head_of_lane == h, q, jnp.zeros_like(q)) for h in range(NA_HEADS)],
                             axis=0)
        s = _dot_nt(qs, kb) + bias_ref[r - rs]
        m = jnp.max(s, axis=-1, keepdims=True)
        p = jnp.exp(s - m)
        l = jnp.sum(p, axis=-1, keepdims=True)
        o = _dot(p.astype(BF16), vb) / l
        out = jnp.zeros((GRID_W, NA_HEADS * NA_HEAD_DIM), F32)
        for h in range(NA_HEADS):
            out = jnp.where(head_of_lane == h, o[h * GRID_W:(h + 1) * GRID_W, :], out)
        o_ref[pl.ds(pl.multiple_of(r * GRID_W, GRID_W), GRID_W), :] = out.astype(BF16)
        return carry

    lax.fori_loop(0, rows, body, 0, unroll=32)


def _na_call(q, k, v, bias, l):
    B, S, W = q.shape
    rows = S // GRID_W
    blk = pl.BlockSpec((None, S, W), lambda b: (b, 0, 0))
    return pl.pallas_call(
        functools.partial(_na_kernel, rows=rows),
        grid=(B,),
        in_specs=[blk, blk, blk, _layer_spec(bias, l)],
        out_specs=blk,
        out_shape=jax.ShapeDtypeStruct((B, S, W), BF16),
        compiler_params=pltpu.CompilerParams(dimension_semantics=("parallel",),
                                             vmem_limit_bytes=VMEM_LIMIT),
        name="na_attn",
    )(q, k, v, bias)


L_MIN = 2.0 ** -60
MLA_BIAS_LANE = MLA_NOPE + MLA_ROPE


def _softmax_pv(q, k, v):
    s = _dot_nt(q, k)
    m = jnp.max(s, axis=-1, keepdims=True)
    p = jnp.exp2(s - m)
    l = jnp.sum(p, axis=-1, keepdims=True)
    return _dot(p.astype(BF16), v) / l


VT_ROWS = 80


def _shifted_head(q_aug, k_aug, vt_ext):
    pt = jnp.exp2(_dot_nt(k_aug, q_aug)).astype(BF16)
    ot = _dot(vt_ext, pt)
    hd = VT_ROWS - 16
    l = ot[hd:hd + 1, :]
    return ot[0:hd, :] / l, l


def _store_vt(vt_ref, idx, vt_rows):
    hd = VT_ROWS - 16
    S = vt_rows.shape[1]
    vt_ref[idx, 0:hd, :] = vt_rows.astype(BF16)
    first = lax.broadcasted_iota(jnp.int32, (16, S), 0) == 0
    vt_ref[idx, hd:VT_ROWS, :] = jnp.where(first, 1.0, 0.0).astype(BF16)


def _row_norm(x):
    xf = x.astype(F32)
    return jnp.sqrt(jnp.sum(xf * xf, axis=-1, keepdims=True))


def _max_row_norm(x, mask=None):
    xf = x.astype(F32)
    sq = xf * xf if mask is None else jnp.where(mask, xf * xf, 0.0)
    n = jnp.max(jnp.sum(sq, axis=-1, keepdims=True), axis=0, keepdims=True)
    return jnp.broadcast_to(jnp.sqrt(n), (1, LANES))


def _mla_exact(q_ref, k_ref, v_ref, o_ref):
    v = v_ref[...]
    head_of_lane = lax.broadcasted_iota(jnp.int32, o_ref.shape, 1) // MLA_V
    out = jnp.zeros(o_ref.shape, F32)
    for h in range(MLA_HEADS):
        sl = slice(h * LANES, (h + 1) * LANES)
        out = jnp.where(head_of_lane == h, _softmax_pv(q_ref[:, sl], k_ref[:, sl], v), out)
    o_ref[...] = out.astype(BF16)


def _mla_attn_kernel(q_ref, k_ref, v_ref, o_ref, kaug_ref, vt_ref, kmax_ref):
    S, tq = k_ref.shape[0], q_ref.shape[0]

    @pl.when(pl.program_id(1) == 0)
    def _():
        lane_s = lax.broadcasted_iota(jnp.int32, (S, LANES), 1)
        for h in range(MLA_HEADS):
            sl = slice(h * LANES, (h + 1) * LANES)
            k = k_ref[:, sl]
            kaug_ref[:, sl] = jnp.where(lane_s == MLA_BIAS_LANE, jnp.ones_like(k), k)
            kmax_ref[h:h + 1, :] = _max_row_norm(k)
        for p in range(MLA_HEADS // 2):
            vt = jnp.transpose(v_ref[:, p * LANES:(p + 1) * LANES].astype(F32))
            _store_vt(vt_ref, 2 * p, vt[0:MLA_V])
            _store_vt(vt_ref, 2 * p + 1, vt[MLA_V:2 * MLA_V])

    lane = lax.broadcasted_iota(jnp.int32, (tq, LANES), 1)
    lmin = None
    for p in range(MLA_HEADS // 2):
        outs = []
        for h in (2 * p, 2 * p + 1):
            sl = slice(h * LANES, (h + 1) * LANES)
            q = q_ref[:, sl]
            b = _row_norm(q) * kmax_ref[h:h + 1, 0:1]
            q_aug = jnp.where(lane == MLA_BIAS_LANE, (-b).astype(BF16), q)
            o, l = _shifted_head(q_aug, kaug_ref[:, sl], vt_ref[h])
            outs.append(o)
            lmin = l if lmin is None else jnp.minimum(lmin, l)
        o_ref[:, p * LANES:(p + 1) * LANES] = jnp.transpose(jnp.concatenate(outs, axis=0)).astype(BF16)

    @pl.when(jnp.min(lmin) < L_MIN)
    def _():
        _mla_exact(q_ref, k_ref, v_ref, o_ref)


def _gqa_exact(q_ref, k_ref, v_ref, o_ref):
    k = k_ref[...]
    v = v_ref[...]
    low = lax.broadcasted_iota(jnp.int32, (q_ref.shape[0], LANES), 1) < GQA_HEAD_DIM
    for j in range(GQA_HEADS // GQA_KV_HEADS):
        sl = slice(j * LANES, (j + 1) * LANES)
        q = q_ref[:, sl]
        zero = jnp.zeros_like(q)
        o_lo = _softmax_pv(jnp.where(low, q, zero), k, v)
        o_hi = _softmax_pv(jnp.where(low, zero, q), k, v)
        o_ref[:, sl] = jnp.where(low, o_lo, o_hi).astype(BF16)


def _gqa_attn_kernel(q_ref, k_ref, v_ref, o_ref, kaug_ref, vt_ref, kmax_ref):
    S, tq = k_ref.shape[0], q_ref.shape[0]

    @pl.when(pl.program_id(1) == 0)
    def _():
        lane_s = lax.broadcasted_iota(jnp.int32, (S, LANES), 1)
        k = k_ref[...]
        kaug_ref[:, :LANES] = k
        kaug_ref[:, LANES:] = jnp.where(lane_s == 0, 1.0, 0.0).astype(BF16)
        vt = jnp.transpose(v_ref[...].astype(F32))
        for g in range(GQA_KV_HEADS):
            _store_vt(vt_ref, g, vt[g * GQA_HEAD_DIM:(g + 1) * GQA_HEAD_DIM])
        kmax_ref[0:1, :] = _max_row_norm(k, lane_s < GQA_HEAD_DIM)
        kmax_ref[1:2, :] = _max_row_norm(k, lane_s >= GQA_HEAD_DIM)

    lane = lax.broadcasted_iota(jnp.int32, (tq, LANES), 1)
    low = lane < GQA_HEAD_DIM
    pairs = GQA_HEADS // GQA_KV_HEADS
    q_aug = [[], []]
    for j in range(pairs):
        q = q_ref[:, j * LANES:(j + 1) * LANES]
        zero = jnp.zeros_like(q)
        for half in range(GQA_KV_HEADS):
            qm = jnp.where(low, q, zero) if half == 0 else jnp.where(low, zero, q)
            b = _row_norm(qm) * kmax_ref[half:half + 1, 0:1]
            bias = jnp.where(lane == 0, -b, 0.0).astype(BF16)
            q_aug[half].append(jnp.concatenate([qm, bias], axis=1))
    pt = jnp.exp2(_dot_nt(kaug_ref[...], jnp.concatenate(q_aug[0] + q_aug[1], axis=0))).astype(BF16)
    n = pairs * tq
    outs, lmin = [], None
    for half in range(GQA_KV_HEADS):
        ot = _dot(vt_ref[half], pt[:, half * n:(half + 1) * n])
        l = ot[GQA_HEAD_DIM:GQA_HEAD_DIM + 1, :]
        outs.append(ot[0:GQA_HEAD_DIM, :] / l)
        lmin = l if lmin is None else jnp.minimum(lmin, l)
    for j in range(pairs):
        ot = jnp.concatenate([o[:, j * tq:(j + 1) * tq] for o in outs], axis=0)
        o_ref[:, j * LANES:(j + 1) * LANES] = jnp.transpose(ot).astype(BF16)

    @pl.when(jnp.min(lmin) < L_MIN)
    def _():
        _gqa_exact(q_ref, k_ref, v_ref, o_ref)


def _attn_call(kernel, q, k, v, out_width, tq, scratch, name):
    B, S, _ = q.shape
    return pl.pallas_call(
        kernel,
        grid=(B, S // tq),
        in_specs=[pl.BlockSpec((None, tq, q.shape[2]), lambda b, i: (b, i, 0)),
                  pl.BlockSpec((None, S, k.shape[2]), lambda b, i: (b, 0, 0)),
                  pl.BlockSpec((None, S, v.shape[2]), lambda b, i: (b, 0, 0))],
        out_specs=pl.BlockSpec((None, tq, out_width), lambda b, i: (b, i, 0)),
        out_shape=jax.ShapeDtypeStruct((B, S, out_width), BF16),
        scratch_shapes=scratch + [pltpu.VMEM((8, LANES), F32)],
        compiler_params=pltpu.CompilerParams(dimension_semantics=("parallel", "arbitrary"),
                                             vmem_limit_bytes=VMEM_LIMIT),
        name=name,
    )(q, k, v)


def _merge_kernel(x_ref, mod_ref, ona_ref, omla_ref, ogqa_ref, gate_ref, wa_ref, wb_ref, wc_ref, wo_ref,
                  o_ref):
    D = D_MODEL
    m = (gate_ref[:, 0:D].astype(F32) * _dot(ona_ref[...], wa_ref[...])
         + gate_ref[:, D:2 * D].astype(F32) * _dot(omla_ref[...], wb_ref[...])
         + gate_ref[:, 2 * D:3 * D].astype(F32) * _dot(ogqa_ref[...], wc_ref[...]))
    y = _dot(m.astype(BF16), wo_ref[...])
    o_ref[...] = x_ref[...] + mod_ref[2:3, :] * y


def _merge_call(x, mod, l, ona, omla, ogqa, gates, wa, wb, wc, wo):
    B, S, D = x.shape
    tm = TOKEN_TILE
    tok = lambda a: pl.BlockSpec((None, tm, a.shape[2]), lambda b, i: (b, i, 0))
    full = lambda a: pl.BlockSpec(a.shape, lambda b, i: (0,) * a.ndim)
    return pl.pallas_call(
        _merge_kernel,
        grid=(B, S // tm),
        in_specs=[tok(x), _mod_spec(mod, l), tok(ona), tok(omla), tok(ogqa), tok(gates)]
        + [_layer_spec(a, l) for a in (wa, wb, wc, wo)],
        out_specs=tok(x),
        out_shape=jax.ShapeDtypeStruct(x.shape, F32),
        compiler_params=pltpu.CompilerParams(dimension_semantics=("parallel", "parallel"),
                                             vmem_limit_bytes=VMEM_LIMIT),
        name="merge_out",
    )(x, mod, ona, omla, ogqa, gates, wa, wb, wc, wo)


def _ffn_kernel(x_ref, mod_ref, g_ref, wgu_ref, wd_ref, fg_ref, o_ref, *, final_norm):
    x = x_ref[...]
    h = _rms(x) * g_ref[...] * (1.0 + mod_ref[4:5, :]) + mod_ref[3:4, :]
    hb = h.astype(BF16)
    acc = jnp.zeros(x.shape, F32)
    for c0, c1 in FFN_CHUNKS:
        gate = _dot(hb, wgu_ref[:, c0:c1])
        up = _dot(hb, wgu_ref[:, FFN_HIDDEN + c0:FFN_HIDDEN + c1])
        act = (gate * _sigmoid(gate) * up).astype(BF16)
        acc = acc + _dot(act, wd_ref[c0:c1, :])
    y = x + mod_ref[5:6, :] * acc
    if final_norm:
        y = _rms(y) * fg_ref[...]
    o_ref[...] = y


def _ffn_call(x, mod, l, g, wgu, wd, fg, final_norm):
    B, S, D = x.shape
    tm = TOKEN_TILE
    tok = pl.BlockSpec((None, tm, D), lambda b, i: (b, i, 0))
    full = lambda a: pl.BlockSpec(a.shape, lambda b, i: (0,) * a.ndim)
    return pl.pallas_call(
        functools.partial(_ffn_kernel, final_norm=final_norm),
        grid=(B, S // tm),
        in_specs=[tok, _mod_spec(mod, l), _layer_spec(g, l), _layer_spec(wgu, l), _layer_spec(wd, l), full(fg)],
        out_specs=tok,
        out_shape=jax.ShapeDtypeStruct(x.shape, F32),
        compiler_params=pltpu.CompilerParams(dimension_semantics=("parallel", "parallel"),
                                             vmem_limit_bytes=VMEM_LIMIT),
        name="swiglu",
    )(x, mod, g, wgu, wd, fg)


def _rope_tables(S):
    f32 = np.float32
    t = np.arange(S, dtype=np.int32)
    pos_t = t.astype(f32)
    pos_row = (t // GRID_W).astype(f32)
    pos_col = (t % GRID_W).astype(f32)
    half = 16
    lane = np.arange(LANES)

    def tables(pos_of_lane, roped, first_half, inv_freq):
        ang = (pos_of_lane * inv_freq[lane % half][None, :]).astype(f32)
        cos, sin = np.cos(ang), np.sin(ang)
        c = np.where(roped[None, :], cos, f32(1.0))
        up = np.where((roped & first_half)[None, :], -sin, f32(0.0))
        dn = np.where((roped & ~first_half)[None, :], sin, f32(0.0))
        return tuple(jnp.asarray(a, F32) for a in (c, up, dn))

    inv_t = (f32(1.0) / (f32(ROPE_THETA) ** (np.arange(half, dtype=f32) / f32(half)))).astype(f32)
    inv_a = (f32(1.0) / (f32(AXIAL_THETA) ** (np.arange(half, dtype=f32) / f32(half)))).astype(f32)
    m_roped = (lane >= MLA_NOPE) & (lane < MLA_NOPE + MLA_ROPE)
    m_first = ((lane - MLA_NOPE) % 32) < half
    mt = tables(np.broadcast_to(pos_t[:, None], (S, LANES)), m_roped, m_first, inv_t)
    a_first = (lane % 32) < half
    use_row = (lane % GRID_W) < 32
    pos_rc = np.where(use_row[None, :], pos_row[:, None], pos_col[:, None])
    at = tables(pos_rc, np.ones(LANES, bool), a_first, inv_a)
    return mt, at


def _na_bias_table(rpb):
    L, W = rpb.shape[0], GRID_W
    pad = W - NA_KW
    ext = jnp.concatenate([jnp.repeat(rpb[..., :1], pad, -1), rpb, jnp.repeat(rpb[..., -1:], pad, -1)], -1)
    t = jnp.stack([ext[..., W - 1 - c:2 * W - 1 - c] for c in range(W)], axis=-2)
    c = np.arange(W)[:, None, None]
    kc = np.arange(W)[None, None, :]
    ws = np.clip(c - NA_KW // 2, 0, W - NA_KW)
    valid = (kc >= ws) & (kc < ws + NA_KW)
    out = []
    for d in range(NA_KH):
        td = jnp.transpose(t[:, :, NA_KH - 1 - d:2 * NA_KH - 1 - d], (0, 1, 3, 2, 4))
        out.append(jnp.where(valid, td.astype(F32), MASK_VALUE))
    return jnp.stack(out, axis=1).reshape(L, NA_KH, NA_HEADS * W, NA_KH * W)


W_IN_KR = 1152
W_IN_QC = 1184
W_IN_COLS = 5024


def _w_in_prep_kernel(w_ref, o_ref):
    tk = w_ref.shape[0]
    z = lambda n: jnp.zeros((tk, n), F32)
    o_ref[:, 0:W_IN_KR] = w_ref[:, 0:W_IN_KR].astype(BF16)
    o_ref[:, C_MLA + 384:C_GQA] = jnp.concatenate(
        [z(MLA_NOPE), w_ref[:, W_IN_KR:W_IN_QC], z(LANES - MLA_NOPE - MLA_ROPE)], axis=1).astype(BF16)
    rest = w_ref[:, W_IN_QC:W_IN_COLS]
    hd, pairs = GQA_HEAD_DIM, GQA_HEADS // GQA_KV_HEADS
    for j in range(pairs):
        o_ref[:, C_GQA + LANES * j:C_GQA + LANES * (j + 1)] = jnp.concatenate(
            [rest[:, hd * j:hd * (j + 1)], rest[:, hd * (pairs + j):hd * (pairs + j + 1)]], axis=1).astype(BF16)
    o_ref[:, C_GQA + 512:C_TOTAL] = rest[:, 512:].astype(BF16)


def _prep_w_in(w):
    L, D, N = w.shape
    tk = 256
    return pl.pallas_call(
        _w_in_prep_kernel,
        grid=(L, D // tk),
        in_specs=[pl.BlockSpec((None, tk, N), lambda l, i: (l, i, 0))],
        out_specs=pl.BlockSpec((None, tk, C_TOTAL), lambda l, i: (l, i, 0)),
        out_shape=jax.ShapeDtypeStruct((L, D, C_TOTAL), BF16),
        compiler_params=pltpu.CompilerParams(dimension_semantics=("parallel", "parallel"),
                                             vmem_limit_bytes=VMEM_LIMIT),
        name="w_in_prep",
    )(w)


def _prep_w_uq(w):
    z = jnp.zeros(w.shape[:-1] + (32,), BF16)
    hd = MLA_NOPE + MLA_ROPE
    return jnp.concatenate([a for h in range(MLA_HEADS) for a in (w[..., hd * h:hd * (h + 1)].astype(BF16), z)],
                           axis=-1)


def _prep_w_ukv(w):
    z = jnp.zeros(w.shape[:-1] + (64,), BF16)
    hd = MLA_NOPE + MLA_V
    ks = [a for h in range(MLA_HEADS) for a in (w[..., hd * h:hd * h + MLA_NOPE].astype(BF16), z)]
    vs = [w[..., hd * h + MLA_NOPE:hd * (h + 1)].astype(BF16) for h in range(MLA_HEADS)]
    return jnp.concatenate(ks + vs, axis=-1)


def _prep_w_br_gqa(w):
    rows = []
    for j in range(4):
        rows += [w[:, 64 * j:64 * (j + 1)], w[:, 64 * (4 + j):64 * (5 + j)]]
    return jnp.concatenate([r.astype(BF16) for r in rows], axis=1)


def kernel(x, c, ada_w, ada_b, norm_mix_g, norm_ffn_g, w_in, na_rpb, mla_q_norm_g, mla_kv_norm_g, mla_w_uq,
           mla_w_ukv, gqa_q_norm_g, gqa_k_norm_g, w_br_na, w_br_mla, w_br_gqa, w_out, ffn_w_gate_up,
           ffn_w_down, final_norm_g):
    B, S, D = x.shape
    L = ada_w.shape[0]
    mod = _ada_call(c, ada_w, ada_b).reshape(L, B, N_MOD, D)
    mtabs, atabs = _rope_tables(S)
    lane = np.arange(LANES)
    bd = jnp.asarray(np.tile((lane[:, None] // 64 == lane[None, :] // 64) / 64.0, (2, 1)), BF16)
    row = lambda a: a.reshape(L, 1, -1)
    gqk = jnp.concatenate([jnp.tile(gqa_q_norm_g, (1, GQA_HEADS)), jnp.tile(gqa_k_norm_g, (1, GQA_KV_HEADS))], 1)
    w_in_p, w_uq_p, w_ukv_p = _prep_w_in(w_in), _prep_w_uq(mla_w_uq), _prep_w_ukv(mla_w_ukv)
    bias = _na_bias_table(na_rpb)
    wa, wb, wc, wo = w_br_na.astype(BF16), w_br_mla.astype(BF16), _prep_w_br_gqa(w_br_gqa), w_out.astype(BF16)
    wgu, wd = ffn_w_gate_up.astype(BF16), ffn_w_down.astype(BF16)
    fg = final_norm_g.reshape(1, D)
    for l in range(L):
        qna, kna, vna, qm, km, vm, qg, kg, vg, gates = _in_call(
            x, mod, l, row(norm_mix_g), w_in_p, row(mla_q_norm_g), row(mla_kv_norm_g), w_uq_p, w_ukv_p,
            row(gqk), bd, mtabs, atabs)
        o_na = _na_call(qna, kna, vna, bias, l)
        o_mla = _attn_call(_mla_attn_kernel, qm, km, vm, MLA_HEADS * MLA_V, MLA_Q_TILE,
                           [pltpu.VMEM((S, MLA_HEADS * LANES), BF16),
                            pltpu.VMEM((MLA_HEADS, VT_ROWS, S), BF16)], "mla_attn")
        o_gqa = _attn_call(_gqa_attn_kernel, qg, kg, vg, GQA_HEADS * GQA_HEAD_DIM, GQA_Q_TILE,
                           [pltpu.VMEM((S, 2 * LANES), BF16), pltpu.VMEM((GQA_KV_HEADS, VT_ROWS, S), BF16)],
                           "gqa_attn")
        x = _merge_call(x, mod, l, o_na, o_mla, o_gqa, gates, wa, wb, wc, wo)
        x = _ffn_call(x, mod, l, row(norm_ffn_g), wgu, wd, fg, final_norm=(l == L - 1))
    return x
```

```python
import functools

import numpy as np
import jax
import jax.numpy as jnp
from jax import lax
from jax.experimental import pallas as pl
from jax.experimental.pallas import tpu as pltpu

D_MODEL = 1024
GRID_W = 64
NA_HEADS = 4
NA_HEAD_DIM = 64
NA_KH = 8
NA_KW = 16
MLA_HEADS = 4
MLA_Q_LORA = 256
MLA_KV_LORA = 128
MLA_NOPE = 64
MLA_ROPE = 32
MLA_V = 64
GQA_HEADS = 8
GQA_KV_HEADS = 2
GQA_HEAD_DIM = 64
FFN_HIDDEN = 2816
ROPE_THETA = 10000.0
AXIAL_THETA = 10000.0
NORM_EPS = 1e-6
N_MOD = 6
MASK_VALUE = -1e30
LOG2E = 1.4426950408889634

LANES = 128
VMEM_LIMIT = 56 * 1024 * 1024
TOKEN_TILE = 512
MXU_TILE = 256
MLA_Q_TILE = 512
GQA_Q_TILE = 256
_FFN_SPLIT = (FFN_HIDDEN // MXU_TILE + 1) // 2 * MXU_TILE
FFN_CHUNKS = ((0, _FFN_SPLIT), (_FFN_SPLIT, FFN_HIDDEN))

C_NA = 0
C_MLA = 768
C_GQA = 1280
C_GATE = 2048
C_TOTAL = 5120

F32 = jnp.float32
BF16 = jnp.bfloat16


def _dot(a, b):
    return jnp.dot(a, b, preferred_element_type=F32)


def _dot_nt(a, b):
    return lax.dot_general(a, b, (((1,), (1,)), ((), ())), preferred_element_type=F32)


def _rms(x):
    return x * lax.rsqrt(jnp.mean(x * x, axis=-1, keepdims=True) + NORM_EPS)


def _rope_slab(x, c, s_up, s_dn):
    return x * c + pltpu.roll(x, LANES - 16, 1) * s_up + pltpu.roll(x, 16, 1) * s_dn


def _sigmoid(x):
    return 1.0 / (1.0 + jnp.exp(-x))


def _layer_spec(a, l, single_buffer=False):
    mode = pl.Buffered(1) if single_buffer else None
    return pl.BlockSpec((None,) + a.shape[1:], lambda *_: (l,) + (0,) * (a.ndim - 1), pipeline_mode=mode)


def _mod_spec(mod, l):
    return pl.BlockSpec((None, None) + mod.shape[2:], lambda b, *_: (l, b, 0, 0))


def _ada_kernel(c_ref, w_ref, b_ref, o_ref):
    c = c_ref[...]
    s = (c * _sigmoid(c)).astype(BF16)
    o_ref[...] = _dot(s, w_ref[...].astype(BF16)) + b_ref[...]


def _ada_call(c, ada_w, ada_b):
    L, D, N = ada_w.shape
    B = c.shape[0]
    tn = 1536
    return pl.pallas_call(
        _ada_kernel,
        grid=(L, N // tn),
        in_specs=[pl.BlockSpec((B, D), lambda l, j: (0, 0)),
                  pl.BlockSpec((None, D, tn), lambda l, j: (l, 0, j)),
                  pl.BlockSpec((None, 1, tn), lambda l, j: (l, 0, j))],
        out_specs=pl.BlockSpec((None, B, tn), lambda l, j: (l, 0, j)),
        out_shape=jax.ShapeDtypeStruct((L, B, N), F32),
        compiler_params=pltpu.CompilerParams(dimension_semantics=("parallel", "parallel"),
                                             vmem_limit_bytes=VMEM_LIMIT),
        name="ada_mod",
    )(c, ada_w, ada_b.reshape(L, 1, N))


def _in_kernel(x_ref, mod_ref, g_ref, w_ref, qng_ref, kvng_ref, wuq_ref, wukv_ref, gqk_ref, bd_ref,
               mc_ref, mu_ref, md_ref, ac_ref, au_ref, ad_ref,
               qna_ref, kna_ref, vna_ref, qm_ref, km_ref, vm_ref, qg_ref, kg_ref, vg_ref, gate_ref):
    x = x_ref[...]
    h = _rms(x) * g_ref[...] * (1.0 + mod_ref[1:2, :]) + mod_ref[0:1, :]
    hb = h.astype(BF16)

    def proj(a, b):
        return _dot(hb, w_ref[:, a:b])

    p = proj(C_NA, C_NA + 768)
    qna_ref[...] = (p[:, :256] * NA_HEAD_DIM ** -0.5).astype(BF16)
    kna_ref[...] = p[:, 256:512].astype(BF16)
    vna_ref[...] = p[:, 512:768].astype(BF16)

    p = proj(C_MLA, C_MLA + 512)
    mc, mu, md = mc_ref[...], mu_ref[...], md_ref[...]
    nq = (_rms(p[:, :256]) * qng_ref[...]).astype(BF16)
    qb = _dot(nq, wuq_ref[...])
    mla_scale = (MLA_NOPE + MLA_ROPE) ** -0.5 * LOG2E
    for s in range(MLA_HEADS):
        sl = slice(s * LANES, (s + 1) * LANES)
        qm_ref[:, sl] = (_rope_slab(qb[:, sl], mc, mu, md) * mla_scale).astype(BF16)
    nkv = (_rms(p[:, 256:384]) * kvng_ref[...]).astype(BF16)
    kvb = _dot(nkv, wukv_ref[...])
    kpe = _rope_slab(p[:, 384:512], mc, mu, md)
    for s in range(MLA_HEADS):
        sl = slice(s * LANES, (s + 1) * LANES)
        km_ref[:, sl] = (kvb[:, sl] + kpe).astype(BF16)
    vm_ref[...] = kvb[:, 512:768].astype(BF16)

    p = proj(C_GQA, C_GQA + 768)
    ac, au, ad = ac_ref[...], au_ref[...], ad_ref[...]
    bd = bd_ref[...]
    for s in range(5):
        sl = slice(s * LANES, (s + 1) * LANES)
        v = p[:, sl]
        sq = v * v
        hi = sq.astype(BF16)
        lo = (sq - hi.astype(F32)).astype(BF16)
        msq = _dot(jnp.concatenate([hi, lo], axis=1), bd)
        y = v * lax.rsqrt(msq + NORM_EPS) * gqk_ref[:, sl]
        r = _rope_slab(y, ac, au, ad)
        if s < 4:
            qg_ref[:, sl] = (r * (GQA_HEAD_DIM ** -0.5 * LOG2E)).astype(BF16)
        else:
            kg_ref[...] = r.astype(BF16)
    vg_ref[...] = p[:, 640:768].astype(BF16)

    for j in range(3):
        g = proj(C_GATE + j * D_MODEL, C_GATE + (j + 1) * D_MODEL)
        gate_ref[:, j * D_MODEL:(j + 1) * D_MODEL] = _sigmoid(g).astype(BF16)


def _in_call(x, mod, l, g, w, qng, kvng, wuq, wukv, gqk, bd, mtabs, atabs):
    B, S, D = x.shape
    tm = TOKEN_TILE
    tok = lambda w_: pl.BlockSpec((None, tm, w_), lambda b, i: (b, i, 0))
    full = lambda a: pl.BlockSpec(a.shape, lambda b, i: (0,) * a.ndim)
    tab = pl.BlockSpec((tm, LANES), lambda b, i: (i, 0))
    widths = (256, 256, 256, 512, 512, 256, 512, 128, 128, 3 * D)
    return pl.pallas_call(
        _in_kernel,
        grid=(B, S // tm),
        in_specs=[tok(D), _mod_spec(mod, l)] + [_layer_spec(a, l) for a in (g, w, qng, kvng, wuq, wukv, gqk)]
        + [full(bd)] + [tab] * 6,
        out_specs=[tok(w_) for w_ in widths],
        out_shape=[jax.ShapeDtypeStruct((B, S, w_), BF16) for w_ in widths],
        compiler_params=pltpu.CompilerParams(dimension_semantics=("parallel", "parallel"),
                                             vmem_limit_bytes=VMEM_LIMIT),
        name="in_proj",
    )(x, mod, g, w, qng, kvng, wuq, wukv, gqk, bd, *mtabs, *atabs)


def _na_kernel(q_ref, k_ref, v_ref, bias_ref, o_ref, *, rows):
    kh = NA_KH
    head_of_lane = lax.broadcasted_iota(jnp.int32, (GRID_W, NA_HEADS * NA_HEAD_DIM), 1) // NA_HEAD_DIM

    def body(r, carry):
        rs = jnp.clip(r - kh // 2, 0, rows - kh)
        q = q_ref[pl.ds(pl.multiple_of(r * GRID_W, GRID_W), GRID_W), :]
        kstart = pl.multiple_of(rs * GRID_W, GRID_W)
        kb = k_ref[pl.ds(kstart, kh * GRID_W), :]
        vb = v_ref[pl.ds(kstart, kh * GRID_W), :]
        qs = jnp.concatenate([jnp.where(head_of_lane == h, q, jnp.zeros_like(q)) for h in range(NA_HEADS)],
                             axis=0)
        s = _dot_nt(qs, kb) + bias_ref[r - rs]
        m = jnp.max(s, axis=-1, keepdims=True)
        p = jnp.exp(s - m)
        l = jnp.sum(p, axis=-1, keepdims=True)
        o = _dot(p.astype(BF16), vb) / l
        out = jnp.zeros((GRID_W, NA_HEADS * NA_HEAD_DIM), F32)
        for h in range(NA_HEADS):
            out = jnp.where(head_of_lane == h, o[h * GRID_W:(h + 1) * GRID_W, :], out)
        o_ref[pl.ds(pl.multiple_of(r * GRID_W, GRID_W), GRID_W), :] = out.astype(BF16)
        return carry

    lax.fori_loop(0, rows, body, 0, unroll=32)


def _na_call(q, k, v, bias, l):
    B, S, W = q.shape
    rows = S // GRID_W
    blk = pl.BlockSpec((None, S, W), lambda b: (b, 0, 0))
    return pl.pallas_call(
        functools.partial(_na_kernel, rows=rows),
        grid=(B,),
        in_specs=[blk, blk, blk, _layer_spec(bias, l)],
        out_specs=blk,
        out_shape=jax.ShapeDtypeStruct((B, S, W), BF16),
        compiler_params=pltpu.CompilerParams(dimension_semantics=("parallel",),
                                             vmem_limit_bytes=VMEM_LIMIT),
        name="na_attn",
    )(q, k, v, bias)


L_MIN = 2.0 ** -60
MLA_BIAS_LANE = MLA_NOPE + MLA_ROPE


def _softmax_pv(q, k, v):
    s = _dot_nt(q, k)
    m = jnp.max(s, axis=-1, keepdims=True)
    p = jnp.exp2(s - m)
    l = jnp.sum(p, axis=-1, keepdims=True)
    return _dot(p.astype(BF16), v) / l


VT_ROWS = 80


def _shifted_head(q_aug, k_aug, vt_ext):
    pt = jnp.exp2(_dot_nt(k_aug, q_aug)).astype(BF16)
    ot = _dot(vt_ext, pt)
    hd = VT_ROWS - 16
    l = ot[hd:hd + 1, :]
    return ot[0:hd, :] / l, l


def _store_vt(vt_ref, idx, vt_rows):
    hd = VT_ROWS - 16
    S = vt_rows.shape[1]
    vt_ref[idx, 0:hd, :] = vt_rows.astype(BF16)
    first = lax.broadcasted_iota(jnp.int32, (16, S), 0) == 0
    vt_ref[idx, hd:VT_ROWS, :] = jnp.where(first, 1.0, 0.0).astype(BF16)


def _row_norm(x):
    xf = x.astype(F32)
    return jnp.sqrt(jnp.sum(xf * xf, axis=-1, keepdims=True))


def _max_row_norm(x, mask=None):
    xf = x.astype(F32)
    sq = xf * xf if mask is None else jnp.where(mask, xf * xf, 0.0)
    n = jnp.max(jnp.sum(sq, axis=-1, keepdims=True), axis=0, keepdims=True)
    return jnp.broadcast_to(jnp.sqrt(n), (1, LANES))


def _mla_exact(q_ref, k_ref, v_ref, o_ref):
    v = v_ref[...]
    head_of_lane = lax.broadcasted_iota(jnp.int32, o_ref.shape, 1) // MLA_V
    out = jnp.zeros(o_ref.shape, F32)
    for h in range(MLA_HEADS):
        sl = slice(h * LANES, (h + 1) * LANES)
        out = jnp.where(head_of_lane == h, _softmax_pv(q_ref[:, sl], k_ref[:, sl], v), out)
    o_ref[...] = out.astype(BF16)


def _mla_attn_kernel(q_ref, k_ref, v_ref, o_ref, kaug_ref, vt_ref, kmax_ref):
    S, tq = k_ref.shape[0], q_ref.shape[0]

    @pl.when(pl.program_id(1) == 0)
    def _():
        lane_s = lax.broadcasted_iota(jnp.int32, (S, LANES), 1)
        for h in range(MLA_HEADS):
            sl = slice(h * LANES, (h + 1) * LANES)
            k = k_ref[:, sl]
            kaug_ref[:, sl] = jnp.where(lane_s == MLA_BIAS_LANE, jnp.ones_like(k), k)
            kmax_ref[h:h + 1, :] = _max_row_norm(k)
        for p in range(MLA_HEADS // 2):
            vt = jnp.transpose(v_ref[:, p * LANES:(p + 1) * LANES].astype(F32))
            _store_vt(vt_ref, 2 * p, vt[0:MLA_V])
            _store_vt(vt_ref, 2 * p + 1, vt[MLA_V:2 * MLA_V])

    lane = lax.broadcasted_iota(jnp.int32, (tq, LANES), 1)
    lmin = None
    for p in range(MLA_HEADS // 2):
        outs = []
        for h in (2 * p, 2 * p + 1):
            sl = slice(h * LANES, (h + 1) * LANES)
            q = q_ref[:, sl]
            b = _row_norm(q) * kmax_ref[h:h + 1, 0:1]
            q_aug = jnp.where(lane == MLA_BIAS_LANE, (-b).astype(BF16), q)
            o, l = _shifted_head(q_aug, kaug_ref[:, sl], vt_ref[h])
            outs.append(o)
            lmin = l if lmin is None else jnp.minimum(lmin, l)
        o_ref[:, p * LANES:(p + 1) * LANES] = jnp.transpose(jnp.concatenate(outs, axis=0)).astype(BF16)

    @pl.when(jnp.min(lmin) < L_MIN)
    def _():
        _mla_exact(q_ref, k_ref, v_ref, o_ref)


def _gqa_exact(q_ref, k_ref, v_ref, o_ref):
    k = k_ref[...]
    v = v_ref[...]
    low = lax.broadcasted_iota(jnp.int32, (q_ref.shape[0], LANES), 1) < GQA_HEAD_DIM
    for j in range(GQA_HEADS // GQA_KV_HEADS):
        sl = slice(j * LANES, (j + 1) * LANES)
        q = q_ref[:, sl]
        zero = jnp.zeros_like(q)
        o_lo = _softmax_pv(jnp.where(low, q, zero), k, v)
        o_hi = _softmax_pv(jnp.where(low, zero, q), k, v)
        o_ref[:, sl] = jnp.where(low, o_lo, o_hi).astype(BF16)


def _gqa_attn_kernel(q_ref, k_ref, v_ref, o_ref, kaug_ref, vt_ref, kmax_ref):
    S, tq = k_ref.shape[0], q_ref.shape[0]

    @pl.when(pl.program_id(1) == 0)
    def _():
        lane_s = lax.broadcasted_iota(jnp.int32, (S, LANES), 1)
        k = k_ref[...]
        kaug_ref[:, :LANES] = k
        kaug_ref[:, LANES:] = jnp.where(lane_s == 0, 1.0, 0.0).astype(BF16)
        vt = jnp.transpose(v_ref[...].astype(F32))
        for g in range(GQA_KV_HEADS):
            _store_vt(vt_ref, g, vt[g * GQA_HEAD_DIM:(g + 1) * GQA_HEAD_DIM])
        kmax_ref[0:1, :] = _max_row_norm(k, lane_s < GQA_HEAD_DIM)
        kmax_ref[1:2, :] = _max_row_norm(k, lane_s >= GQA_HEAD_DIM)

    lane = lax.broadcasted_iota(jnp.int32, (tq, LANES), 1)
    low = lane < GQA_HEAD_DIM
    pairs = GQA_HEADS // GQA_KV_HEADS
    q_aug = [[], []]
    for j in range(pairs):
        q = q_ref[:, j * LANES:(j + 1) * LANES]
        zero = jnp.zeros_like(q)
        for half in range(GQA_KV_HEADS):
            qm = jnp.where(low, q, zero) if half == 0 else jnp.where(low, zero, q)
            b = _row_norm(qm) * kmax_ref[half:half + 1, 0:1]
            bias = jnp.where(lane == 0, -b, 0.0).astype(BF16)
            q_aug[half].append(jnp.concatenate([qm, bias], axis=1))
    pt = jnp.exp2(_dot_nt(kaug_ref[...], jnp.concatenate(q_aug[0] + q_aug[1], axis=0))).astype(BF16)
    n = pairs * tq
    outs, lmin = [], None
    for half in range(GQA_KV_HEADS):
        ot = _dot(vt_ref[half], pt[:, half * n:(half + 1) * n])
        l = ot[GQA_HEAD_DIM:GQA_HEAD_DIM + 1, :]
        outs.append(ot[0:GQA_HEAD_DIM, :] / l)
        lmin = l if lmin is None else jnp.minimum(lmin, l)
    for j in range(pairs):
        ot = jnp.concatenate([o[:, j * tq:(j + 1) * tq] for o in outs], axis=0)
        o_ref[:, j * LANES:(j + 1) * LANES] = jnp.transpose(ot).astype(BF16)

    @pl.when(jnp.min(lmin) < L_MIN)
    def _():
        _gqa_exact(q_ref, k_ref, v_ref, o_ref)


def _attn_call(kernel, q, k, v, out_width, tq, scratch, name):
    B, S, _ = q.shape
    return pl.pallas_call(
        kernel,
        grid=(B, S // tq),
        in_specs=[pl.BlockSpec((None, tq, q.shape[2]), lambda b, i: (b, i, 0)),
                  pl.BlockSpec((None, S, k.shape[2]), lambda b, i: (b, 0, 0)),
                  pl.BlockSpec((None, S, v.shape[2]), lambda b, i: (b, 0, 0))],
        out_specs=pl.BlockSpec((None, tq, out_width), lambda b, i: (b, i, 0)),
        out_shape=jax.ShapeDtypeStruct((B, S, out_width), BF16),
        scratch_shapes=scratch + [pltpu.VMEM((8, LANES), F32)],
        compiler_params=pltpu.CompilerParams(dimension_semantics=("parallel", "arbitrary"),
                                             vmem_limit_bytes=VMEM_LIMIT),
        name=name,
    )(q, k, v)


def _post_kernel(x_ref, mod_ref, ona_ref, omla_ref, ogqa_ref, gate_ref, wa_ref, wb_ref, wc_ref, wo_ref,
                 g_ref, wgu_ref, wd_ref, fg_ref, o_ref, *, final_norm):
    D = D_MODEL
    m = (gate_ref[:, 0:D].astype(F32) * _dot(ona_ref[...], wa_ref[...])
         + gate_ref[:, D:2 * D].astype(F32) * _dot(omla_ref[...], wb_ref[...])
         + gate_ref[:, 2 * D:3 * D].astype(F32) * _dot(ogqa_ref[...], wc_ref[...]))
    x = x_ref[...] + mod_ref[2:3, :] * _dot(m.astype(BF16), wo_ref[...])

    h = _rms(x) * g_ref[...] * (1.0 + mod_ref[4:5, :]) + mod_ref[3:4, :]
    hb = h.astype(BF16)
    acc = jnp.zeros(x.shape, F32)
    for c0, c1 in FFN_CHUNKS:
        gate = _dot(hb, wgu_ref[:, c0:c1])
        up = _dot(hb, wgu_ref[:, FFN_HIDDEN + c0:FFN_HIDDEN + c1])
        act = (gate * _sigmoid(gate) * up).astype(BF16)
        acc = acc + _dot(act, wd_ref[c0:c1, :])
    y = x + mod_ref[5:6, :] * acc
    if final_norm:
        y = _rms(y) * fg_ref[...]
    o_ref[...] = y


def _post_call(x, mod, l, ona, omla, ogqa, gates, wa, wb, wc, wo, g, wgu, wd, fg, final_norm):
    B, S, D = x.shape
    tm = TOKEN_TILE
    tok = lambda a: pl.BlockSpec((None, tm, a.shape[2]), lambda b, i: (b, i, 0))
    weights = (wa, wb, wc, wo, g, wgu, wd)
    return pl.pallas_call(
        functools.partial(_post_kernel, final_norm=final_norm),
        grid=(B, S // tm),
        in_specs=[tok(x), _mod_spec(mod, l), tok(ona), tok(omla), tok(ogqa), tok(gates)]
        + [_layer_spec(a, l, single_buffer=True) for a in weights]
        + [pl.BlockSpec(fg.shape, lambda b, i: (0, 0))],
        out_specs=tok(x),
        out_shape=jax.ShapeDtypeStruct(x.shape, F32),
        compiler_params=pltpu.CompilerParams(dimension_semantics=("parallel", "parallel"),
                                             vmem_limit_bytes=VMEM_LIMIT),
        name="merge_swiglu",
    )(x, mod, ona, omla, ogqa, gates, *weights, fg)


def _rope_tables(S):
    f32 = np.float32
    t = np.arange(S, dtype=np.int32)
    pos_t = t.astype(f32)
    pos_row = (t // GRID_W).astype(f32)
    pos_col = (t % GRID_W).astype(f32)
    half = 16
    lane = np.arange(LANES)

    def tables(pos_of_lane, roped, first_half, inv_freq):
        ang = (pos_of_lane * inv_freq[lane % half][None, :]).astype(f32)
        cos, sin = np.cos(ang), np.sin(ang)
        c = np.where(roped[None, :], cos, f32(1.0))
        up = np.where((roped & first_half)[None, :], -sin, f32(0.0))
        dn = np.where((roped & ~first_half)[None, :], sin, f32(0.0))
        return tuple(jnp.asarray(a, F32) for a in (c, up, dn))

    inv_t = (f32(1.0) / (f32(ROPE_THETA) ** (np.arange(half, dtype=f32) / f32(half)))).astype(f32)
    inv_a = (f32(1.0) / (f32(AXIAL_THETA) ** (np.arange(half, dtype=f32) / f32(half)))).astype(f32)
    m_roped = (lane >= MLA_NOPE) & (lane < MLA_NOPE + MLA_ROPE)
    m_first = ((lane - MLA_NOPE) % 32) < half
    mt = tables(np.broadcast_to(pos_t[:, None], (S, LANES)), m_roped, m_first, inv_t)
    a_first = (lane % 32) < half
    use_row = (lane % GRID_W) < 32
    pos_rc = np.where(use_row[None, :], pos_row[:, None], pos_col[:, None])
    at = tables(pos_rc, np.ones(LANES, bool), a_first, inv_a)
    return mt, at


def _na_bias_table(rpb):
    L, W = rpb.shape[0], GRID_W
    pad = W - NA_KW
    ext = jnp.concatenate([jnp.repeat(rpb[..., :1], pad, -1), rpb, jnp.repeat(rpb[..., -1:], pad + 1, -1)], -1)
    flat = jnp.tile(ext, (1, 1, 1, W))[..., :W * (2 * W - 1)]
    t = flat.reshape(ext.shape[:-1] + (W, 2 * W - 1))[..., W - 1:]
    c = np.arange(W)[:, None, None]
    kc = np.arange(W)[None, None, :]
    ws = np.clip(c - NA_KW // 2, 0, W - NA_KW)
    valid = (kc >= ws) & (kc < ws + NA_KW)
    out = []
    for d in range(NA_KH):
        td = jnp.transpose(t[:, :, NA_KH - 1 - d:2 * NA_KH - 1 - d], (0, 1, 3, 2, 4))
        out.append(jnp.where(valid, td.astype(F32), MASK_VALUE))
    return jnp.stack(out, axis=1).reshape(L, NA_KH, NA_HEADS * W, NA_KH * W)


W_IN_KR = 1152
W_IN_QC = 1184
W_IN_COLS = 5024


def _w_in_prep_kernel(w_ref, o_ref):
    tk = w_ref.shape[0]
    z = lambda n: jnp.zeros((tk, n), F32)
    o_ref[:, 0:W_IN_KR] = w_ref[:, 0:W_IN_KR].astype(BF16)
    o_ref[:, C_MLA + 384:C_GQA] = jnp.concatenate(
        [z(MLA_NOPE), w_ref[:, W_IN_KR:W_IN_QC], z(LANES - MLA_NOPE - MLA_ROPE)], axis=1).astype(BF16)
    rest = w_ref[:, W_IN_QC:W_IN_COLS]
    hd, pairs = GQA_HEAD_DIM, GQA_HEADS // GQA_KV_HEADS
    for j in range(pairs):
        o_ref[:, C_GQA + LANES * j:C_GQA + LANES * (j + 1)] = jnp.concatenate(
            [rest[:, hd * j:hd * (j + 1)], rest[:, hd * (pairs + j):hd * (pairs + j + 1)]], axis=1).astype(BF16)
    o_ref[:, C_GQA + 512:C_TOTAL] = rest[:, 512:].astype(BF16)


def _prep_w_in(w):
    L, D, N = w.shape
    tk = 256
    return pl.pallas_call(
        _w_in_prep_kernel,
        grid=(L, D // tk),
        in_specs=[pl.BlockSpec((None, tk, N), lambda l, i: (l, i, 0))],
        out_specs=pl.BlockSpec((None, tk, C_TOTAL), lambda l, i: (l, i, 0)),
        out_shape=jax.ShapeDtypeStruct((L, D, C_TOTAL), BF16),
        compiler_params=pltpu.CompilerParams(dimension_semantics=("parallel", "parallel"),
                                             vmem_limit_bytes=VMEM_LIMIT),
        name="w_in_prep",
    )(w)


def _prep_w_uq(w):
    z = jnp.zeros(w.shape[:-1] + (32,), BF16)
    hd = MLA_NOPE + MLA_ROPE
    return jnp.concatenate([a for h in range(MLA_HEADS) for a in (w[..., hd * h:hd * (h + 1)].astype(BF16), z)],
                           axis=-1)


def _prep_w_ukv(w):
    z = jnp.zeros(w.shape[:-1] + (64,), BF16)
    hd = MLA_NOPE + MLA_V
    ks = [a for h in range(MLA_HEADS) for a in (w[..., hd * h:hd * h + MLA_NOPE].astype(BF16), z)]
    vs = [w[..., hd * h + MLA_NOPE:hd * (h + 1)].astype(BF16) for h in range(MLA_HEADS)]
    return jnp.concatenate(ks + vs, axis=-1)


def _prep_w_br_gqa(w):
    rows = []
    for j in range(4):
        rows += [w[:, 64 * j:64 * (j + 1)], w[:, 64 * (4 + j):64 * (5 + j)]]
    return jnp.concatenate([r.astype(BF16) for r in rows], axis=1)


def kernel(x, c, ada_w, ada_b, norm_mix_g, norm_ffn_g, w_in, na_rpb, mla_q_norm_g, mla_kv_norm_g, mla_w_uq,
           mla_w_ukv, gqa_q_norm_g, gqa_k_norm_g, w_br_na, w_br_mla, w_br_gqa, w_out, ffn_w_gate_up,
           ffn_w_down, final_norm_g):
    B, S, D = x.shape
    L = ada_w.shape[0]
    mod = _ada_call(c, ada_w, ada_b).reshape(L, B, N_MOD, D)
    mtabs, atabs = _rope_tables(S)
    lane = np.arange(LANES)
    bd = jnp.asarray(np.tile((lane[:, None] // 64 == lane[None, :] // 64) / 64.0, (2, 1)), BF16)
    row = lambda a: a.reshape(L, 1, -1)
    gqk = jnp.concatenate([jnp.tile(gqa_q_norm_g, (1, GQA_HEADS)), jnp.tile(gqa_k_norm_g, (1, GQA_KV_HEADS))], 1)
    w_in_p, w_uq_p, w_ukv_p = _prep_w_in(w_in), _prep_w_uq(mla_w_uq), _prep_w_ukv(mla_w_ukv)
    bias = _na_bias_table(na_rpb)
    wa, wb, wc, wo = w_br_na.astype(BF16), w_br_mla.astype(BF16), _prep_w_br_gqa(w_br_gqa), w_out.astype(BF16)
    wgu, wd = ffn_w_gate_up.astype(BF16), ffn_w_down.astype(BF16)
    fg = final_norm_g.reshape(1, D)
    for l in range(L):
        qna, kna, vna, qm, km, vm, qg, kg, vg, gates = _in_call(
            x, mod, l, row(norm_mix_g), w_in_p, row(mla_q_norm_g), row(mla_kv_norm_g), w_uq_p, w_ukv_p,
            row(gqk), bd, mtabs, atabs)
        o_na = _na_call(qna, kna, vna, bias, l)
        o_mla = _attn_call(_mla_attn_kernel, qm, km, vm, MLA_HEADS * MLA_V, MLA_Q_TILE,
                           [pltpu.VMEM((S, MLA_HEADS * LANES), BF16),
                            pltpu.VMEM((MLA_HEADS, VT_ROWS, S), BF16)], "mla_attn")
        o_gqa = _attn_call(_gqa_attn_kernel, qg, kg, vg, GQA_HEADS * GQA_HEAD_DIM, GQA_Q_TILE,
                           [pltpu.VMEM((S, 2 * LANES), BF16), pltpu.VMEM((GQA_KV_HEADS, VT_ROWS, S), BF16)],
                           "gqa_attn")
        x = _post_call(x, mod, l, o_na, o_mla, o_gqa, gates, wa, wb, wc, wo, row(norm_ffn_g), wgu, wd, fg,
                       final_norm=(l == L - 1))
    return x
```

```python
import functools

import numpy as np
import jax
import jax.numpy as jnp
from jax import lax
from jax.experimental import pallas as pl
from jax.experimental.pallas import tpu as pltpu

D_MODEL = 1024
GRID_W = 64
NA_HEADS = 4
NA_HEAD_DIM = 64
NA_KH = 8
NA_KW = 16
MLA_HEADS = 4
MLA_Q_LORA = 256
MLA_KV_LORA = 128
MLA_NOPE = 64
MLA_ROPE = 32
MLA_V = 64
GQA_HEADS = 8
GQA_KV_HEADS = 2
GQA_HEAD_DIM = 64
FFN_HIDDEN = 2816
ROPE_THETA = 10000.0
AXIAL_THETA = 10000.0
NORM_EPS = 1e-6
N_MOD = 6
MASK_VALUE = -1e30
LOG2E = 1.4426950408889634

LANES = 128
VMEM_LIMIT = 56 * 1024 * 1024
TOKEN_TILE = 512
MXU_TILE = 256
MLA_Q_TILE = 512
GQA_Q_TILE = 256
_FFN_SPLIT = (FFN_HIDDEN // MXU_TILE + 1) // 2 * MXU_TILE
FFN_CHUNKS = ((0, _FFN_SPLIT), (_FFN_SPLIT, FFN_HIDDEN))

C_NA = 0
C_MLA = 768
C_GQA = 1280
C_GATE = 2048
C_TOTAL = 5120

F32 = jnp.float32
BF16 = jnp.bfloat16


def _dot(a, b):
    return jnp.dot(a, b, preferred_element_type=F32)


def _dot_nt(a, b):
    return lax.dot_general(a, b, (((1,), (1,)), ((), ())), preferred_element_type=F32)


def _rms(x):
    return x * lax.rsqrt(jnp.mean(x * x, axis=-1, keepdims=True) + NORM_EPS)


def _rope_slab(x, c, s_up, s_dn):
    return x * c + pltpu.roll(x, LANES - 16, 1) * s_up + pltpu.roll(x, 16, 1) * s_dn


def _sigmoid(x):
    return 1.0 / (1.0 + jnp.exp(-x))


def _layer_spec(a, l, single_buffer=False):
    mode = pl.Buffered(1) if single_buffer else None
    return pl.BlockSpec((None,) + a.shape[1:], lambda *_: (l,) + (0,) * (a.ndim - 1), pipeline_mode=mode)


def _mod_spec(mod, l):
    return pl.BlockSpec((None, None) + mod.shape[2:], lambda b, *_: (l, b, 0, 0))


def _ada_kernel(c_ref, w_ref, b_ref, o_ref):
    c = c_ref[...]
    s = (c * _sigmoid(c)).astype(BF16)
    o_ref[...] = _dot(s, w_ref[...].astype(BF16)) + b_ref[...]


def _ada_call(c, ada_w, ada_b):
    L, D, N = ada_w.shape
    B = c.shape[0]
    tn = 1536
    return pl.pallas_call(
        _ada_kernel,
        grid=(L, N // tn),
        in_specs=[pl.BlockSpec((B, D), lambda l, j: (0, 0)),
                  pl.BlockSpec((None, D, tn), lambda l, j: (l, 0, j)),
                  pl.BlockSpec((None, 1, tn), lambda l, j: (l, 0, j))],
        out_specs=pl.BlockSpec((None, B, tn), lambda l, j: (l, 0, j)),
        out_shape=jax.ShapeDtypeStruct((L, B, N), F32),
        compiler_params=pltpu.CompilerParams(dimension_semantics=("parallel", "parallel"),
                                             vmem_limit_bytes=VMEM_LIMIT),
        name="ada_mod",
    )(c, ada_w, ada_b.reshape(L, 1, N))


def _in_kernel(x_ref, mod_ref, g_ref, w_ref, qng_ref, kvng_ref, wuq_ref, wukv_ref, gqk_ref, bd_ref,
               mc_ref, mu_ref, md_ref, ac_ref, au_ref, ad_ref,
               qna_ref, kna_ref, vna_ref, qm_ref, km_ref, vm_ref, qg_ref, kg_ref, vg_ref, gate_ref):
    x = x_ref[...]
    h = _rms(x) * g_ref[...] * (1.0 + mod_ref[1:2, :]) + mod_ref[0:1, :]
    hb = h.astype(BF16)

    def proj(a, b):
        return _dot(hb, w_ref[:, a:b])

    p = proj(C_NA, C_NA + 768)
    qna_ref[...] = (p[:, :256] * NA_HEAD_DIM ** -0.5).astype(BF16)
    kna_ref[...] = p[:, 256:512].astype(BF16)
    vna_ref[...] = p[:, 512:768].astype(BF16)

    p = proj(C_MLA, C_MLA + 512)
    mc, mu, md = mc_ref[...], mu_ref[...], md_ref[...]
    nq = (_rms(p[:, :256]) * qng_ref[...]).astype(BF16)
    qb = _dot(nq, wuq_ref[...])
    mla_scale = (MLA_NOPE + MLA_ROPE) ** -0.5 * LOG2E
    for s in range(MLA_HEADS):
        sl = slice(s * LANES, (s + 1) * LANES)
        qm_ref[:, sl] = (_rope_slab(qb[:, sl], mc, mu, md) * mla_scale).astype(BF16)
    nkv = (_rms(p[:, 256:384]) * kvng_ref[...]).astype(BF16)
    kvb = _dot(nkv, wukv_ref[...])
    kpe = _rope_slab(p[:, 384:512], mc, mu, md)
    for s in range(MLA_HEADS):
        sl = slice(s * LANES, (s + 1) * LANES)
        km_ref[:, sl] = (kvb[:, sl] + kpe).astype(BF16)
    vm_ref[...] = kvb[:, 512:768].astype(BF16)

    p = proj(C_GQA, C_GQA + 768)
    ac, au, ad = ac_ref[...], au_ref[...], ad_ref[...]
    bd = bd_ref[...]
    for s in range(5):
        sl = slice(s * LANES, (s + 1) * LANES)
        v = p[:, sl]
        sq = v * v
        hi = sq.astype(BF16)
        lo = (sq - hi.astype(F32)).astype(BF16)
        msq = _dot(jnp.concatenate([hi, lo], axis=1), bd)
        y = v * lax.rsqrt(msq + NORM_EPS) * gqk_ref[:, sl]
        r = _rope_slab(y, ac, au, ad)
        if s < 4:
            qg_ref[:, sl] = (r * (GQA_HEAD_DIM ** -0.5 * LOG2E)).astype(BF16)
        else:
            kg_ref[...] = r.astype(BF16)
    vg_ref[...] = p[:, 640:768].astype(BF16)

    for j in range(3):
        g = proj(C_GATE + j * D_MODEL, C_GATE + (j + 1) * D_MODEL)
        gate_ref[:, j * D_MODEL:(j + 1) * D_MODEL] = _sigmoid(g).astype(BF16)


def _in_call(x, mod, l, g, w, qng, kvng, wuq, wukv, gqk, bd, mtabs, atabs):
    B, S, D = x.shape
    tm = TOKEN_TILE
    tok = lambda w_: pl.BlockSpec((None, tm, w_), lambda b, i: (b, i, 0))
    full = lambda a: pl.BlockSpec(a.shape, lambda b, i: (0,) * a.ndim)
    tab = pl.BlockSpec((tm, LANES), lambda b, i: (i, 0))
    widths = (256, 256, 256, 512, 512, 256, 512, 128, 128, 3 * D)
    return pl.pallas_call(
        _in_kernel,
        grid=(B, S // tm),
        in_specs=[tok(D), _mod_spec(mod, l)] + [_layer_spec(a, l) for a in (g, w, qng, kvng, wuq, wukv, gqk)]
        + [full(bd)] + [tab] * 6,
        out_specs=[tok(w_) for w_ in widths],
        out_shape=[jax.ShapeDtypeStruct((B, S, w_), BF16) for w_ in widths],
        compiler_params=pltpu.CompilerParams(dimension_semantics=("parallel", "parallel"),
                                             vmem_limit_bytes=VMEM_LIMIT),
        name="in_proj",
    )(x, mod, g, w, qng, kvng, wuq, wukv, gqk, bd, *mtabs, *atabs)


def _na_kernel(q_ref, k_ref, v_ref, bias_ref, o_ref, *, rows):
    kh = NA_KH
    head_of_lane = lax.broadcasted_iota(jnp.int32, (GRID_W, NA_HEADS * NA_HEAD_DIM), 1) // NA_HEAD_DIM

    def body(r, carry):
        rs = jnp.clip(r - kh // 2, 0, rows - kh)
        q = q_ref[pl.ds(pl.multiple_of(r * GRID_W, GRID_W), GRID_W), :]
        kstart = pl.multiple_of(rs * GRID_W, GRID_W)
        kb = k_ref[pl.ds(kstart, kh * GRID_W), :]
        vb = v_ref[pl.ds(kstart, kh * GRID_W), :]
        qs = jnp.concatenate([jnp.where(head_of_lane == h, q, jnp.zeros_like(q)) for h in range(NA_HEADS)],
                             axis=0)
        s = _dot_nt(qs, kb) + bias_ref[r - rs]
        m = jnp.max(s, axis=-1, keepdims=True)
        p = jnp.exp(s - m)
        l = jnp.sum(p, axis=-1, keepdims=True)
        o = _dot(p.astype(BF16), vb) / l
        out = jnp.zeros((GRID_W, NA_HEADS * NA_HEAD_DIM), F32)
        for h in range(NA_HEADS):
            out = jnp.where(head_of_lane == h, o[h * GRID_W:(h + 1) * GRID_W, :], out)
        o_ref[pl.ds(pl.multiple_of(r * GRID_W, GRID_W), GRID_W), :] = out.astype(BF16)
        return carry

    lax.fori_loop(0, rows, body, 0, unroll=32)


def _na_call(q, k, v, bias, l):
    B, S, W = q.shape
    rows = S // GRID_W
    blk = pl.BlockSpec((None, S, W), lambda b: (b, 0, 0))
    return pl.pallas_call(
        functools.partial(_na_kernel, rows=rows),
        grid=(B,),
        in_specs=[blk, blk, blk, _layer_spec(bias, l)],
        out_specs=blk,
        out_shape=jax.ShapeDtypeStruct((B, S, W), BF16),
        compiler_params=pltpu.CompilerParams(dimension_semantics=("parallel",),
                                             vmem_limit_bytes=VMEM_LIMIT),
        name="na_attn",
    )(q, k, v, bias)


L_MIN = 2.0 ** -60
MLA_BIAS_LANE = MLA_NOPE + MLA_ROPE


def _softmax_pv(q, k, v):
    s = _dot_nt(q, k)
    m = jnp.max(s, axis=-1, keepdims=True)
    p = jnp.exp2(s - m)
    l = jnp.sum(p, axis=-1, keepdims=True)
    return _dot(p.astype(BF16), v) / l


VT_ROWS = 80


def _store_vt(vt_ref, idx, vt_rows):
    hd = VT_ROWS - 16
    S = vt_rows.shape[1]
    vt_ref[idx, 0:hd, :] = vt_rows.astype(BF16)
    first = lax.broadcasted_iota(jnp.int32, (16, S), 0) == 0
    vt_ref[idx, hd:VT_ROWS, :] = jnp.where(first, 1.0, 0.0).astype(BF16)


def _row_norm(x):
    xf = x.astype(F32)
    return jnp.sqrt(jnp.sum(xf * xf, axis=-1, keepdims=True))


def _max_row_norm(x, mask=None):
    xf = x.astype(F32)
    sq = xf * xf if mask is None else jnp.where(mask, xf * xf, 0.0)
    n = jnp.max(jnp.sum(sq, axis=-1, keepdims=True), axis=0, keepdims=True)
    return jnp.broadcast_to(jnp.sqrt(n), (1, LANES))


def _mla_exact(q_ref, k_ref, v_ref, o_ref):
    v = v_ref[...]
    head_of_lane = lax.broadcasted_iota(jnp.int32, o_ref.shape, 1) // MLA_V
    out = jnp.zeros(o_ref.shape, F32)
    for h in range(MLA_HEADS):
        sl = slice(h * LANES, (h + 1) * LANES)
        out = jnp.where(head_of_lane == h, _softmax_pv(q_ref[:, sl], k_ref[:, sl], v), out)
    o_ref[...] = out.astype(BF16)


def _pipelined_attention(init, scores, values, exact, pt_refs, lmin_ref):
    i = pl.program_id(1)
    n_tiles = pl.num_programs(1) - 1

    @pl.when(i == 0)
    def _():
        init()
        lmin_ref[0] = 1.0
        scores(pt_refs[0])

    for parity in (0, 1):
        @pl.when((i > 0) & (i < n_tiles) & (i % 2 == parity))
        def _():
            lmin_ref[0] = values(pt_refs[1 - parity])
            scores(pt_refs[parity])

    for parity in (0, 1):
        @pl.when((i == n_tiles) & (i % 2 == parity))
        def _():
            lmin_ref[0] = values(pt_refs[1 - parity])

    @pl.when((i > 0) & (lmin_ref[0] < L_MIN))
    def _():
        exact()


def _mla_attn_kernel(q_ref, qprev_ref, k_ref, v_ref, o_ref, kaug_ref, vt_ref, kmax_ref, pta_ref, ptb_ref, lmin_ref):
    S, tq = k_ref.shape[0], q_ref.shape[0]

    def init():
        lane_s = lax.broadcasted_iota(jnp.int32, (S, LANES), 1)
        for h in range(MLA_HEADS):
            sl = slice(h * LANES, (h + 1) * LANES)
            k = k_ref[:, sl]
            kaug_ref[:, sl] = jnp.where(lane_s == MLA_BIAS_LANE, jnp.ones_like(k), k)
            kmax_ref[h:h + 1, :] = _max_row_norm(k)
        for p in range(MLA_HEADS // 2):
            vt = jnp.transpose(v_ref[:, p * LANES:(p + 1) * LANES].astype(F32))
            _store_vt(vt_ref, 2 * p, vt[0:MLA_V])
            _store_vt(vt_ref, 2 * p + 1, vt[MLA_V:2 * MLA_V])

    def scores(pt_ref):
        lane = lax.broadcasted_iota(jnp.int32, (tq, LANES), 1)
        for h in range(MLA_HEADS):
            sl = slice(h * LANES, (h + 1) * LANES)
            q = q_ref[:, sl]
            b = _row_norm(q) * kmax_ref[h:h + 1, 0:1]
            q_aug = jnp.where(lane == MLA_BIAS_LANE, (-b).astype(BF16), q)
            pt_ref[:, h * tq:(h + 1) * tq] = jnp.exp2(_dot_nt(kaug_ref[:, sl], q_aug)).astype(BF16)

    def values(pt_ref):
        lmin = None
        for p in range(MLA_HEADS // 2):
            outs = []
            for h in (2 * p, 2 * p + 1):
                ot = _dot(vt_ref[h], pt_ref[:, h * tq:(h + 1) * tq])
                l = ot[MLA_V:MLA_V + 1, :]
                outs.append(ot[0:MLA_V, :] / l)
                lmin = l if lmin is None else jnp.minimum(lmin, l)
            o_ref[:, p * LANES:(p + 1) * LANES] = jnp.transpose(jnp.concatenate(outs, axis=0)).astype(BF16)
        return jnp.min(lmin)

    _pipelined_attention(init, scores, values, lambda: _mla_exact(qprev_ref, k_ref, v_ref, o_ref),
                         (pta_ref, ptb_ref), lmin_ref)


def _gqa_exact(q_ref, k_ref, v_ref, o_ref):
    k = k_ref[...]
    v = v_ref[...]
    low = lax.broadcasted_iota(jnp.int32, (q_ref.shape[0], LANES), 1) < GQA_HEAD_DIM
    for j in range(GQA_HEADS // GQA_KV_HEADS):
        sl = slice(j * LANES, (j + 1) * LANES)
        q = q_ref[:, sl]
        zero = jnp.zeros_like(q)
        o_lo = _softmax_pv(jnp.where(low, q, zero), k, v)
        o_hi = _softmax_pv(jnp.where(low, zero, q), k, v)
        o_ref[:, sl] = jnp.where(low, o_lo, o_hi).astype(BF16)


def _gqa_attn_kernel(q_ref, qprev_ref, k_ref, v_ref, o_ref, kaug_ref, vt_ref, kmax_ref, pta_ref, ptb_ref, lmin_ref):
    S, tq = k_ref.shape[0], q_ref.shape[0]
    pairs = GQA_HEADS // GQA_KV_HEADS
    n = pairs * tq

    def init():
        lane_s = lax.broadcasted_iota(jnp.int32, (S, LANES), 1)
        k = k_ref[...]
        kaug_ref[:, :LANES] = k
        kaug_ref[:, LANES:] = jnp.where(lane_s == 0, 1.0, 0.0).astype(BF16)
        vt = jnp.transpose(v_ref[...].astype(F32))
        for g in range(GQA_KV_HEADS):
            _store_vt(vt_ref, g, vt[g * GQA_HEAD_DIM:(g + 1) * GQA_HEAD_DIM])
        kmax_ref[0:1, :] = _max_row_norm(k, lane_s < GQA_HEAD_DIM)
        kmax_ref[1:2, :] = _max_row_norm(k, lane_s >= GQA_HEAD_DIM)

    def scores(pt_ref):
        lane = lax.broadcasted_iota(jnp.int32, (tq, LANES), 1)
        low = lane < GQA_HEAD_DIM
        q_aug = [[], []]
        for j in range(pairs):
            q = q_ref[:, j * LANES:(j + 1) * LANES]
            zero = jnp.zeros_like(q)
            for half in range(GQA_KV_HEADS):
                qm = jnp.where(low, q, zero) if half == 0 else jnp.where(low, zero, q)
                b = _row_norm(qm) * kmax_ref[half:half + 1, 0:1]
                bias = jnp.where(lane == 0, -b, 0.0).astype(BF16)
                q_aug[half].append(jnp.concatenate([qm, bias], axis=1))
        pt_ref[...] = jnp.exp2(_dot_nt(kaug_ref[...], jnp.concatenate(q_aug[0] + q_aug[1], axis=0))).astype(BF16)

    def values(pt_ref):
        outs, lmin = [], None
        for half in range(GQA_KV_HEADS):
            ot = _dot(vt_ref[half], pt_ref[:, half * n:(half + 1) * n])
            l = ot[GQA_HEAD_DIM:GQA_HEAD_DIM + 1, :]
            outs.append(ot[0:GQA_HEAD_DIM, :] / l)
            lmin = l if lmin is None else jnp.minimum(lmin, l)
        for j in range(pairs):
            ot = jnp.concatenate([o[:, j * tq:(j + 1) * tq] for o in outs], axis=0)
            o_ref[:, j * LANES:(j + 1) * LANES] = jnp.transpose(ot).astype(BF16)
        return jnp.min(lmin)

    _pipelined_attention(init, scores, values, lambda: _gqa_exact(qprev_ref, k_ref, v_ref, o_ref),
                         (pta_ref, ptb_ref), lmin_ref)


def _attn_call(kernel, q, k, v, out_width, tq, heads, scratch, name):
    B, S, _ = q.shape
    n_tiles = S // tq
    cur = lambda b, i: (b, jnp.minimum(i, n_tiles - 1), 0)
    prev = lambda b, i: (b, jnp.maximum(i - 1, 0), 0)
    pt = pltpu.VMEM((S, heads * tq), BF16)
    return pl.pallas_call(
        kernel,
        grid=(B, n_tiles + 1),
        in_specs=[pl.BlockSpec((None, tq, q.shape[2]), cur),
                  pl.BlockSpec((None, tq, q.shape[2]), prev),
                  pl.BlockSpec((None, S, k.shape[2]), lambda b, i: (b, 0, 0)),
                  pl.BlockSpec((None, S, v.shape[2]), lambda b, i: (b, 0, 0))],
        out_specs=pl.BlockSpec((None, tq, out_width), prev),
        out_shape=jax.ShapeDtypeStruct((B, S, out_width), BF16),
        scratch_shapes=scratch + [pltpu.VMEM((8, LANES), F32), pt, pt, pltpu.SMEM((1,), F32)],
        compiler_params=pltpu.CompilerParams(dimension_semantics=("parallel", "arbitrary"),
                                             vmem_limit_bytes=VMEM_LIMIT),
        name=name,
    )(q, q, k, v)


def _post_kernel(x_ref, mod_ref, ona_ref, omla_ref, ogqa_ref, gate_ref, wa_ref, wb_ref, wc_ref, wo_ref,
                 g_ref, wgu_ref, wd_ref, fg_ref, o_ref, *, final_norm):
    D = D_MODEL
    m = (gate_ref[:, 0:D].astype(F32) * _dot(ona_ref[...], wa_ref[...])
         + gate_ref[:, D:2 * D].astype(F32) * _dot(omla_ref[...], wb_ref[...])
         + gate_ref[:, 2 * D:3 * D].astype(F32) * _dot(ogqa_ref[...], wc_ref[...]))
    x = x_ref[...] + mod_ref[2:3, :] * _dot(m.astype(BF16), wo_ref[...])

    h = _rms(x) * g_ref[...] * (1.0 + mod_ref[4:5, :]) + mod_ref[3:4, :]
    hb = h.astype(BF16)
    acc = jnp.zeros(x.shape, F32)
    for c0, c1 in FFN_CHUNKS:
        gate = _dot(hb, wgu_ref[:, c0:c1])
        up = _dot(hb, wgu_ref[:, FFN_HIDDEN + c0:FFN_HIDDEN + c1])
        act = (gate * _sigmoid(gate) * up).astype(BF16)
        acc = acc + _dot(act, wd_ref[c0:c1, :])
    y = x + mod_ref[5:6, :] * acc
    if final_norm:
        y = _rms(y) * fg_ref[...]
    o_ref[...] = y


def _post_call(x, mod, l, ona, omla, ogqa, gates, wa, wb, wc, wo, g, wgu, wd, fg, final_norm):
    B, S, D = x.shape
    tm = TOKEN_TILE
    tok = lambda a: pl.BlockSpec((None, tm, a.shape[2]), lambda b, i: (b, i, 0))
    weights = (wa, wb, wc, wo, g, wgu, wd)
    return pl.pallas_call(
        functools.partial(_post_kernel, final_norm=final_norm),
        grid=(B, S // tm),
        in_specs=[tok(x), _mod_spec(mod, l), tok(ona), tok(omla), tok(ogqa), tok(gates)]
        + [_layer_spec(a, l, single_buffer=True) for a in weights]
        + [pl.BlockSpec(fg.shape, lambda b, i: (0, 0))],
        out_specs=tok(x),
        out_shape=jax.ShapeDtypeStruct(x.shape, F32),
        compiler_params=pltpu.CompilerParams(dimension_semantics=("parallel", "parallel"),
                                             vmem_limit_bytes=VMEM_LIMIT),
        name="merge_swiglu",
    )(x, mod, ona, omla, ogqa, gates, *weights, fg)


def _rope_tables(S):
    f32 = np.float32
    t = np.arange(S, dtype=np.int32)
    pos_t = t.astype(f32)
    pos_row = (t // GRID_W).astype(f32)
    pos_col = (t % GRID_W).astype(f32)
    half = 16
    lane = np.arange(LANES)

    def tables(pos_of_lane, roped, first_half, inv_freq):
        ang = (pos_of_lane * inv_freq[lane % half][None, :]).astype(f32)
        cos, sin = np.cos(ang), np.sin(ang)
        c = np.where(roped[None, :], cos, f32(1.0))
        up = np.where((roped & first_half)[None, :], -sin, f32(0.0))
        dn = np.where((roped & ~first_half)[None, :], sin, f32(0.0))
        return tuple(jnp.asarray(a, F32) for a in (c, up, dn))

    inv_t = (f32(1.0) / (f32(ROPE_THETA) ** (np.arange(half, dtype=f32) / f32(half)))).astype(f32)
    inv_a = (f32(1.0) / (f32(AXIAL_THETA) ** (np.arange(half, dtype=f32) / f32(half)))).astype(f32)
    m_roped = (lane >= MLA_NOPE) & (lane < MLA_NOPE + MLA_ROPE)
    m_first = ((lane - MLA_NOPE) % 32) < half
    mt = tables(np.broadcast_to(pos_t[:, None], (S, LANES)), m_roped, m_first, inv_t)
    a_first = (lane % 32) < half
    use_row = (lane % GRID_W) < 32
    pos_rc = np.where(use_row[None, :], pos_row[:, None], pos_col[:, None])
    at = tables(pos_rc, np.ones(LANES, bool), a_first, inv_a)
    return mt, at


def _na_bias_table(rpb):
    L, W = rpb.shape[0], GRID_W
    pad = W - NA_KW
    ext = jnp.concatenate([jnp.repeat(rpb[..., :1], pad, -1), rpb, jnp.repeat(rpb[..., -1:], pad, -1)], -1)
    t = jnp.stack([ext[..., W - 1 - c:2 * W - 1 - c] for c in range(W)], axis=-2)
    c = np.arange(W)[:, None, None]
    kc = np.arange(W)[None, None, :]
    ws = np.clip(c - NA_KW // 2, 0, W - NA_KW)
    valid = (kc >= ws) & (kc < ws + NA_KW)
    out = []
    for d in range(NA_KH):
        td = jnp.transpose(t[:, :, NA_KH - 1 - d:2 * NA_KH - 1 - d], (0, 1, 3, 2, 4))
        out.append(jnp.where(valid, td.astype(F32), MASK_VALUE))
    return jnp.stack(out, axis=1).reshape(L, NA_KH, NA_HEADS * W, NA_KH * W)


W_IN_KR = 1152
W_IN_QC = 1184
W_IN_COLS = 5024


def _w_in_prep_kernel(w_ref, o_ref):
    tk = w_ref.shape[0]
    z = lambda n: jnp.zeros((tk, n), F32)
    o_ref[:, 0:W_IN_KR] = w_ref[:, 0:W_IN_KR].astype(BF16)
    o_ref[:, C_MLA + 384:C_GQA] = jnp.concatenate(
        [z(MLA_NOPE), w_ref[:, W_IN_KR:W_IN_QC], z(LANES - MLA_NOPE - MLA_ROPE)], axis=1).astype(BF16)
    rest = w_ref[:, W_IN_QC:W_IN_COLS]
    hd, pairs = GQA_HEAD_DIM, GQA_HEADS // GQA_KV_HEADS
    for j in range(pairs):
        o_ref[:, C_GQA + LANES * j:C_GQA + LANES * (j + 1)] = jnp.concatenate(
            [rest[:, hd * j:hd * (j + 1)], rest[:, hd * (pairs + j):hd * (pairs + j + 1)]], axis=1).astype(BF16)
    o_ref[:, C_GQA + 512:C_TOTAL] = rest[:, 512:].astype(BF16)


def _prep_w_in(w):
    L, D, N = w.shape
    tk = 256
    return pl.pallas_call(
        _w_in_prep_kernel,
        grid=(L, D // tk),
        in_specs=[pl.BlockSpec((None, tk, N), lambda l, i: (l, i, 0))],
        out_specs=pl.BlockSpec((None, tk, C_TOTAL), lambda l, i: (l, i, 0)),
        out_shape=jax.ShapeDtypeStruct((L, D, C_TOTAL), BF16),
        compiler_params=pltpu.CompilerParams(dimension_semantics=("parallel", "parallel"),
                                             vmem_limit_bytes=VMEM_LIMIT),
        name="w_in_prep",
    )(w)


def _prep_w_uq(w):
    z = jnp.zeros(w.shape[:-1] + (32,), BF16)
    hd = MLA_NOPE + MLA_ROPE
    return jnp.concatenate([a for h in range(MLA_HEADS) for a in (w[..., hd * h:hd * (h + 1)].astype(BF16), z)],
                           axis=-1)


def _prep_w_ukv(w):
    z = jnp.zeros(w.shape[:-1] + (64,), BF16)
    hd = MLA_NOPE + MLA_V
    ks = [a for h in range(MLA_HEADS) for a in (w[..., hd * h:hd * h + MLA_NOPE].astype(BF16), z)]
    vs = [w[..., hd * h + MLA_NOPE:hd * (h + 1)].astype(BF16) for h in range(MLA_HEADS)]
    return jnp.concatenate(ks + vs, axis=-1)


def _prep_w_br_gqa(w):
    rows = []
    for j in range(4):
        rows += [w[:, 64 * j:64 * (j + 1)], w[:, 64 * (4 + j):64 * (5 + j)]]
    return jnp.concatenate([r.astype(BF16) for r in rows], axis=1)


def kernel(x, c, ada_w, ada_b, norm_mix_g, norm_ffn_g, w_in, na_rpb, mla_q_norm_g, mla_kv_norm_g, mla_w_uq,
           mla_w_ukv, gqa_q_norm_g, gqa_k_norm_g, w_br_na, w_br_mla, w_br_gqa, w_out, ffn_w_gate_up,
           ffn_w_down, final_norm_g):
    B, S, D = x.shape
    L = ada_w.shape[0]
    mod = _ada_call(c, ada_w, ada_b).reshape(L, B, N_MOD, D)
    mtabs, atabs = _rope_tables(S)
    lane = np.arange(LANES)
    bd = jnp.asarray(np.tile((lane[:, None] // 64 == lane[None, :] // 64) / 64.0, (2, 1)), BF16)
    row = lambda a: a.reshape(L, 1, -1)
    gqk = jnp.concatenate([jnp.tile(gqa_q_norm_g, (1, GQA_HEADS)), jnp.tile(gqa_k_norm_g, (1, GQA_KV_HEADS))], 1)
    w_in_p, w_uq_p, w_ukv_p = _prep_w_in(w_in), _prep_w_uq(mla_w_uq), _prep_w_ukv(mla_w_ukv)
    bias = _na_bias_table(na_rpb)
    wa, wb, wc, wo = w_br_na.astype(BF16), w_br_mla.astype(BF16), _prep_w_br_gqa(w_br_gqa), w_out.astype(BF16)
    wgu, wd = ffn_w_gate_up.astype(BF16), ffn_w_down.astype(BF16)
    fg = final_norm_g.reshape(1, D)
    for l in range(L):
        qna, kna, vna, qm, km, vm, qg, kg, vg, gates = _in_call(
            x, mod, l, row(norm_mix_g), w_in_p, row(mla_q_norm_g), row(mla_kv_norm_g), w_uq_p, w_ukv_p,
            row(gqk), bd, mtabs, atabs)
        o_na = _na_call(qna, kna, vna, bias, l)
        o_mla = _attn_call(_mla_attn_kernel, qm, km, vm, MLA_HEADS * MLA_V, MLA_Q_TILE, MLA_HEADS,
                           [pltpu.VMEM((S, MLA_HEADS * LANES), BF16),
                            pltpu.VMEM((MLA_HEADS, VT_ROWS, S), BF16)], "mla_attn")
        o_gqa = _attn_call(_gqa_attn_kernel, qg, kg, vg, GQA_HEADS * GQA_HEAD_DIM, GQA_Q_TILE, GQA_HEADS,
                           [pltpu.VMEM((S, 2 * LANES), BF16), pltpu.VMEM((GQA_KV_HEADS, VT_ROWS, S), BF16)],
                           "gqa_attn")
        x = _post_call(x, mod, l, o_na, o_mla, o_gqa, gates, wa, wb, wc, wo, row(norm_ffn_g), wgu, wd, fg,
                       final_norm=(l == L - 1))
    return x
```

```python
import functools

import numpy as np
import jax
import jax.numpy as jnp
from jax import lax
from jax.experimental import pallas as pl
from jax.experimental.pallas import tpu as pltpu

D_MODEL = 1024
GRID_W = 64
NA_HEADS = 4
NA_HEAD_DIM = 64
NA_KH = 8
NA_KW = 16
MLA_HEADS = 4
MLA_Q_LORA = 256
MLA_KV_LORA = 128
MLA_NOPE = 64
MLA_ROPE = 32
MLA_V = 64
GQA_HEADS = 8
GQA_KV_HEADS = 2
GQA_HEAD_DIM = 64
FFN_HIDDEN = 2816
ROPE_THETA = 10000.0
AXIAL_THETA = 10000.0
NORM_EPS = 1e-6
N_MOD = 6
MASK_VALUE = -1e30
LOG2E = 1.4426950408889634

LANES = 128
VMEM_LIMIT = 56 * 1024 * 1024
TOKEN_TILE = 512
MXU_TILE = 256
MLA_Q_TILE = 512
GQA_Q_TILE = 256
_FFN_SPLIT = (FFN_HIDDEN // MXU_TILE + 1) // 2 * MXU_TILE
FFN_CHUNKS = ((0, _FFN_SPLIT), (_FFN_SPLIT, FFN_HIDDEN))

C_NA = 0
C_MLA = 768
C_GQA = 1280
C_GATE = 2048
C_TOTAL = 5120

F32 = jnp.float32
BF16 = jnp.bfloat16


def _dot(a, b):
    return jnp.dot(a, b, preferred_element_type=F32)


def _dot_nt(a, b):
    return lax.dot_general(a, b, (((1,), (1,)), ((), ())), preferred_element_type=F32)


def _rms(x):
    return x * lax.rsqrt(jnp.mean(x * x, axis=-1, keepdims=True) + NORM_EPS)


def _rope_slab(x, c, s_up, s_dn):
    return x * c + pltpu.roll(x, LANES - 16, 1) * s_up + pltpu.roll(x, 16, 1) * s_dn


def _sigmoid(x):
    return 1.0 / (1.0 + jnp.exp(-x))


def _layer_spec(a, l, single_buffer=False):
    mode = pl.Buffered(1) if single_buffer else None
    return pl.BlockSpec((None,) + a.shape[1:], lambda *_: (l,) + (0,) * (a.ndim - 1), pipeline_mode=mode)


def _mod_spec(mod, l):
    return pl.BlockSpec((None, None) + mod.shape[2:], lambda b, *_: (l, b, 0, 0))


def _ada_kernel(c_ref, w_ref, b_ref, o_ref):
    c = c_ref[...]
    s = (c * _sigmoid(c)).astype(BF16)
    o_ref[...] = _dot(s, w_ref[...].astype(BF16)) + b_ref[...]


def _ada_call(c, ada_w, ada_b):
    L, D, N = ada_w.shape
    B = c.shape[0]
    tn = 1536
    return pl.pallas_call(
        _ada_kernel,
        grid=(L, N // tn),
        in_specs=[pl.BlockSpec((B, D), lambda l, j: (0, 0)),
                  pl.BlockSpec((None, D, tn), lambda l, j: (l, 0, j)),
                  pl.BlockSpec((None, 1, tn), lambda l, j: (l, 0, j))],
        out_specs=pl.BlockSpec((None, B, tn), lambda l, j: (l, 0, j)),
        out_shape=jax.ShapeDtypeStruct((L, B, N), F32),
        compiler_params=pltpu.CompilerParams(dimension_semantics=("parallel", "parallel"),
                                             vmem_limit_bytes=VMEM_LIMIT),
        name="ada_mod",
    )(c, ada_w, ada_b.reshape(L, 1, N))


def _in_kernel(x_ref, mod_ref, g_ref, w_ref, qng_ref, kvng_ref, wuq_ref, wukv_ref, gqk_ref, bd_ref,
               mc_ref, mu_ref, md_ref, ac_ref, au_ref, ad_ref,
               qna_ref, kna_ref, vna_ref, qm_ref, km_ref, vm_ref, qg_ref, kg_ref, vg_ref, gate_ref):
    x = x_ref[...]
    h = _rms(x) * g_ref[...] * (1.0 + mod_ref[1:2, :]) + mod_ref[0:1, :]
    hb = h.astype(BF16)

    def proj(a, b):
        return _dot_nt(hb, w_ref[a:b, :])

    p = proj(C_NA, C_NA + 768)
    qna_ref[...] = (p[:, :256] * NA_HEAD_DIM ** -0.5).astype(BF16)
    kna_ref[...] = p[:, 256:512].astype(BF16)
    vna_ref[...] = p[:, 512:768].astype(BF16)

    p = proj(C_MLA, C_MLA + 512)
    mc, mu, md = mc_ref[...], mu_ref[...], md_ref[...]
    nq = (_rms(p[:, :256]) * qng_ref[...]).astype(BF16)
    qb = _dot(nq, wuq_ref[...])
    mla_scale = (MLA_NOPE + MLA_ROPE) ** -0.5 * LOG2E
    for s in range(MLA_HEADS):
        sl = slice(s * LANES, (s + 1) * LANES)
        qm_ref[:, sl] = (_rope_slab(qb[:, sl], mc, mu, md) * mla_scale).astype(BF16)
    nkv = (_rms(p[:, 256:384]) * kvng_ref[...]).astype(BF16)
    kvb = _dot(nkv, wukv_ref[...])
    kpe = _rope_slab(p[:, 384:512], mc, mu, md)
    for s in range(MLA_HEADS):
        sl = slice(s * LANES, (s + 1) * LANES)
        km_ref[:, sl] = (kvb[:, sl] + kpe).astype(BF16)
    vm_ref[...] = kvb[:, 512:768].astype(BF16)

    p = proj(C_GQA, C_GQA + 768)
    ac, au, ad = ac_ref[...], au_ref[...], ad_ref[...]
    bd = bd_ref[...]
    for s in range(5):
        sl = slice(s * LANES, (s + 1) * LANES)
        v = p[:, sl]
        sq = v * v
        hi = sq.astype(BF16)
        lo = (sq - hi.astype(F32)).astype(BF16)
        msq = _dot(jnp.concatenate([hi, lo], axis=1), bd)
        y = v * lax.rsqrt(msq + NORM_EPS) * gqk_ref[:, sl]
        r = _rope_slab(y, ac, au, ad)
        if s < 4:
            qg_ref[:, sl] = (r * (GQA_HEAD_DIM ** -0.5 * LOG2E)).astype(BF16)
        else:
            kg_ref[...] = r.astype(BF16)
    vg_ref[...] = p[:, 640:768].astype(BF16)

    for j in range(3):
        g = proj(C_GATE + j * D_MODEL, C_GATE + (j + 1) * D_MODEL)
        gate_ref[:, j * D_MODEL:(j + 1) * D_MODEL] = _sigmoid(g).astype(BF16)


def _in_call(x, mod, l, g, w, qng, kvng, wuq, wukv, gqk, bd, mtabs, atabs):
    B, S, D = x.shape
    tm = TOKEN_TILE
    tok = lambda w_: pl.BlockSpec((None, tm, w_), lambda b, i: (b, i, 0))
    full = lambda a: pl.BlockSpec(a.shape, lambda b, i: (0,) * a.ndim)
    tab = pl.BlockSpec((tm, LANES), lambda b, i: (i, 0))
    widths = (256, 256, 256, 512, 512, 256, 512, 128, 128, 3 * D)
    return pl.pallas_call(
        _in_kernel,
        grid=(B, S // tm),
        in_specs=[tok(D), _mod_spec(mod, l)] + [_layer_spec(a, l) for a in (g, w, qng, kvng, wuq, wukv, gqk)]
        + [full(bd)] + [tab] * 6,
        out_specs=[tok(w_) for w_ in widths],
        out_shape=[jax.ShapeDtypeStruct((B, S, w_), BF16) for w_ in widths],
        compiler_params=pltpu.CompilerParams(dimension_semantics=("parallel", "parallel"),
                                             vmem_limit_bytes=VMEM_LIMIT),
        name="in_proj",
    )(x, mod, g, w, qng, kvng, wuq, wukv, gqk, bd, *mtabs, *atabs)


def _na_kernel(q_ref, k_ref, v_ref, bias_ref, o_ref, *, rows):
    kh = NA_KH
    head_of_lane = lax.broadcasted_iota(jnp.int32, (GRID_W, NA_HEADS * NA_HEAD_DIM), 1) // NA_HEAD_DIM

    def body(r, carry):
        rs = jnp.clip(r - kh // 2, 0, rows - kh)
        q = q_ref[pl.ds(pl.multiple_of(r * GRID_W, GRID_W), GRID_W), :]
        kstart = pl.multiple_of(rs * GRID_W, GRID_W)
        kb = k_ref[pl.ds(kstart, kh * GRID_W), :]
        vb = v_ref[pl.ds(kstart, kh * GRID_W), :]
        qs = jnp.concatenate([jnp.where(head_of_lane == h, q, jnp.zeros_like(q)) for h in range(NA_HEADS)],
                             axis=0)
        s = _dot_nt(qs, kb) + bias_ref[r - rs]
        m = jnp.max(s, axis=-1, keepdims=True)
        p = jnp.exp(s - m)
        l = jnp.sum(p, axis=-1, keepdims=True)
        o = _dot(p.astype(BF16), vb) / l
        out = jnp.zeros((GRID_W, NA_HEADS * NA_HEAD_DIM), F32)
        for h in range(NA_HEADS):
            out = jnp.where(head_of_lane == h, o[h * GRID_W:(h + 1) * GRID_W, :], out)
        o_ref[pl.ds(pl.multiple_of(r * GRID_W, GRID_W), GRID_W), :] = out.astype(BF16)
        return carry

    lax.fori_loop(0, rows, body, 0, unroll=32)


def _na_call(q, k, v, bias, l):
    B, S, W = q.shape
    rows = S // GRID_W
    blk = pl.BlockSpec((None, S, W), lambda b: (b, 0, 0))
    return pl.pallas_call(
        functools.partial(_na_kernel, rows=rows),
        grid=(B,),
        in_specs=[blk, blk, blk, _layer_spec(bias, l)],
        out_specs=blk,
        out_shape=jax.ShapeDtypeStruct((B, S, W), BF16),
        compiler_params=pltpu.CompilerParams(dimension_semantics=("parallel",),
                                             vmem_limit_bytes=VMEM_LIMIT),
        name="na_attn",
    )(q, k, v, bias)


L_MIN = 2.0 ** -60
MLA_BIAS_LANE = MLA_NOPE + MLA_ROPE


def _softmax_pv(q, k, v):
    s = _dot_nt(q, k)
    m = jnp.max(s, axis=-1, keepdims=True)
    p = jnp.exp2(s - m)
    l = jnp.sum(p, axis=-1, keepdims=True)
    return _dot(p.astype(BF16), v) / l


VT_ROWS = 80


def _store_vt(vt_ref, idx, vt_rows):
    hd = VT_ROWS - 16
    S = vt_rows.shape[1]
    vt_ref[idx, 0:hd, :] = vt_rows.astype(BF16)
    first = lax.broadcasted_iota(jnp.int32, (16, S), 0) == 0
    vt_ref[idx, hd:VT_ROWS, :] = jnp.where(first, 1.0, 0.0).astype(BF16)


def _row_norm(x):
    xf = x.astype(F32)
    return jnp.sqrt(jnp.sum(xf * xf, axis=-1, keepdims=True))


def _max_row_norm(x, mask=None):
    xf = x.astype(F32)
    sq = xf * xf if mask is None else jnp.where(mask, xf * xf, 0.0)
    n = jnp.max(jnp.sum(sq, axis=-1, keepdims=True), axis=0, keepdims=True)
    return jnp.broadcast_to(jnp.sqrt(n), (1, LANES))


def _mla_exact(q_ref, k_ref, v_ref, o_ref):
    v = v_ref[...]
    head_of_lane = lax.broadcasted_iota(jnp.int32, o_ref.shape, 1) // MLA_V
    out = jnp.zeros(o_ref.shape, F32)
    for h in range(MLA_HEADS):
        sl = slice(h * LANES, (h + 1) * LANES)
        out = jnp.where(head_of_lane == h, _softmax_pv(q_ref[:, sl], k_ref[:, sl], v), out)
    o_ref[...] = out.astype(BF16)


def _pipelined_attention(init, scores, values, exact, pt_refs, lmin_ref):
    i = pl.program_id(1)
    n_tiles = pl.num_programs(1) - 1

    @pl.when(i == 0)
    def _():
        init()
        lmin_ref[0] = 1.0
        scores(pt_refs[0])

    for parity in (0, 1):
        @pl.when((i > 0) & (i < n_tiles) & (i % 2 == parity))
        def _():
            lmin_ref[0] = values(pt_refs[1 - parity])
            scores(pt_refs[parity])

    for parity in (0, 1):
        @pl.when((i == n_tiles) & (i % 2 == parity))
        def _():
            lmin_ref[0] = values(pt_refs[1 - parity])

    @pl.when((i > 0) & (lmin_ref[0] < L_MIN))
    def _():
        exact()


def _serial_attention(init, scores, values, exact, pt_ref):
    @pl.when(pl.program_id(1) == 0)
    def _():
        init()

    scores(pt_ref)
    lmin = values(pt_ref)

    @pl.when(lmin < L_MIN)
    def _():
        exact()


def _unpack_attn_refs(refs, pipelined):
    if pipelined:
        q, qprev, k, v, o, kaug, vt, kmax, pta, ptb, lmin = refs
        return q, qprev, k, v, o, kaug, vt, kmax, (pta, ptb), lmin
    q, k, v, o, kaug, vt, kmax, pt = refs
    return q, q, k, v, o, kaug, vt, kmax, (pt,), None


def _run_attention(pipelined, init, scores, values, exact, pt_refs, lmin_ref):
    if pipelined:
        _pipelined_attention(init, scores, values, exact, pt_refs, lmin_ref)
    else:
        _serial_attention(init, scores, values, exact, pt_refs[0])


def _mla_attn_kernel(*refs, pipelined):
    q_ref, qprev_ref, k_ref, v_ref, o_ref, kaug_ref, vt_ref, kmax_ref, pt_refs, lmin_ref = _unpack_attn_refs(
        refs, pipelined)
    S, tq = k_ref.shape[0], q_ref.shape[0]

    def init():
        lane_s = lax.broadcasted_iota(jnp.int32, (S, LANES), 1)
        for h in range(MLA_HEADS):
            sl = slice(h * LANES, (h + 1) * LANES)
            k = k_ref[:, sl]
            kaug_ref[:, sl] = jnp.where(lane_s == MLA_BIAS_LANE, jnp.ones_like(k), k)
            kmax_ref[h:h + 1, :] = _max_row_norm(k)
        for p in range(MLA_HEADS // 2):
            vt = jnp.transpose(v_ref[:, p * LANES:(p + 1) * LANES].astype(F32))
            _store_vt(vt_ref, 2 * p, vt[0:MLA_V])
            _store_vt(vt_ref, 2 * p + 1, vt[MLA_V:2 * MLA_V])

    def scores(pt_ref):
        lane = lax.broadcasted_iota(jnp.int32, (tq, LANES), 1)
        for h in range(MLA_HEADS):
            sl = slice(h * LANES, (h + 1) * LANES)
            q = q_ref[:, sl]
            b = _row_norm(q) * kmax_ref[h:h + 1, 0:1]
            q_aug = jnp.where(lane == MLA_BIAS_LANE, (-b).astype(BF16), q)
            pt_ref[:, h * tq:(h + 1) * tq] = jnp.exp2(_dot_nt(kaug_ref[:, sl], q_aug)).astype(BF16)

    def values(pt_ref):
        lmin = None
        for p in range(MLA_HEADS // 2):
            outs = []
            for h in (2 * p, 2 * p + 1):
                ot = _dot(vt_ref[h], pt_ref[:, h * tq:(h + 1) * tq])
                l = ot[MLA_V:MLA_V + 1, :]
                outs.append(ot[0:MLA_V, :] / l)
                lmin = l if lmin is None else jnp.minimum(lmin, l)
            o_ref[:, p * LANES:(p + 1) * LANES] = jnp.transpose(jnp.concatenate(outs, axis=0)).astype(BF16)
        return jnp.min(lmin)

    _run_attention(pipelined, init, scores, values, lambda: _mla_exact(qprev_ref, k_ref, v_ref, o_ref),
                   pt_refs, lmin_ref)


def _gqa_exact(q_ref, k_ref, v_ref, o_ref):
    k = k_ref[...]
    v = v_ref[...]
    low = lax.broadcasted_iota(jnp.int32, (q_ref.shape[0], LANES), 1) < GQA_HEAD_DIM
    for j in range(GQA_HEADS // GQA_KV_HEADS):
        sl = slice(j * LANES, (j + 1) * LANES)
        q = q_ref[:, sl]
        zero = jnp.zeros_like(q)
        o_lo = _softmax_pv(jnp.where(low, q, zero), k, v)
        o_hi = _softmax_pv(jnp.where(low, zero, q), k, v)
        o_ref[:, sl] = jnp.where(low, o_lo, o_hi).astype(BF16)


def _gqa_attn_kernel(*refs, pipelined):
    q_ref, qprev_ref, k_ref, v_ref, o_ref, kaug_ref, vt_ref, kmax_ref, pt_refs, lmin_ref = _unpack_attn_refs(
        refs, pipelined)
    S, tq = k_ref.shape[0], q_ref.shape[0]
    pairs = GQA_HEADS // GQA_KV_HEADS
    n = pairs * tq

    def init():
        lane_s = lax.broadcasted_iota(jnp.int32, (S, LANES), 1)
        k = k_ref[...]
        kaug_ref[:, :LANES] = k
        kaug_ref[:, LANES:] = jnp.where(lane_s == 0, 1.0, 0.0).astype(BF16)
        vt = jnp.transpose(v_ref[...].astype(F32))
        for g in range(GQA_KV_HEADS):
            _store_vt(vt_ref, g, vt[g * GQA_HEAD_DIM:(g + 1) * GQA_HEAD_DIM])
        kmax_ref[0:1, :] = _max_row_norm(k, lane_s < GQA_HEAD_DIM)
        kmax_ref[1:2, :] = _max_row_norm(k, lane_s >= GQA_HEAD_DIM)

    def scores(pt_ref):
        lane = lax.broadcasted_iota(jnp.int32, (tq, LANES), 1)
        low = lane < GQA_HEAD_DIM
        q_aug = [[], []]
        for j in range(pairs):
            q = q_ref[:, j * LANES:(j + 1) * LANES]
            zero = jnp.zeros_like(q)
            for half in range(GQA_KV_HEADS):
                qm = jnp.where(low, q, zero) if half == 0 else jnp.where(low, zero, q)
                b = _row_norm(qm) * kmax_ref[half:half + 1, 0:1]
                bias = jnp.where(lane == 0, -b, 0.0).astype(BF16)
                q_aug[half].append(jnp.concatenate([qm, bias], axis=1))
        pt_ref[...] = jnp.exp2(_dot_nt(kaug_ref[...], jnp.concatenate(q_aug[0] + q_aug[1], axis=0))).astype(BF16)

    def values(pt_ref):
        outs, lmin = [], None
        for half in range(GQA_KV_HEADS):
            ot = _dot(vt_ref[half], pt_ref[:, half * n:(half + 1) * n])
            l = ot[GQA_HEAD_DIM:GQA_HEAD_DIM + 1, :]
            outs.append(ot[0:GQA_HEAD_DIM, :] / l)
            lmin = l if lmin is None else jnp.minimum(lmin, l)
        for j in range(pairs):
            ot = jnp.concatenate([o[:, j * tq:(j + 1) * tq] for o in outs], axis=0)
            o_ref[:, j * LANES:(j + 1) * LANES] = jnp.transpose(ot).astype(BF16)
        return jnp.min(lmin)

    _run_attention(pipelined, init, scores, values, lambda: _gqa_exact(qprev_ref, k_ref, v_ref, o_ref),
                   pt_refs, lmin_ref)


def _attn_call(kernel, q, k, v, out_width, tq, heads, scratch, name, pipelined):
    B, S, _ = q.shape
    n_tiles = S // tq
    pt = pltpu.VMEM((S, heads * tq), BF16)
    kv_specs = [pl.BlockSpec((None, S, k.shape[2]), lambda b, i: (b, 0, 0)),
                pl.BlockSpec((None, S, v.shape[2]), lambda b, i: (b, 0, 0))]
    scratch = scratch + [pltpu.VMEM((8, LANES), F32)]
    if pipelined:
        cur = lambda b, i: (b, jnp.minimum(i, n_tiles - 1), 0)
        prev = lambda b, i: (b, jnp.maximum(i - 1, 0), 0)
        grid = (B, n_tiles + 1)
        q_specs = [pl.BlockSpec((None, tq, q.shape[2]), cur), pl.BlockSpec((None, tq, q.shape[2]), prev)]
        out_spec = pl.BlockSpec((None, tq, out_width), prev)
        scratch = scratch + [pt, pt, pltpu.SMEM((1,), F32)]
    else:
        tile = lambda b, i: (b, i, 0)
        grid = (B, n_tiles)
        q_specs = [pl.BlockSpec((None, tq, q.shape[2]), tile)]
        out_spec = pl.BlockSpec((None, tq, out_width), tile)
        scratch = scratch + [pt]
    return pl.pallas_call(
        functools.partial(kernel, pipelined=pipelined),
        grid=grid,
        in_specs=q_specs + kv_specs,
        out_specs=out_spec,
        out_shape=jax.ShapeDtypeStruct((B, S, out_width), BF16),
        scratch_shapes=scratch,
        compiler_params=pltpu.CompilerParams(dimension_semantics=("parallel", "arbitrary"),
                                             vmem_limit_bytes=VMEM_LIMIT),
        name=name,
    )(*([q] * len(q_specs)), k, v)


def _post_kernel(x_ref, mod_ref, ona_ref, omla_ref, ogqa_ref, gate_ref, wa_ref, wb_ref, wc_ref, wo_ref,
                 g_ref, wgu_ref, wd_ref, fg_ref, o_ref, *, final_norm):
    D = D_MODEL
    m = (gate_ref[:, 0:D].astype(F32) * _dot(ona_ref[...], wa_ref[...])
         + gate_ref[:, D:2 * D].astype(F32) * _dot(omla_ref[...], wb_ref[...])
         + gate_ref[:, 2 * D:3 * D].astype(F32) * _dot(ogqa_ref[...], wc_ref[...]))
    x = x_ref[...] + mod_ref[2:3, :] * _dot(m.astype(BF16), wo_ref[...])

    h = _rms(x) * g_ref[...] * (1.0 + mod_ref[4:5, :]) + mod_ref[3:4, :]
    hb = h.astype(BF16)
    acc = jnp.zeros(x.shape, F32)
    for c0, c1 in FFN_CHUNKS:
        gate = _dot(hb, wgu_ref[:, c0:c1])
        up = _dot(hb, wgu_ref[:, FFN_HIDDEN + c0:FFN_HIDDEN + c1])
        act = (gate * _sigmoid(gate) * up).astype(BF16)
        acc = acc + _dot(act, wd_ref[c0:c1, :])
    y = x + mod_ref[5:6, :] * acc
    if final_norm:
        y = _rms(y) * fg_ref[...]
    o_ref[...] = y


def _post_call(x, mod, l, ona, omla, ogqa, gates, wa, wb, wc, wo, g, wgu, wd, fg, final_norm):
    B, S, D = x.shape
    tm = TOKEN_TILE
    tok = lambda a: pl.BlockSpec((None, tm, a.shape[2]), lambda b, i: (b, i, 0))
    weights = (wa, wb, wc, wo, g, wgu, wd)
    return pl.pallas_call(
        functools.partial(_post_kernel, final_norm=final_norm),
        grid=(B, S // tm),
        in_specs=[tok(x), _mod_spec(mod, l), tok(ona), tok(omla), tok(ogqa), tok(gates)]
        + [_layer_spec(a, l, single_buffer=True) for a in weights]
        + [pl.BlockSpec(fg.shape, lambda b, i: (0, 0))],
        out_specs=tok(x),
        out_shape=jax.ShapeDtypeStruct(x.shape, F32),
        compiler_params=pltpu.CompilerParams(dimension_semantics=("parallel", "parallel"),
                                             vmem_limit_bytes=VMEM_LIMIT),
        name="merge_swiglu",
    )(x, mod, ona, omla, ogqa, gates, *weights, fg)


def _rope_tables(S):
    f32 = np.float32
    t = np.arange(S, dtype=np.int32)
    pos_t = t.astype(f32)
    pos_row = (t // GRID_W).astype(f32)
    pos_col = (t % GRID_W).astype(f32)
    half = 16
    lane = np.arange(LANES)

    def tables(pos_of_lane, roped, first_half, inv_freq):
        ang = (pos_of_lane * inv_freq[lane % half][None, :]).astype(f32)
        cos, sin = np.cos(ang), np.sin(ang)
        c = np.where(roped[None, :], cos, f32(1.0))
        up = np.where((roped & first_half)[None, :], -sin, f32(0.0))
        dn = np.where((roped & ~first_half)[None, :], sin, f32(0.0))
        return tuple(jnp.asarray(a, F32) for a in (c, up, dn))

    inv_t = (f32(1.0) / (f32(ROPE_THETA) ** (np.arange(half, dtype=f32) / f32(half)))).astype(f32)
    inv_a = (f32(1.0) / (f32(AXIAL_THETA) ** (np.arange(half, dtype=f32) / f32(half)))).astype(f32)
    m_roped = (lane >= MLA_NOPE) & (lane < MLA_NOPE + MLA_ROPE)
    m_first = ((lane - MLA_NOPE) % 32) < half
    mt = tables(np.broadcast_to(pos_t[:, None], (S, LANES)), m_roped, m_first, inv_t)
    a_first = (lane % 32) < half
    use_row = (lane % GRID_W) < 32
    pos_rc = np.where(use_row[None, :], pos_row[:, None], pos_col[:, None])
    at = tables(pos_rc, np.ones(LANES, bool), a_first, inv_a)
    return mt, at


def _na_bias_table(rpb):
    L, W = rpb.shape[0], GRID_W
    pad = W - NA_KW
    ext = jnp.concatenate([jnp.repeat(rpb[..., :1], pad, -1), rpb, jnp.repeat(rpb[..., -1:], pad, -1)], -1)
    t = jnp.stack([ext[..., W - 1 - c:2 * W - 1 - c] for c in range(W)], axis=-2)
    c = np.arange(W)[:, None, None]
    kc = np.arange(W)[None, None, :]
    ws = np.clip(c - NA_KW // 2, 0, W - NA_KW)
    valid = (kc >= ws) & (kc < ws + NA_KW)
    out = []
    for d in range(NA_KH):
        td = jnp.transpose(t[:, :, NA_KH - 1 - d:2 * NA_KH - 1 - d], (0, 1, 3, 2, 4))
        out.append(jnp.where(valid, td.astype(F32), MASK_VALUE))
    return jnp.stack(out, axis=1).reshape(L, NA_KH, NA_HEADS * W, NA_KH * W)


W_IN_KR = 1152
W_IN_QC = 1184
W_IN_COLS = 5024


def _w_in_prep_kernel(w_ref, o_ref):
    tk = w_ref.shape[1]
    hd, pairs = GQA_HEAD_DIM, GQA_HEADS // GQA_KV_HEADS
    kr0 = C_MLA + 384 + MLA_NOPE
    o_ref[0:W_IN_KR, :] = w_ref[0:W_IN_KR, :].astype(BF16)
    o_ref[C_MLA + 384:kr0, :] = jnp.zeros((MLA_NOPE, tk), BF16)
    o_ref[kr0:kr0 + MLA_ROPE, :] = w_ref[W_IN_KR:W_IN_QC, :].astype(BF16)
    o_ref[kr0 + MLA_ROPE:C_GQA, :] = jnp.zeros((LANES - MLA_NOPE - MLA_ROPE, tk), BF16)
    for j in range(pairs):
        for half, head in enumerate((j, pairs + j)):
            dst = C_GQA + LANES * j + hd * half
            o_ref[dst:dst + hd, :] = w_ref[W_IN_QC + hd * head:W_IN_QC + hd * (head + 1), :].astype(BF16)
    o_ref[C_GQA + 512:C_TOTAL, :] = w_ref[W_IN_QC + 512:W_IN_COLS, :].astype(BF16)


def _prep_w_in(w):
    L, D, N = w.shape
    tk = 256
    return pl.pallas_call(
        _w_in_prep_kernel,
        grid=(L, D // tk),
        in_specs=[pl.BlockSpec((None, N, tk), lambda l, i: (l, 0, i))],
        out_specs=pl.BlockSpec((None, C_TOTAL, tk), lambda l, i: (l, 0, i)),
        out_shape=jax.ShapeDtypeStruct((L, C_TOTAL, D), BF16),
        compiler_params=pltpu.CompilerParams(dimension_semantics=("parallel", "parallel"),
                                             vmem_limit_bytes=VMEM_LIMIT),
        name="w_in_prep",
    )(jnp.swapaxes(w, 1, 2))


def _prep_w_uq(w):
    z = jnp.zeros(w.shape[:-1] + (32,), BF16)
    hd = MLA_NOPE + MLA_ROPE
    return jnp.concatenate([a for h in range(MLA_HEADS) for a in (w[..., hd * h:hd * (h + 1)].astype(BF16), z)],
                           axis=-1)


def _prep_w_ukv(w):
    z = jnp.zeros(w.shape[:-1] + (64,), BF16)
    hd = MLA_NOPE + MLA_V
    ks = [a for h in range(MLA_HEADS) for a in (w[..., hd * h:hd * h + MLA_NOPE].astype(BF16), z)]
    vs = [w[..., hd * h + MLA_NOPE:hd * (h + 1)].astype(BF16) for h in range(MLA_HEADS)]
    return jnp.concatenate(ks + vs, axis=-1)


def _prep_w_br_gqa(w):
    rows = []
    for j in range(4):
        rows += [w[:, 64 * j:64 * (j + 1)], w[:, 64 * (4 + j):64 * (5 + j)]]
    return jnp.concatenate([r.astype(BF16) for r in rows], axis=1)


def kernel(x, c, ada_w, ada_b, norm_mix_g, norm_ffn_g, w_in, na_rpb, mla_q_norm_g, mla_kv_norm_g, mla_w_uq,
           mla_w_ukv, gqa_q_norm_g, gqa_k_norm_g, w_br_na, w_br_mla, w_br_gqa, w_out, ffn_w_gate_up,
           ffn_w_down, final_norm_g):
    B, S, D = x.shape
    L = ada_w.shape[0]
    mod = _ada_call(c, ada_w, ada_b).reshape(L, B, N_MOD, D)
    mtabs, atabs = _rope_tables(S)
    lane = np.arange(LANES)
    bd = jnp.asarray(np.tile((lane[:, None] // 64 == lane[None, :] // 64) / 64.0, (2, 1)), BF16)
    row = lambda a: a.reshape(L, 1, -1)
    gqk = jnp.concatenate([jnp.tile(gqa_q_norm_g, (1, GQA_HEADS)), jnp.tile(gqa_k_norm_g, (1, GQA_KV_HEADS))], 1)
    w_in_p, w_uq_p, w_ukv_p = _prep_w_in(w_in), _prep_w_uq(mla_w_uq), _prep_w_ukv(mla_w_ukv)
    bias = _na_bias_table(na_rpb)
    wa, wb, wc, wo = w_br_na.astype(BF16), w_br_mla.astype(BF16), _prep_w_br_gqa(w_br_gqa), w_out.astype(BF16)
    wgu, wd = ffn_w_gate_up.astype(BF16), ffn_w_down.astype(BF16)
    fg = final_norm_g.reshape(1, D)
    for l in range(L):
        qna, kna, vna, qm, km, vm, qg, kg, vg, gates = _in_call(
            x, mod, l, row(norm_mix_g), w_in_p, row(mla_q_norm_g), row(mla_kv_norm_g), w_uq_p, w_ukv_p,
            row(gqk), bd, mtabs, atabs)
        o_na = _na_call(qna, kna, vna, bias, l)
        o_mla = _attn_call(_mla_attn_kernel, qm, km, vm, MLA_HEADS * MLA_V, MLA_Q_TILE, MLA_HEADS,
                           [pltpu.VMEM((S, MLA_HEADS * LANES), BF16),
                            pltpu.VMEM((MLA_HEADS, VT_ROWS, S), BF16)], "mla_attn", pipelined=False)
        o_gqa = _attn_call(_gqa_attn_kernel, qg, kg, vg, GQA_HEADS * GQA_HEAD_DIM, GQA_Q_TILE, GQA_HEADS,
                           [pltpu.VMEM((S, 2 * LANES), BF16), pltpu.VMEM((GQA_KV_HEADS, VT_ROWS, S), BF16)],
                           "gqa_attn", pipelined=True)
        x = _post_call(x, mod, l, o_na, o_mla, o_gqa, gates, wa, wb, wc, wo, row(norm_ffn_g), wgu, wd, fg,
                       final_norm=(l == L - 1))
    return x
```

```python
import functools

import numpy as np
import jax
import jax.numpy as jnp
from jax import lax
from jax.experimental import pallas as pl
from jax.experimental.pallas import tpu as pltpu

D_MODEL = 1024
GRID_W = 64
NA_HEADS = 4
NA_HEAD_DIM = 64
NA_KH = 8
NA_KW = 16
MLA_HEADS = 4
MLA_Q_LORA = 256
MLA_KV_LORA = 128
MLA_NOPE = 64
MLA_ROPE = 32
MLA_V = 64
GQA_HEADS = 8
GQA_KV_HEADS = 2
GQA_HEAD_DIM = 64
FFN_HIDDEN = 2816
ROPE_THETA = 10000.0
AXIAL_THETA = 10000.0
NORM_EPS = 1e-6
N_MOD = 6
MASK_VALUE = -1e30
LOG2E = 1.4426950408889634

LANES = 128
VMEM_LIMIT = 56 * 1024 * 1024
TOKEN_TILE = 512
MXU_TILE = 256
MLA_Q_TILE = 512
GQA_Q_TILE = 256
_FFN_SPLIT = (FFN_HIDDEN // MXU_TILE + 1) // 2 * MXU_TILE
FFN_CHUNKS = ((0, _FFN_SPLIT), (_FFN_SPLIT, FFN_HIDDEN))

C_NA = 0
C_MLA = 768
C_GQA = 1280
C_GATE = 2048
C_TOTAL = 5120

F32 = jnp.float32
BF16 = jnp.bfloat16


def _dot(a, b):
    return jnp.dot(a, b, preferred_element_type=F32)


def _dot_nt(a, b):
    return lax.dot_general(a, b, (((1,), (1,)), ((), ())), preferred_element_type=F32)


def _rms(x):
    return x * lax.rsqrt(jnp.mean(x * x, axis=-1, keepdims=True) + NORM_EPS)


def _rope_slab(x, c, s_up, s_dn):
    return x * c + pltpu.roll(x, LANES - 16, 1) * s_up + pltpu.roll(x, 16, 1) * s_dn


def _sigmoid(x):
    return 1.0 / (1.0 + jnp.exp(-x))


def _layer_spec(a, l, single_buffer=False):
    mode = pl.Buffered(1) if single_buffer else None
    return pl.BlockSpec((None,) + a.shape[1:], lambda *_: (l,) + (0,) * (a.ndim - 1), pipeline_mode=mode)


def _mod_spec(mod, l):
    return pl.BlockSpec((None, None) + mod.shape[2:], lambda b, *_: (l, b, 0, 0))


def _ada_kernel(c_ref, w_ref, b_ref, o_ref):
    c = c_ref[...]
    s = (c * _sigmoid(c)).astype(BF16)
    o_ref[...] = _dot(s, w_ref[...].astype(BF16)) + b_ref[...]


def _ada_call(c, ada_w, ada_b):
    L, D, N = ada_w.shape
    B = c.shape[0]
    tn = 1536
    return pl.pallas_call(
        _ada_kernel,
        grid=(L, N // tn),
        in_specs=[pl.BlockSpec((B, D), lambda l, j: (0, 0)),
                  pl.BlockSpec((None, D, tn), lambda l, j: (l, 0, j)),
                  pl.BlockSpec((None, 1, tn), lambda l, j: (l, 0, j))],
        out_specs=pl.BlockSpec((None, B, tn), lambda l, j: (l, 0, j)),
        out_shape=jax.ShapeDtypeStruct((L, B, N), F32),
        compiler_params=pltpu.CompilerParams(dimension_semantics=("parallel", "parallel"),
                                             vmem_limit_bytes=VMEM_LIMIT),
        name="ada_mod",
    )(c, ada_w, ada_b.reshape(L, 1, N))


def _in_kernel(x_ref, mod_ref, g_ref, w_ref, qng_ref, kvng_ref, wuq_ref, wukv_ref, gqk_ref, bd_ref,
               mc_ref, mu_ref, md_ref, ac_ref, au_ref, ad_ref,
               qna_ref, kna_ref, vna_ref, qm_ref, km_ref, vm_ref, qg_ref, kg_ref, vg_ref, gate_ref):
    x = x_ref[...]
    h = _rms(x) * g_ref[...] * (1.0 + mod_ref[1:2, :]) + mod_ref[0:1, :]
    hb = h.astype(BF16)

    def proj(a, b):
        return _dot_nt(hb, w_ref[a:b, :])

    p = proj(C_NA, C_NA + 768)
    qna_ref[...] = (p[:, :256] * (NA_HEAD_DIM ** -0.5 * LOG2E)).astype(BF16)
    kna_ref[...] = p[:, 256:512].astype(BF16)
    vna_ref[...] = p[:, 512:768].astype(BF16)

    p = proj(C_MLA, C_MLA + 512)
    mc, mu, md = mc_ref[...], mu_ref[...], md_ref[...]
    nq = (_rms(p[:, :256]) * qng_ref[...]).astype(BF16)
    qb = _dot(nq, wuq_ref[...])
    mla_scale = (MLA_NOPE + MLA_ROPE) ** -0.5 * LOG2E
    for s in range(MLA_HEADS):
        sl = slice(s * LANES, (s + 1) * LANES)
        qm_ref[:, sl] = (_rope_slab(qb[:, sl], mc, mu, md) * mla_scale).astype(BF16)
    nkv = (_rms(p[:, 256:384]) * kvng_ref[...]).astype(BF16)
    kvb = _dot(nkv, wukv_ref[...])
    kpe = _rope_slab(p[:, 384:512], mc, mu, md)
    for s in range(MLA_HEADS):
        sl = slice(s * LANES, (s + 1) * LANES)
        km_ref[:, sl] = (kvb[:, sl] + kpe).astype(BF16)
    vm_ref[...] = kvb[:, 512:768].astype(BF16)

    p = proj(C_GQA, C_GQA + 768)
    ac, au, ad = ac_ref[...], au_ref[...], ad_ref[...]
    bd = bd_ref[...]
    for s in range(5):
        sl = slice(s * LANES, (s + 1) * LANES)
        v = p[:, sl]
        sq = v * v
        hi = sq.astype(BF16)
        lo = (sq - hi.astype(F32)).astype(BF16)
        msq = _dot(jnp.concatenate([hi, lo], axis=1), bd)
        y = v * lax.rsqrt(msq + NORM_EPS) * gqk_ref[:, sl]
        r = _rope_slab(y, ac, au, ad)
        if s < 4:
            qg_ref[:, sl] = (r * (GQA_HEAD_DIM ** -0.5 * LOG2E)).astype(BF16)
        else:
            kg_ref[...] = r.astype(BF16)
    vg_ref[...] = p[:, 640:768].astype(BF16)

    for j in range(3):
        g = proj(C_GATE + j * D_MODEL, C_GATE + (j + 1) * D_MODEL)
        gate_ref[:, j * D_MODEL:(j + 1) * D_MODEL] = _sigmoid(g).astype(BF16)


def _in_call(x, mod, l, g, w, qng, kvng, wuq, wukv, gqk, bd, mtabs, atabs):
    B, S, D = x.shape
    tm = TOKEN_TILE
    tok = lambda w_: pl.BlockSpec((None, tm, w_), lambda b, i: (b, i, 0))
    full = lambda a: pl.BlockSpec(a.shape, lambda b, i: (0,) * a.ndim)
    tab = pl.BlockSpec((tm, LANES), lambda b, i: (i, 0))
    widths = (256, 256, 256, 512, 512, 256, 512, 128, 128, 3 * D)
    return pl.pallas_call(
        _in_kernel,
        grid=(B, S // tm),
        in_specs=[tok(D), _mod_spec(mod, l)] + [_layer_spec(a, l) for a in (g, w, qng, kvng, wuq, wukv, gqk)]
        + [full(bd)] + [tab] * 6,
        out_specs=[tok(w_) for w_ in widths],
        out_shape=[jax.ShapeDtypeStruct((B, S, w_), BF16) for w_ in widths],
        compiler_params=pltpu.CompilerParams(dimension_semantics=("parallel", "parallel"),
                                             vmem_limit_bytes=VMEM_LIMIT),
        name="in_proj",
    )(x, mod, g, w, qng, kvng, wuq, wukv, gqk, bd, *mtabs, *atabs)


def _na_kernel(q_ref, k_ref, v_ref, bias_ref, o_ref, *, rows):
    kh = NA_KH
    head_of_lane = lax.broadcasted_iota(jnp.int32, (GRID_W, NA_HEADS * NA_HEAD_DIM), 1) // NA_HEAD_DIM

    def body(r, carry):
        rs = jnp.clip(r - kh // 2, 0, rows - kh)
        q = q_ref[pl.ds(pl.multiple_of(r * GRID_W, GRID_W), GRID_W), :]
        kstart = pl.multiple_of(rs * GRID_W, GRID_W)
        kb = k_ref[pl.ds(kstart, kh * GRID_W), :]
        vb = v_ref[pl.ds(kstart, kh * GRID_W), :]
        qs = jnp.concatenate([jnp.where(head_of_lane == h, q, jnp.zeros_like(q)) for h in range(NA_HEADS)],
                             axis=0)
        s = _dot_nt(qs, kb) + bias_ref[r - rs]
        m = jnp.max(s, axis=-1, keepdims=True)
        p = jnp.exp2(s - m)
        l = jnp.sum(p, axis=-1, keepdims=True)
        o = _dot(p.astype(BF16), vb) / l
        out = jnp.zeros((GRID_W, NA_HEADS * NA_HEAD_DIM), F32)
        for h in range(NA_HEADS):
            out = jnp.where(head_of_lane == h, o[h * GRID_W:(h + 1) * GRID_W, :], out)
        o_ref[pl.ds(pl.multiple_of(r * GRID_W, GRID_W), GRID_W), :] = out.astype(BF16)
        return carry

    lax.fori_loop(0, rows, body, 0, unroll=32)


def _na_call(q, k, v, bias, l):
    B, S, W = q.shape
    rows = S // GRID_W
    blk = pl.BlockSpec((None, S, W), lambda b: (b, 0, 0))
    return pl.pallas_call(
        functools.partial(_na_kernel, rows=rows),
        grid=(B,),
        in_specs=[blk, blk, blk, _layer_spec(bias, l)],
        out_specs=blk,
        out_shape=jax.ShapeDtypeStruct((B, S, W), BF16),
        compiler_params=pltpu.CompilerParams(dimension_semantics=("parallel",),
                                             vmem_limit_bytes=VMEM_LIMIT),
        name="na_attn",
    )(q, k, v, bias)


L_MIN = 2.0 ** -60
MLA_BIAS_LANE = MLA_NOPE + MLA_ROPE


def _softmax_pv(q, k, v):
    s = _dot_nt(q, k)
    m = jnp.max(s, axis=-1, keepdims=True)
    p = jnp.exp2(s - m)
    l = jnp.sum(p, axis=-1, keepdims=True)
    return _dot(p.astype(BF16), v) / l


VT_ROWS = 80


def _store_vt(vt_ref, idx, vt_rows):
    hd = VT_ROWS - 16
    S = vt_rows.shape[1]
    vt_ref[idx, 0:hd, :] = vt_rows.astype(BF16)
    first = lax.broadcasted_iota(jnp.int32, (16, S), 0) == 0
    vt_ref[idx, hd:VT_ROWS, :] = jnp.where(first, 1.0, 0.0).astype(BF16)


def _row_norm(x):
    xf = x.astype(F32)
    return jnp.sqrt(jnp.sum(xf * xf, axis=-1, keepdims=True))


def _max_row_norm(x, mask=None):
    xf = x.astype(F32)
    sq = xf * xf if mask is None else jnp.where(mask, xf * xf, 0.0)
    n = jnp.max(jnp.sum(sq, axis=-1, keepdims=True), axis=0, keepdims=True)
    return jnp.broadcast_to(jnp.sqrt(n), (1, LANES))


def _mla_exact(q_ref, k_ref, v_ref, o_ref):
    v = v_ref[...]
    head_of_lane = lax.broadcasted_iota(jnp.int32, o_ref.shape, 1) // MLA_V
    out = jnp.zeros(o_ref.shape, F32)
    for h in range(MLA_HEADS):
        sl = slice(h * LANES, (h + 1) * LANES)
        out = jnp.where(head_of_lane == h, _softmax_pv(q_ref[:, sl], k_ref[:, sl], v), out)
    o_ref[...] = out.astype(BF16)


def _pipelined_attention(init, scores, values, exact, pt_refs, lmin_ref):
    i = pl.program_id(1)
    n_tiles = pl.num_programs(1) - 1

    @pl.when(i == 0)
    def _():
        init()
        lmin_ref[0] = 1.0
        scores(pt_refs[0])

    for parity in (0, 1):
        @pl.when((i > 0) & (i < n_tiles) & (i % 2 == parity))
        def _():
            lmin_ref[0] = values(pt_refs[1 - parity])
            scores(pt_refs[parity])

    for parity in (0, 1):
        @pl.when((i == n_tiles) & (i % 2 == parity))
        def _():
            lmin_ref[0] = values(pt_refs[1 - parity])

    @pl.when((i > 0) & (lmin_ref[0] < L_MIN))
    def _():
        exact()


def _serial_attention(init, scores, values, exact, pt_ref):
    @pl.when(pl.program_id(1) == 0)
    def _():
        init()

    scores(pt_ref)
    lmin = values(pt_ref)

    @pl.when(lmin < L_MIN)
    def _():
        exact()


def _unpack_attn_refs(refs, pipelined):
    if pipelined:
        q, qprev, k, v, o, kaug, vt, kmax, pta, ptb, lmin = refs
        return q, qprev, k, v, o, kaug, vt, kmax, (pta, ptb), lmin
    q, k, v, o, kaug, vt, kmax, pt = refs
    return q, q, k, v, o, kaug, vt, kmax, (pt,), None


def _run_attention(pipelined, init, scores, values, exact, pt_refs, lmin_ref):
    if pipelined:
        _pipelined_attention(init, scores, values, exact, pt_refs, lmin_ref)
    else:
        _serial_attention(init, scores, values, exact, pt_refs[0])


def _mla_attn_kernel(*refs, pipelined):
    q_ref, qprev_ref, k_ref, v_ref, o_ref, kaug_ref, vt_ref, kmax_ref, pt_refs, lmin_ref = _unpack_attn_refs(
        refs, pipelined)
    S, tq = k_ref.shape[0], q_ref.shape[0]

    def init():
        lane_s = lax.broadcasted_iota(jnp.int32, (S, LANES), 1)
        for h in range(MLA_HEADS):
            sl = slice(h * LANES, (h + 1) * LANES)
            k = k_ref[:, sl]
            kaug_ref[:, sl] = jnp.where(lane_s == MLA_BIAS_LANE, jnp.ones_like(k), k)
            kmax_ref[h:h + 1, :] = _max_row_norm(k)
        for p in range(MLA_HEADS // 2):
            vt = jnp.transpose(v_ref[:, p * LANES:(p + 1) * LANES].astype(F32))
            _store_vt(vt_ref, 2 * p, vt[0:MLA_V])
            _store_vt(vt_ref, 2 * p + 1, vt[MLA_V:2 * MLA_V])

    def scores(pt_ref):
        lane = lax.broadcasted_iota(jnp.int32, (tq, LANES), 1)
        for h in range(MLA_HEADS):
            sl = slice(h * LANES, (h + 1) * LANES)
            q = q_ref[:, sl]
            b = _row_norm(q) * kmax_ref[h:h + 1, 0:1]
            q_aug = jnp.where(lane == MLA_BIAS_LANE, (-b).astype(BF16), q)
            pt_ref[:, h * tq:(h + 1) * tq] = jnp.exp2(_dot_nt(kaug_ref[:, sl], q_aug)).astype(BF16)

    def values(pt_ref):
        lmin = None
        for p in range(MLA_HEADS // 2):
            outs = []
            for h in (2 * p, 2 * p + 1):
                ot = _dot(vt_ref[h], pt_ref[:, h * tq:(h + 1) * tq])
                l = ot[MLA_V:MLA_V + 1, :]
                outs.append(ot[0:MLA_V, :] / l)
                lmin = l if lmin is None else jnp.minimum(lmin, l)
            o_ref[:, p * LANES:(p + 1) * LANES] = jnp.transpose(jnp.concatenate(outs, axis=0)).astype(BF16)
        return jnp.min(lmin)

    _run_attention(pipelined, init, scores, values, lambda: _mla_exact(qprev_ref, k_ref, v_ref, o_ref),
                   pt_refs, lmin_ref)


def _gqa_exact(q_ref, k_ref, v_ref, o_ref):
    k = k_ref[...]
    v = v_ref[...]
    low = lax.broadcasted_iota(jnp.int32, (q_ref.shape[0], LANES), 1) < GQA_HEAD_DIM
    for j in range(GQA_HEADS // GQA_KV_HEADS):
        sl = slice(j * LANES, (j + 1) * LANES)
        q = q_ref[:, sl]
        zero = jnp.zeros_like(q)
        o_lo = _softmax_pv(jnp.where(low, q, zero), k, v)
        o_hi = _softmax_pv(jnp.where(low, zero, q), k, v)
        o_ref[:, sl] = jnp.where(low, o_lo, o_hi).astype(BF16)


def _gqa_attn_kernel(*refs, pipelined):
    q_ref, qprev_ref, k_ref, v_ref, o_ref, kaug_ref, vt_ref, kmax_ref, pt_refs, lmin_ref = _unpack_attn_refs(
        refs, pipelined)
    S, tq = k_ref.shape[0], q_ref.shape[0]
    pairs = GQA_HEADS // GQA_KV_HEADS
    n = pairs * tq

    def init():
        lane_s = lax.broadcasted_iota(jnp.int32, (S, LANES), 1)
        k = k_ref[...]
        kaug_ref[:, :LANES] = k
        kaug_ref[:, LANES:] = jnp.where(lane_s == 0, 1.0, 0.0).astype(BF16)
        vt = jnp.transpose(v_ref[...].astype(F32))
        for g in range(GQA_KV_HEADS):
            _store_vt(vt_ref, g, vt[g * GQA_HEAD_DIM:(g + 1) * GQA_HEAD_DIM])
        kmax_ref[0:1, :] = _max_row_norm(k, lane_s < GQA_HEAD_DIM)
        kmax_ref[1:2, :] = _max_row_norm(k, lane_s >= GQA_HEAD_DIM)

    def scores(pt_ref):
        lane = lax.broadcasted_iota(jnp.int32, (tq, LANES), 1)
        low = lane < GQA_HEAD_DIM
        q_aug = [[], []]
        for j in range(pairs):
            q = q_ref[:, j * LANES:(j + 1) * LANES]
            zero = jnp.zeros_like(q)
            for half in range(GQA_KV_HEADS):
                qm = jnp.where(low, q, zero) if half == 0 else jnp.where(low, zero, q)
                b = _row_norm(qm) * kmax_ref[half:half + 1, 0:1]
                bias = jnp.where(lane == 0, -b, 0.0).astype(BF16)
                q_aug[half].append(jnp.concatenate([qm, bias], axis=1))
        pt_ref[...] = jnp.exp2(_dot_nt(kaug_ref[...], jnp.concatenate(q_aug[0] + q_aug[1], axis=0))).astype(BF16)

    def values(pt_ref):
        outs, lmin = [], None
        for half in range(GQA_KV_HEADS):
            ot = _dot(vt_ref[half], pt_ref[:, half * n:(half + 1) * n])
            l = ot[GQA_HEAD_DIM:GQA_HEAD_DIM + 1, :]
            outs.append(ot[0:GQA_HEAD_DIM, :] / l)
            lmin = l if lmin is None else jnp.minimum(lmin, l)
        for j in range(pairs):
            ot = jnp.concatenate([o[:, j * tq:(j + 1) * tq] for o in outs], axis=0)
            o_ref[:, j * LANES:(j + 1) * LANES] = jnp.transpose(ot).astype(BF16)
        return jnp.min(lmin)

    _run_attention(pipelined, init, scores, values, lambda: _gqa_exact(qprev_ref, k_ref, v_ref, o_ref),
                   pt_refs, lmin_ref)


def _attn_call(kernel, q, k, v, out_width, tq, heads, scratch, name, pipelined):
    B, S, _ = q.shape
    n_tiles = S // tq
    pt = pltpu.VMEM((S, heads * tq), BF16)
    kv_specs = [pl.BlockSpec((None, S, k.shape[2]), lambda b, i: (b, 0, 0)),
                pl.BlockSpec((None, S, v.shape[2]), lambda b, i: (b, 0, 0))]
    scratch = scratch + [pltpu.VMEM((8, LANES), F32)]
    if pipelined:
        cur = lambda b, i: (b, jnp.minimum(i, n_tiles - 1), 0)
        prev = lambda b, i: (b, jnp.maximum(i - 1, 0), 0)
        grid = (B, n_tiles + 1)
        q_specs = [pl.BlockSpec((None, tq, q.shape[2]), cur), pl.BlockSpec((None, tq, q.shape[2]), prev)]
        out_spec = pl.BlockSpec((None, tq, out_width), prev)
        scratch = scratch + [pt, pt, pltpu.SMEM((1,), F32)]
    else:
        tile = lambda b, i: (b, i, 0)
        grid = (B, n_tiles)
        q_specs = [pl.BlockSpec((None, tq, q.shape[2]), tile)]
        out_spec = pl.BlockSpec((None, tq, out_width), tile)
        scratch = scratch + [pt]
    return pl.pallas_call(
        functools.partial(kernel, pipelined=pipelined),
        grid=grid,
        in_specs=q_specs + kv_specs,
        out_specs=out_spec,
        out_shape=jax.ShapeDtypeStruct((B, S, out_width), BF16),
        scratch_shapes=scratch,
        compiler_params=pltpu.CompilerParams(dimension_semantics=("parallel", "arbitrary"),
                                             vmem_limit_bytes=VMEM_LIMIT),
        name=name,
    )(*([q] * len(q_specs)), k, v)


def _post_kernel(x_ref, mod_ref, ona_ref, omla_ref, ogqa_ref, gate_ref, wa_ref, wb_ref, wc_ref, wo_ref,
                 g_ref, wgu_ref, wd_ref, fg_ref, o_ref, *, final_norm):
    D = D_MODEL
    m = (gate_ref[:, 0:D].astype(F32) * _dot(ona_ref[...], wa_ref[...])
         + gate_ref[:, D:2 * D].astype(F32) * _dot(omla_ref[...], wb_ref[...])
         + gate_ref[:, 2 * D:3 * D].astype(F32) * _dot(ogqa_ref[...], wc_ref[...]))
    x = x_ref[...] + mod_ref[2:3, :] * _dot(m.astype(BF16), wo_ref[...])

    h = _rms(x) * g_ref[...] * (1.0 + mod_ref[4:5, :]) + mod_ref[3:4, :]
    hb = h.astype(BF16)
    acc = jnp.zeros(x.shape, F32)
    for c0, c1 in FFN_CHUNKS:
        gate = _dot(hb, wgu_ref[:, c0:c1])
        up = _dot(hb, wgu_ref[:, FFN_HIDDEN + c0:FFN_HIDDEN + c1])
        act = (gate * _sigmoid(gate) * up).astype(BF16)
        acc = acc + _dot(act, wd_ref[c0:c1, :])
    y = x + mod_ref[5:6, :] * acc
    if final_norm:
        y = _rms(y) * fg_ref[...]
    o_ref[...] = y


def _post_call(x, mod, l, ona, omla, ogqa, gates, wa, wb, wc, wo, g, wgu, wd, fg, final_norm):
    B, S, D = x.shape
    tm = TOKEN_TILE
    tok = lambda a: pl.BlockSpec((None, tm, a.shape[2]), lambda b, i: (b, i, 0))
    weights = (wa, wb, wc, wo, g, wgu, wd)
    return pl.pallas_call(
        functools.partial(_post_kernel, final_norm=final_norm),
        grid=(B, S // tm),
        in_specs=[tok(x), _mod_spec(mod, l), tok(ona), tok(omla), tok(ogqa), tok(gates)]
        + [_layer_spec(a, l, single_buffer=True) for a in weights]
        + [pl.BlockSpec(fg.shape, lambda b, i: (0, 0))],
        out_specs=tok(x),
        out_shape=jax.ShapeDtypeStruct(x.shape, F32),
        compiler_params=pltpu.CompilerParams(dimension_semantics=("parallel", "parallel"),
                                             vmem_limit_bytes=VMEM_LIMIT),
        name="merge_swiglu",
    )(x, mod, ona, omla, ogqa, gates, *weights, fg)


def _rope_tables(S):
    f32 = np.float32
    t = np.arange(S, dtype=np.int32)
    pos_t = t.astype(f32)
    pos_row = (t // GRID_W).astype(f32)
    pos_col = (t % GRID_W).astype(f32)
    half = 16
    lane = np.arange(LANES)

    def tables(pos_of_lane, roped, first_half, inv_freq):
        ang = (pos_of_lane * inv_freq[lane % half][None, :]).astype(f32)
        cos, sin = np.cos(ang), np.sin(ang)
        c = np.where(roped[None, :], cos, f32(1.0))
        up = np.where((roped & first_half)[None, :], -sin, f32(0.0))
        dn = np.where((roped & ~first_half)[None, :], sin, f32(0.0))
        return tuple(jnp.asarray(a, F32) for a in (c, up, dn))

    inv_t = (f32(1.0) / (f32(ROPE_THETA) ** (np.arange(half, dtype=f32) / f32(half)))).astype(f32)
    inv_a = (f32(1.0) / (f32(AXIAL_THETA) ** (np.arange(half, dtype=f32) / f32(half)))).astype(f32)
    m_roped = (lane >= MLA_NOPE) & (lane < MLA_NOPE + MLA_ROPE)
    m_first = ((lane - MLA_NOPE) % 32) < half
    mt = tables(np.broadcast_to(pos_t[:, None], (S, LANES)), m_roped, m_first, inv_t)
    a_first = (lane % 32) < half
    use_row = (lane % GRID_W) < 32
    pos_rc = np.where(use_row[None, :], pos_row[:, None], pos_col[:, None])
    at = tables(pos_rc, np.ones(LANES, bool), a_first, inv_a)
    return mt, at


def _na_bias_kernel(ext_ref, o_ref):
    W = GRID_W
    n_off = 2 * NA_KH - 1
    col = lax.broadcasted_iota(jnp.int32, (W, LANES), 0)
    lane = lax.broadcasted_iota(jnp.int32, (W, LANES), 1)
    kc = lane % W
    ws = jnp.clip(col - NA_KW // 2, 0, W - NA_KW)
    valid = (kc >= ws) & (kc < ws + NA_KW)
    low = lane < W
    for h in range(NA_HEADS):
        lo, hi = [], []
        for ro in range(n_off):
            e = jnp.broadcast_to(ext_ref[h * n_off + ro:h * n_off + ro + 1, :], (W, LANES))
            lo.append(pltpu.roll(e, W + 1, 1, stride=1, stride_axis=0))
            hi.append(pltpu.roll(e, 1, 1, stride=1, stride_axis=0))
        for d in range(NA_KH):
            for p in range(NA_KH // 2):
                ro = 2 * p - d + NA_KH - 1
                blk = jnp.where(low, lo[ro], hi[ro + 1])
                o_ref[d, h * W:(h + 1) * W, p * LANES:(p + 1) * LANES] = jnp.where(valid, blk * LOG2E, MASK_VALUE)


def _na_bias_table(rpb):
    L, W = rpb.shape[0], GRID_W
    pad = W - NA_KW
    ext = jnp.concatenate([jnp.repeat(rpb[..., :1], pad, -1), rpb, jnp.repeat(rpb[..., -1:], pad + 1, -1)], -1)
    ext = ext.reshape(L, NA_HEADS * (2 * NA_KH - 1), 2 * W).astype(F32)
    return pl.pallas_call(
        _na_bias_kernel,
        grid=(L,),
        in_specs=[pl.BlockSpec((None,) + ext.shape[1:], lambda l: (l, 0, 0))],
        out_specs=pl.BlockSpec((None, NA_KH, NA_HEADS * W, NA_KH * W), lambda l: (l, 0, 0, 0)),
        out_shape=jax.ShapeDtypeStruct((L, NA_KH, NA_HEADS * W, NA_KH * W), F32),
        compiler_params=pltpu.CompilerParams(dimension_semantics=("parallel",), vmem_limit_bytes=VMEM_LIMIT),
        name="na_bias",
    )(ext)


W_IN_KR = 1152
W_IN_QC = 1184
W_IN_COLS = 5024


def _w_in_prep_kernel(w_ref, o_ref):
    tk = w_ref.shape[1]
    hd, pairs = GQA_HEAD_DIM, GQA_HEADS // GQA_KV_HEADS
    kr0 = C_MLA + 384 + MLA_NOPE
    o_ref[0:W_IN_KR, :] = w_ref[0:W_IN_KR, :].astype(BF16)
    o_ref[C_MLA + 384:kr0, :] = jnp.zeros((MLA_NOPE, tk), BF16)
    o_ref[kr0:kr0 + MLA_ROPE, :] = w_ref[W_IN_KR:W_IN_QC, :].astype(BF16)
    o_ref[kr0 + MLA_ROPE:C_GQA, :] = jnp.zeros((LANES - MLA_NOPE - MLA_ROPE, tk), BF16)
    for j in range(pairs):
        for half, head in enumerate((j, pairs + j)):
            dst = C_GQA + LANES * j + hd * half
            o_ref[dst:dst + hd, :] = w_ref[W_IN_QC + hd * head:W_IN_QC + hd * (head + 1), :].astype(BF16)
    o_ref[C_GQA + 512:C_TOTAL, :] = w_ref[W_IN_QC + 512:W_IN_COLS, :].astype(BF16)


def _prep_w_in(w):
    L, D, N = w.shape
    tk = 256
    return pl.pallas_call(
        _w_in_prep_kernel,
        grid=(L, D // tk),
        in_specs=[pl.BlockSpec((None, N, tk), lambda l, i: (l, 0, i))],
        out_specs=pl.BlockSpec((None, C_TOTAL, tk), lambda l, i: (l, 0, i)),
        out_shape=jax.ShapeDtypeStruct((L, C_TOTAL, D), BF16),
        compiler_params=pltpu.CompilerParams(dimension_semantics=("parallel", "parallel"),
                                             vmem_limit_bytes=VMEM_LIMIT),
        name="w_in_prep",
    )(jnp.swapaxes(w, 1, 2))


def _prep_w_uq(w):
    z = jnp.zeros(w.shape[:-1] + (32,), BF16)
    hd = MLA_NOPE + MLA_ROPE
    return jnp.concatenate([a for h in range(MLA_HEADS) for a in (w[..., hd * h:hd * (h + 1)].astype(BF16), z)],
                           axis=-1)


def _prep_w_ukv(w):
    z = jnp.zeros(w.shape[:-1] + (64,), BF16)
    hd = MLA_NOPE + MLA_V
    ks = [a for h in range(MLA_HEADS) for a in (w[..., hd * h:hd * h + MLA_NOPE].astype(BF16), z)]
    vs = [w[..., hd * h + MLA_NOPE:hd * (h + 1)].astype(BF16) for h in range(MLA_HEADS)]
    return jnp.concatenate(ks + vs, axis=-1)


def _prep_w_br_gqa(w):
    rows = []
    for j in range(4):
        rows += [w[:, 64 * j:64 * (j + 1)], w[:, 64 * (4 + j):64 * (5 + j)]]
    return jnp.concatenate([r.astype(BF16) for r in rows], axis=1)


def kernel(x, c, ada_w, ada_b, norm_mix_g, norm_ffn_g, w_in, na_rpb, mla_q_norm_g, mla_kv_norm_g, mla_w_uq,
           mla_w_ukv, gqa_q_norm_g, gqa_k_norm_g, w_br_na, w_br_mla, w_br_gqa, w_out, ffn_w_gate_up,
           ffn_w_down, final_norm_g):
    B, S, D = x.shape
    L = ada_w.shape[0]
    mod = _ada_call(c, ada_w, ada_b).reshape(L, B, N_MOD, D)
    mtabs, atabs = _rope_tables(S)
    lane = np.arange(LANES)
    bd = jnp.asarray(np.tile((lane[:, None] // 64 == lane[None, :] // 64) / 64.0, (2, 1)), BF16)
    row = lambda a: a.reshape(L, 1, -1)
    gqk = jnp.concatenate([jnp.tile(gqa_q_norm_g, (1, GQA_HEADS)), jnp.tile(gqa_k_norm_g, (1, GQA_KV_HEADS))], 1)
    w_in_p, w_uq_p, w_ukv_p = _prep_w_in(w_in), _prep_w_uq(mla_w_uq), _prep_w_ukv(mla_w_ukv)
    bias = _na_bias_table(na_rpb)
    wa, wb, wc, wo = w_br_na.astype(BF16), w_br_mla.astype(BF16), _prep_w_br_gqa(w_br_gqa), w_out.astype(BF16)
    wgu, wd = ffn_w_gate_up.astype(BF16), ffn_w_down.astype(BF16)
    fg = final_norm_g.reshape(1, D)
    for l in range(L):
        qna, kna, vna, qm, km, vm, qg, kg, vg, gates = _in_call(
            x, mod, l, row(norm_mix_g), w_in_p, row(mla_q_norm_g), row(mla_kv_norm_g), w_uq_p, w_ukv_p,
            row(gqk), bd, mtabs, atabs)
        o_na = _na_call(qna, kna, vna, bias, l)
        o_mla = _attn_call(_mla_attn_kernel, qm, km, vm, MLA_HEADS * MLA_V, MLA_Q_TILE, MLA_HEADS,
                           [pltpu.VMEM((S, MLA_HEADS * LANES), BF16),
                            pltpu.VMEM((MLA_HEADS, VT_ROWS, S), BF16)], "mla_attn", pipelined=False)
        o_gqa = _attn_call(_gqa_attn_kernel, qg, kg, vg, GQA_HEADS * GQA_HEAD_DIM, GQA_Q_TILE, GQA_HEADS,
                           [pltpu.VMEM((S, 2 * LANES), BF16), pltpu.VMEM((GQA_KV_HEADS, VT_ROWS, S), BF16)],
                           "gqa_attn", pipelined=True)
        x = _post_call(x, mod, l, o_na, o_mla, o_gqa, gates, wa, wb, wc, wo, row(norm_ffn_g), wgu, wd, fg,
                       final_norm=(l == L - 1))
    return x
```

```python
import functools

import numpy as np
import jax
import jax.numpy as jnp
from jax import lax
from jax.experimental import pallas as pl
from jax.experimental.pallas import tpu as pltpu

D_MODEL = 1024
GRID_W = 64
NA_HEADS = 4
NA_HEAD_DIM = 64
NA_KH = 8
NA_KW = 16
MLA_HEADS = 4
MLA_Q_LORA = 256
MLA_KV_LORA = 128
MLA_NOPE = 64
MLA_ROPE = 32
MLA_V = 64
GQA_HEADS = 8
GQA_KV_HEADS = 2
GQA_HEAD_DIM = 64
FFN_HIDDEN = 2816
ROPE_THETA = 10000.0
AXIAL_THETA = 10000.0
NORM_EPS = 1e-6
N_MOD = 6
MASK_VALUE = -1e30
LOG2E = 1.4426950408889634

LANES = 128
VMEM_LIMIT = 56 * 1024 * 1024
TOKEN_TILE = 512
IN_TOKEN_TILE = 1024
MXU_TILE = 256
ADA_COL_TILE = 3072
W_IN_PREP_TILE = 256
MLA_Q_TILE = 512
GQA_Q_TILE = 256
_FFN_SPLIT = (FFN_HIDDEN // MXU_TILE + 1) // 2 * MXU_TILE
FFN_CHUNKS = ((0, _FFN_SPLIT), (_FFN_SPLIT, FFN_HIDDEN))

C_NA = 0
C_MLA = 768
C_GQA = 1280
C_GATE = 2048
C_TOTAL = 5120

F32 = jnp.float32
BF16 = jnp.bfloat16


def _dot(a, b):
    return jnp.dot(a, b, preferred_element_type=F32)


def _dot_nt(a, b):
    return lax.dot_general(a, b, (((1,), (1,)), ((), ())), preferred_element_type=F32)


def _rms(x):
    return x * lax.rsqrt(jnp.mean(x * x, axis=-1, keepdims=True) + NORM_EPS)


def _rope_slab(x, c, s_up, s_dn):
    return x * c + pltpu.roll(x, LANES - 16, 1) * s_up + pltpu.roll(x, 16, 1) * s_dn


def _sigmoid(x):
    return 1.0 / (1.0 + jnp.exp(-x))


def _layer_spec(a, l, single_buffer=False):
    mode = pl.Buffered(1) if single_buffer else None
    return pl.BlockSpec((None,) + a.shape[1:], lambda *_: (l,) + (0,) * (a.ndim - 1), pipeline_mode=mode)


def _mod_spec(mod, l):
    return pl.BlockSpec((None, None) + mod.shape[2:], lambda b, *_: (l, b, 0, 0))


def _ada_kernel(c_ref, w_ref, b_ref, o_ref):
    c = c_ref[...]
    s = (c * _sigmoid(c)).astype(BF16)
    o_ref[...] = _dot(s, w_ref[...].astype(BF16)) + b_ref[...]


def _ada_call(c, ada_w, ada_b):
    L, D, N = ada_w.shape
    B = c.shape[0]
    tn = ADA_COL_TILE
    return pl.pallas_call(
        _ada_kernel,
        grid=(L, N // tn),
        in_specs=[pl.BlockSpec((B, D), lambda l, j: (0, 0)),
                  pl.BlockSpec((None, D, tn), lambda l, j: (l, 0, j)),
                  pl.BlockSpec((None, 1, tn), lambda l, j: (l, 0, j))],
        out_specs=pl.BlockSpec((None, B, tn), lambda l, j: (l, 0, j)),
        out_shape=jax.ShapeDtypeStruct((L, B, N), F32),
        compiler_params=pltpu.CompilerParams(dimension_semantics=("parallel", "parallel"),
                                             vmem_limit_bytes=VMEM_LIMIT),
        name="ada_mod",
    )(c, ada_w, ada_b.reshape(L, 1, N))


def _in_kernel(x_ref, mod_ref, g_ref, w_ref, qng_ref, kvng_ref, wuq_ref, wukv_ref, gqk_ref, bd_ref,
               mc_ref, mu_ref, md_ref, ac_ref, au_ref, ad_ref,
               qna_ref, kna_ref, vna_ref, qm_ref, km_ref, vm_ref, qg_ref, kg_ref, vg_ref, gate_ref):
    x = x_ref[...]
    h = _rms(x) * g_ref[...] * (1.0 + mod_ref[1:2, :]) + mod_ref[0:1, :]
    hb = h.astype(BF16)

    def proj(a, b):
        return _dot_nt(hb, w_ref[a:b, :])

    p = proj(C_NA, C_NA + 768)
    qna_ref[...] = (p[:, :256] * (NA_HEAD_DIM ** -0.5 * LOG2E)).astype(BF16)
    kna_ref[...] = p[:, 256:512].astype(BF16)
    vna_ref[...] = p[:, 512:768].astype(BF16)

    p = proj(C_MLA, C_MLA + 512)
    mc, mu, md = mc_ref[...], mu_ref[...], md_ref[...]
    nq = (_rms(p[:, :256]) * qng_ref[...]).astype(BF16)
    qb = _dot(nq, wuq_ref[...])
    mla_scale = (MLA_NOPE + MLA_ROPE) ** -0.5 * LOG2E
    for s in range(MLA_HEADS):
        sl = slice(s * LANES, (s + 1) * LANES)
        qm_ref[:, sl] = (_rope_slab(qb[:, sl], mc, mu, md) * mla_scale).astype(BF16)
    nkv = (_rms(p[:, 256:384]) * kvng_ref[...]).astype(BF16)
    kvb = _dot(nkv, wukv_ref[...])
    kpe = _rope_slab(p[:, 384:512], mc, mu, md)
    for s in range(MLA_HEADS):
        sl = slice(s * LANES, (s + 1) * LANES)
        km_ref[:, sl] = (kvb[:, sl] + kpe).astype(BF16)
    vm_ref[...] = kvb[:, 512:768].astype(BF16)

    p = proj(C_GQA, C_GQA + 768)
    ac, au, ad = ac_ref[...], au_ref[...], ad_ref[...]
    bd = bd_ref[...]
    for s in range(5):
        sl = slice(s * LANES, (s + 1) * LANES)
        v = p[:, sl]
        sq = v * v
        hi = sq.astype(BF16)
        lo = (sq - hi.astype(F32)).astype(BF16)
        msq = _dot(jnp.concatenate([hi, lo], axis=1), bd)
        y = v * lax.rsqrt(msq + NORM_EPS) * gqk_ref[:, sl]
        r = _rope_slab(y, ac, au, ad)
        if s < 4:
            qg_ref[:, sl] = (r * (GQA_HEAD_DIM ** -0.5 * LOG2E)).astype(BF16)
        else:
            kg_ref[...] = r.astype(BF16)
    vg_ref[...] = p[:, 640:768].astype(BF16)

    for j in range(3):
        g = proj(C_GATE + j * D_MODEL, C_GATE + (j + 1) * D_MODEL)
        gate_ref[:, j * D_MODEL:(j + 1) * D_MODEL] = _sigmoid(g).astype(BF16)


def _in_call(x, mod, l, g, w, qng, kvng, wuq, wukv, gqk, bd, mtabs, atabs):
    B, S, D = x.shape
    tm = IN_TOKEN_TILE
    tok = lambda w_: pl.BlockSpec((None, tm, w_), lambda b, i: (b, i, 0))
    full = lambda a: pl.BlockSpec(a.shape, lambda b, i: (0,) * a.ndim)
    tab = pl.BlockSpec((tm, LANES), lambda b, i: (i, 0))
    widths = (256, 256, 256, 512, 512, 256, 512, 128, 128, 3 * D)
    return pl.pallas_call(
        _in_kernel,
        grid=(B, S // tm),
        in_specs=[tok(D), _mod_spec(mod, l)]
        + [_layer_spec(a, l, single_buffer=True) for a in (g, w, qng, kvng, wuq, wukv, gqk)]
        + [full(bd)] + [tab] * 6,
        out_specs=[tok(w_) for w_ in widths],
        out_shape=[jax.ShapeDtypeStruct((B, S, w_), BF16) for w_ in widths],
        compiler_params=pltpu.CompilerParams(dimension_semantics=("parallel", "parallel"),
                                             vmem_limit_bytes=VMEM_LIMIT),
        name="in_proj",
    )(x, mod, g, w, qng, kvng, wuq, wukv, gqk, bd, *mtabs, *atabs)


def _na_kernel(q_ref, k_ref, v_ref, bias_ref, o_ref, *, rows):
    kh = NA_KH
    head_of_lane = lax.broadcasted_iota(jnp.int32, (GRID_W, NA_HEADS * NA_HEAD_DIM), 1) // NA_HEAD_DIM

    def body(r, carry):
        rs = jnp.clip(r - kh // 2, 0, rows - kh)
        q = q_ref[pl.ds(pl.multiple_of(r * GRID_W, GRID_W), GRID_W), :]
        kstart = pl.multiple_of(rs * GRID_W, GRID_W)
        kb = k_ref[pl.ds(kstart, kh * GRID_W), :]
        vb = v_ref[pl.ds(kstart, kh * GRID_W), :]
        qs = jnp.concatenate([jnp.where(head_of_lane == h, q, jnp.zeros_like(q)) for h in range(NA_HEADS)],
                             axis=0)
        s = _dot_nt(qs, kb) + bias_ref[r - rs]
        m = jnp.max(s, axis=-1, keepdims=True)
        p = jnp.exp2(s - m)
        l = jnp.sum(p, axis=-1, keepdims=True)
        o = _dot(p.astype(BF16), vb) / l
        out = jnp.zeros((GRID_W, NA_HEADS * NA_HEAD_DIM), F32)
        for h in range(NA_HEADS):
            out = jnp.where(head_of_lane == h, o[h * GRID_W:(h + 1) * GRID_W, :], out)
        o_ref[pl.ds(pl.multiple_of(r * GRID_W, GRID_W), GRID_W), :] = out.astype(BF16)
        return carry

    lax.fori_loop(0, rows, body, 0, unroll=32)


def _na_call(q, k, v, bias, l):
    B, S, W = q.shape
    rows = S // GRID_W
    blk = pl.BlockSpec((None, S, W), lambda b: (b, 0, 0))
    return pl.pallas_call(
        functools.partial(_na_kernel, rows=rows),
        grid=(B,),
        in_specs=[blk, blk, blk, _layer_spec(bias, l)],
        out_specs=blk,
        out_shape=jax.ShapeDtypeStruct((B, S, W), BF16),
        compiler_params=pltpu.CompilerParams(dimension_semantics=("parallel",),
                                             vmem_limit_bytes=VMEM_LIMIT),
        name="na_attn",
    )(q, k, v, bias)


L_MIN = 2.0 ** -60
MLA_BIAS_LANE = MLA_NOPE + MLA_ROPE


def _softmax_pv(q, k, v):
    s = _dot_nt(q, k)
    m = jnp.max(s, axis=-1, keepdims=True)
    p = jnp.exp2(s - m)
    l = jnp.sum(p, axis=-1, keepdims=True)
    return _dot(p.astype(BF16), v) / l


VT_ROWS = 80


def _store_vt(vt_ref, idx, vt_rows):
    hd = VT_ROWS - 16
    S = vt_rows.shape[1]
    vt_ref[idx, 0:hd, :] = vt_rows.astype(BF16)
    first = lax.broadcasted_iota(jnp.int32, (16, S), 0) == 0
    vt_ref[idx, hd:VT_ROWS, :] = jnp.where(first, 1.0, 0.0).astype(BF16)


def _row_norm(x):
    xf = x.astype(F32)
    return jnp.sqrt(jnp.sum(xf * xf, axis=-1, keepdims=True))


def _max_row_norm(x, mask=None):
    xf = x.astype(F32)
    sq = xf * xf if mask is None else jnp.where(mask, xf * xf, 0.0)
    n = jnp.max(jnp.sum(sq, axis=-1, keepdims=True), axis=0, keepdims=True)
    return jnp.broadcast_to(jnp.sqrt(n), (1, LANES))


def _mla_exact(q_ref, k_ref, v_ref, o_ref):
    v = v_ref[...]
    head_of_lane = lax.broadcasted_iota(jnp.int32, o_ref.shape, 1) // MLA_V
    out = jnp.zeros(o_ref.shape, F32)
    for h in range(MLA_HEADS):
        sl = slice(h * LANES, (h + 1) * LANES)
        out = jnp.where(head_of_lane == h, _softmax_pv(q_ref[:, sl], k_ref[:, sl], v), out)
    o_ref[...] = out.astype(BF16)


def _pipelined_attention(init, scores, values, exact, pt_refs, lmin_ref):
    i = pl.program_id(1)
    n_tiles = pl.num_programs(1) - 1

    @pl.when(i == 0)
    def _():
        init()
        lmin_ref[0] = 1.0
        scores(pt_refs[0])

    for parity in (0, 1):
        @pl.when((i > 0) & (i < n_tiles) & (i % 2 == parity))
        def _():
            lmin_ref[0] = values(pt_refs[1 - parity])
            scores(pt_refs[parity])

    for parity in (0, 1):
        @pl.when((i == n_tiles) & (i % 2 == parity))
        def _():
            lmin_ref[0] = values(pt_refs[1 - parity])

    @pl.when((i > 0) & (lmin_ref[0] < L_MIN))
    def _():
        exact()


def _serial_attention(init, scores, values, exact, pt_ref):
    @pl.when(pl.program_id(1) == 0)
    def _():
        init()

    scores(pt_ref)
    lmin = values(pt_ref)

    @pl.when(lmin < L_MIN)
    def _():
        exact()


def _unpack_attn_refs(refs, pipelined):
    if pipelined:
        q, qprev, k, v, o, kaug, vt, kmax, pta, ptb, lmin = refs
        return q, qprev, k, v, o, kaug, vt, kmax, (pta, ptb), lmin
    q, k, v, o, kaug, vt, kmax, pt = refs
    return q, q, k, v, o, kaug, vt, kmax, (pt,), None


def _run_attention(pipelined, init, scores, values, exact, pt_refs, lmin_ref):
    if pipelined:
        _pipelined_attention(init, scores, values, exact, pt_refs, lmin_ref)
    else:
        _serial_attention(init, scores, values, exact, pt_refs[0])


def _mla_attn_kernel(*refs, pipelined):
    q_ref, qprev_ref, k_ref, v_ref, o_ref, kaug_ref, vt_ref, kmax_ref, pt_refs, lmin_ref = _unpack_attn_refs(
        refs, pipelined)
    S, tq = k_ref.shape[0], q_ref.shape[0]

    def init():
        lane_s = lax.broadcasted_iota(jnp.int32, (S, LANES), 1)
        for h in range(MLA_HEADS):
            sl = slice(h * LANES, (h + 1) * LANES)
            k = k_ref[:, sl]
            kaug_ref[:, sl] = jnp.where(lane_s == MLA_BIAS_LANE, jnp.ones_like(k), k)
            kmax_ref[h:h + 1, :] = _max_row_norm(k)
        for p in range(MLA_HEADS // 2):
            vt = jnp.transpose(v_ref[:, p * LANES:(p + 1) * LANES].astype(F32))
            _store_vt(vt_ref, 2 * p, vt[0:MLA_V])
            _store_vt(vt_ref, 2 * p + 1, vt[MLA_V:2 * MLA_V])

    def scores(pt_ref):
        lane = lax.broadcasted_iota(jnp.int32, (tq, LANES), 1)
        for h in range(MLA_HEADS):
            sl = slice(h * LANES, (h + 1) * LANES)
            q = q_ref[:, sl]
            b = _row_norm(q) * kmax_ref[h:h + 1, 0:1]
            q_aug = jnp.where(lane == MLA_BIAS_LANE, (-b).astype(BF16), q)
            pt_ref[:, h * tq:(h + 1) * tq] = jnp.exp2(_dot_nt(kaug_ref[:, sl], q_aug)).astype(BF16)

    def values(pt_ref):
        lmin = None
        for p in range(MLA_HEADS // 2):
            outs = []
            for h in (2 * p, 2 * p + 1):
                ot = _dot(vt_ref[h], pt_ref[:, h * tq:(h + 1) * tq])
                l = ot[MLA_V:MLA_V + 1, :]
                outs.append(ot[0:MLA_V, :] / l)
                lmin = l if lmin is None else jnp.minimum(lmin, l)
            o_ref[:, p * LANES:(p + 1) * LANES] = jnp.transpose(jnp.concatenate(outs, axis=0)).astype(BF16)
        return jnp.min(lmin)

    _run_attention(pipelined, init, scores, values, lambda: _mla_exact(qprev_ref, k_ref, v_ref, o_ref),
                   pt_refs, lmin_ref)


def _gqa_exact(q_ref, k_ref, v_ref, o_ref):
    k = k_ref[...]
    v = v_ref[...]
    low = lax.broadcasted_iota(jnp.int32, (q_ref.shape[0], LANES), 1) < GQA_HEAD_DIM
    for j in range(GQA_HEADS // GQA_KV_HEADS):
        sl = slice(j * LANES, (j + 1) * LANES)
        q = q_ref[:, sl]
        zero = jnp.zeros_like(q)
        o_lo = _softmax_pv(jnp.where(low, q, zero), k, v)
        o_hi = _softmax_pv(jnp.where(low, zero, q), k, v)
        o_ref[:, sl] = jnp.where(low, o_lo, o_hi).astype(BF16)


def _gqa_attn_kernel(*refs, pipelined):
    q_ref, qprev_ref, k_ref, v_ref, o_ref, kaug_ref, vt_ref, kmax_ref, pt_refs, lmin_ref = _unpack_attn_refs(
        refs, pipelined)
    S, tq = k_ref.shape[0], q_ref.shape[0]
    pairs = GQA_HEADS // GQA_KV_HEADS
    n = pairs * tq

    def init():
        lane_s = lax.broadcasted_iota(jnp.int32, (S, LANES), 1)
        k = k_ref[...]
        kaug_ref[:, :LANES] = k
        kaug_ref[:, LANES:] = jnp.where(lane_s == 0, 1.0, 0.0).astype(BF16)
        vt = jnp.transpose(v_ref[...].astype(F32))
        for g in range(GQA_KV_HEADS):
            _store_vt(vt_ref, g, vt[g * GQA_HEAD_DIM:(g + 1) * GQA_HEAD_DIM])
        kmax_ref[0:1, :] = _max_row_norm(k, lane_s < GQA_HEAD_DIM)
        kmax_ref[1:2, :] = _max_row_norm(k, lane_s >= GQA_HEAD_DIM)

    def scores(pt_ref):
        lane = lax.broadcasted_iota(jnp.int32, (tq, LANES), 1)
        low = lane < GQA_HEAD_DIM
        q_aug = [[], []]
        for j in range(pairs):
            q = q_ref[:, j * LANES:(j + 1) * LANES]
            zero = jnp.zeros_like(q)
            for half in range(GQA_KV_HEADS):
                qm = jnp.where(low, q, zero) if half == 0 else jnp.where(low, zero, q)
                b = _row_norm(qm) * kmax_ref[half:half + 1, 0:1]
                bias = jnp.where(lane == 0, -b, 0.0).astype(BF16)
                q_aug[half].append(jnp.concatenate([qm, bias], axis=1))
        pt_ref[...] = jnp.exp2(_dot_nt(kaug_ref[...], jnp.concatenate(q_aug[0] + q_aug[1], axis=0))).astype(BF16)

    def values(pt_ref):
        outs, lmin = [], None
        for half in range(GQA_KV_HEADS):
            ot = _dot(vt_ref[half], pt_ref[:, half * n:(half + 1) * n])
            l = ot[GQA_HEAD_DIM:GQA_HEAD_DIM + 1, :]
            outs.append(ot[0:GQA_HEAD_DIM, :] / l)
            lmin = l if lmin is None else jnp.minimum(lmin, l)
        for j in range(pairs):
            ot = jnp.concatenate([o[:, j * tq:(j + 1) * tq] for o in outs], axis=0)
            o_ref[:, j * LANES:(j + 1) * LANES] = jnp.transpose(ot).astype(BF16)
        return jnp.min(lmin)

    _run_attention(pipelined, init, scores, values, lambda: _gqa_exact(qprev_ref, k_ref, v_ref, o_ref),
                   pt_refs, lmin_ref)


def _attn_call(kernel, q, k, v, out_width, tq, heads, scratch, name, pipelined):
    B, S, _ = q.shape
    n_tiles = S // tq
    pt = pltpu.VMEM((S, heads * tq), BF16)
    kv_specs = [pl.BlockSpec((None, S, k.shape[2]), lambda b, i: (b, 0, 0)),
                pl.BlockSpec((None, S, v.shape[2]), lambda b, i: (b, 0, 0))]
    scratch = scratch + [pltpu.VMEM((8, LANES), F32)]
    if pipelined:
        cur = lambda b, i: (b, jnp.minimum(i, n_tiles - 1), 0)
        prev = lambda b, i: (b, jnp.maximum(i - 1, 0), 0)
        grid = (B, n_tiles + 1)
        q_specs = [pl.BlockSpec((None, tq, q.shape[2]), cur), pl.BlockSpec((None, tq, q.shape[2]), prev)]
        out_spec = pl.BlockSpec((None, tq, out_width), prev)
        scratch = scratch + [pt, pt, pltpu.SMEM((1,), F32)]
    else:
        tile = lambda b, i: (b, i, 0)
        grid = (B, n_tiles)
        q_specs = [pl.BlockSpec((None, tq, q.shape[2]), tile)]
        out_spec = pl.BlockSpec((None, tq, out_width), tile)
        scratch = scratch + [pt]
    return pl.pallas_call(
        functools.partial(kernel, pipelined=pipelined),
        grid=grid,
        in_specs=q_specs + kv_specs,
        out_specs=out_spec,
        out_shape=jax.ShapeDtypeStruct((B, S, out_width), BF16),
        scratch_shapes=scratch,
        compiler_params=pltpu.CompilerParams(dimension_semantics=("parallel", "arbitrary"),
                                             vmem_limit_bytes=VMEM_LIMIT),
        name=name,
    )(*([q] * len(q_specs)), k, v)


def _post_kernel(x_ref, mod_ref, ona_ref, omla_ref, ogqa_ref, gate_ref, wa_ref, wb_ref, wc_ref, wo_ref,
                 g_ref, wgu_ref, wd_ref, fg_ref, o_ref, *, final_norm):
    D = D_MODEL
    m = (gate_ref[:, 0:D].astype(F32) * _dot(ona_ref[...], wa_ref[...])
         + gate_ref[:, D:2 * D].astype(F32) * _dot(omla_ref[...], wb_ref[...])
         + gate_ref[:, 2 * D:3 * D].astype(F32) * _dot(ogqa_ref[...], wc_ref[...]))
    x = x_ref[...] + mod_ref[2:3, :] * _dot(m.astype(BF16), wo_ref[...])

    h = _rms(x) * g_ref[...] * (1.0 + mod_ref[4:5, :]) + mod_ref[3:4, :]
    hb = h.astype(BF16)
    acc = jnp.zeros(x.shape, F32)
    for c0, c1 in FFN_CHUNKS:
        gate = _dot(hb, wgu_ref[:, c0:c1])
        up = _dot(hb, wgu_ref[:, FFN_HIDDEN + c0:FFN_HIDDEN + c1])
        act = (gate * _sigmoid(gate) * up).astype(BF16)
        acc = acc + _dot(act, wd_ref[c0:c1, :])
    y = x + mod_ref[5:6, :] * acc
    if final_norm:
        y = _rms(y) * fg_ref[...]
    o_ref[...] = y


def _post_call(x, mod, l, ona, omla, ogqa, gates, wa, wb, wc, wo, g, wgu, wd, fg, final_norm):
    B, S, D = x.shape
    tm = TOKEN_TILE
    tok = lambda a: pl.BlockSpec((None, tm, a.shape[2]), lambda b, i: (b, i, 0))
    weights = (wa, wb, wc, wo, g, wgu, wd)
    return pl.pallas_call(
        functools.partial(_post_kernel, final_norm=final_norm),
        grid=(B, S // tm),
        in_specs=[tok(x), _mod_spec(mod, l), tok(ona), tok(omla), tok(ogqa), tok(gates)]
        + [_layer_spec(a, l, single_buffer=True) for a in weights]
        + [pl.BlockSpec(fg.shape, lambda b, i: (0, 0))],
        out_specs=tok(x),
        out_shape=jax.ShapeDtypeStruct(x.shape, F32),
        compiler_params=pltpu.CompilerParams(dimension_semantics=("parallel", "parallel"),
                                             vmem_limit_bytes=VMEM_LIMIT),
        name="merge_swiglu",
    )(x, mod, ona, omla, ogqa, gates, *weights, fg)


def _rope_tables(S):
    f32 = np.float32
    t = np.arange(S, dtype=np.int32)
    pos_t = t.astype(f32)
    pos_row = (t // GRID_W).astype(f32)
    pos_col = (t % GRID_W).astype(f32)
    half = 16
    lane = np.arange(LANES)

    def tables(pos_of_lane, roped, first_half, inv_freq):
        ang = (pos_of_lane * inv_freq[lane % half][None, :]).astype(f32)
        cos, sin = np.cos(ang), np.sin(ang)
        c = np.where(roped[None, :], cos, f32(1.0))
        up = np.where((roped & first_half)[None, :], -sin, f32(0.0))
        dn = np.where((roped & ~first_half)[None, :], sin, f32(0.0))
        return tuple(jnp.asarray(a, F32) for a in (c, up, dn))

    inv_t = (f32(1.0) / (f32(ROPE_THETA) ** (np.arange(half, dtype=f32) / f32(half)))).astype(f32)
    inv_a = (f32(1.0) / (f32(AXIAL_THETA) ** (np.arange(half, dtype=f32) / f32(half)))).astype(f32)
    m_roped = (lane >= MLA_NOPE) & (lane < MLA_NOPE + MLA_ROPE)
    m_first = ((lane - MLA_NOPE) % 32) < half
    mt = tables(np.broadcast_to(pos_t[:, None], (S, LANES)), m_roped, m_first, inv_t)
    a_first = (lane % 32) < half
    use_row = (lane % GRID_W) < 32
    pos_rc = np.where(use_row[None, :], pos_row[:, None], pos_col[:, None])
    at = tables(pos_rc, np.ones(LANES, bool), a_first, inv_a)
    return mt, at


def _na_bias_kernel(ext_ref, o_ref):
    W = GRID_W
    n_off = 2 * NA_KH - 1
    col = lax.broadcasted_iota(jnp.int32, (W, LANES), 0)
    lane = lax.broadcasted_iota(jnp.int32, (W, LANES), 1)
    kc = lane % W
    ws = jnp.clip(col - NA_KW // 2, 0, W - NA_KW)
    valid = (kc >= ws) & (kc < ws + NA_KW)
    low = lane < W
    for h in range(NA_HEADS):
        lo, hi = [], []
        for ro in range(n_off):
            e = jnp.broadcast_to(ext_ref[h * n_off + ro:h * n_off + ro + 1, :], (W, LANES))
            lo.append(pltpu.roll(e, W + 1, 1, stride=1, stride_axis=0))
            hi.append(pltpu.roll(e, 1, 1, stride=1, stride_axis=0))
        for d in range(NA_KH):
            for p in range(NA_KH // 2):
                ro = 2 * p - d + NA_KH - 1
                blk = jnp.where(low, lo[ro], hi[ro + 1])
                o_ref[d, h * W:(h + 1) * W, p * LANES:(p + 1) * LANES] = jnp.where(valid, blk * LOG2E, MASK_VALUE)


def _na_bias_table(rpb):
    L, W = rpb.shape[0], GRID_W
    pad = W - NA_KW
    ext = jnp.concatenate([jnp.repeat(rpb[..., :1], pad, -1), rpb, jnp.repeat(rpb[..., -1:], pad + 1, -1)], -1)
    ext = ext.reshape(L, NA_HEADS * (2 * NA_KH - 1), 2 * W).astype(F32)
    return pl.pallas_call(
        _na_bias_kernel,
        grid=(L,),
        in_specs=[pl.BlockSpec((None,) + ext.shape[1:], lambda l: (l, 0, 0))],
        out_specs=pl.BlockSpec((None, NA_KH, NA_HEADS * W, NA_KH * W), lambda l: (l, 0, 0, 0)),
        out_shape=jax.ShapeDtypeStruct((L, NA_KH, NA_HEADS * W, NA_KH * W), F32),
        compiler_params=pltpu.CompilerParams(dimension_semantics=("parallel",), vmem_limit_bytes=VMEM_LIMIT),
        name="na_bias",
    )(ext)


W_IN_KR = 1152
W_IN_QC = 1184
W_IN_COLS = 5024


def _w_in_prep_kernel(w_ref, o_ref):
    tk = w_ref.shape[1]
    hd, pairs = GQA_HEAD_DIM, GQA_HEADS // GQA_KV_HEADS
    kr0 = C_MLA + 384 + MLA_NOPE
    o_ref[0:W_IN_KR, :] = w_ref[0:W_IN_KR, :].astype(BF16)
    o_ref[C_MLA + 384:kr0, :] = jnp.zeros((MLA_NOPE, tk), BF16)
    o_ref[kr0:kr0 + MLA_ROPE, :] = w_ref[W_IN_KR:W_IN_QC, :].astype(BF16)
    o_ref[kr0 + MLA_ROPE:C_GQA, :] = jnp.zeros((LANES - MLA_NOPE - MLA_ROPE, tk), BF16)
    for j in range(pairs):
        for half, head in enumerate((j, pairs + j)):
            dst = C_GQA + LANES * j + hd * half
            o_ref[dst:dst + hd, :] = w_ref[W_IN_QC + hd * head:W_IN_QC + hd * (head + 1), :].astype(BF16)
    o_ref[C_GQA + 512:C_TOTAL, :] = w_ref[W_IN_QC + 512:W_IN_COLS, :].astype(BF16)


def _prep_w_in(w):
    L, D, N = w.shape
    tk = W_IN_PREP_TILE
    return pl.pallas_call(
        _w_in_prep_kernel,
        grid=(L, D // tk),
        in_specs=[pl.BlockSpec((None, N, tk), lambda l, i: (l, 0, i))],
        out_specs=pl.BlockSpec((None, C_TOTAL, tk), lambda l, i: (l, 0, i)),
        out_shape=jax.ShapeDtypeStruct((L, C_TOTAL, D), BF16),
        compiler_params=pltpu.CompilerParams(dimension_semantics=("parallel", "parallel"),
                                             vmem_limit_bytes=VMEM_LIMIT),
        name="w_in_prep",
    )(jnp.swapaxes(w, 1, 2))


def _prep_w_uq(w):
    z = jnp.zeros(w.shape[:-1] + (32,), BF16)
    hd = MLA_NOPE + MLA_ROPE
    return jnp.concatenate([a for h in range(MLA_HEADS) for a in (w[..., hd * h:hd * (h + 1)].astype(BF16), z)],
                           axis=-1)


def _prep_w_ukv(w):
    z = jnp.zeros(w.shape[:-1] + (64,), BF16)
    hd = MLA_NOPE + MLA_V
    ks = [a for h in range(MLA_HEADS) for a in (w[..., hd * h:hd * h + MLA_NOPE].astype(BF16), z)]
    vs = [w[..., hd * h + MLA_NOPE:hd * (h + 1)].astype(BF16) for h in range(MLA_HEADS)]
    return jnp.concatenate(ks + vs, axis=-1)


def _prep_w_br_gqa(w):
    rows = []
    for j in range(4):
        rows += [w[:, 64 * j:64 * (j + 1)], w[:, 64 * (4 + j):64 * (5 + j)]]
    return jnp.concatenate([r.astype(BF16) for r in rows], axis=1)


def kernel(x, c, ada_w, ada_b, norm_mix_g, norm_ffn_g, w_in, na_rpb, mla_q_norm_g, mla_kv_norm_g, mla_w_uq,
           mla_w_ukv, gqa_q_norm_g, gqa_k_norm_g, w_br_na, w_br_mla, w_br_gqa, w_out, ffn_w_gate_up,
           ffn_w_down, final_norm_g):
    B, S, D = x.shape
    L = ada_w.shape[0]
    mod = _ada_call(c, ada_w, ada_b).reshape(L, B, N_MOD, D)
    mtabs, atabs = _rope_tables(S)
    lane = np.arange(LANES)
    bd = jnp.asarray(np.tile((lane[:, None] // 64 == lane[None, :] // 64) / 64.0, (2, 1)), BF16)
    row = lambda a: a.reshape(L, 1, -1)
    gqk = jnp.concatenate([jnp.tile(gqa_q_norm_g, (1, GQA_HEADS)), jnp.tile(gqa_k_norm_g, (1, GQA_KV_HEADS))], 1)
    w_in_p, w_uq_p, w_ukv_p = _prep_w_in(w_in), _prep_w_uq(mla_w_uq), _prep_w_ukv(mla_w_ukv)
    bias = _na_bias_table(na_rpb)
    wa, wb, wc, wo = w_br_na.astype(BF16), w_br_mla.astype(BF16), _prep_w_br_gqa(w_br_gqa), w_out.astype(BF16)
    wgu, wd = ffn_w_gate_up.astype(BF16), ffn_w_down.astype(BF16)
    fg = final_norm_g.reshape(1, D)
    for l in range(L):
        qna, kna, vna, qm, km, vm, qg, kg, vg, gates = _in_call(
            x, mod, l, row(norm_mix_g), w_in_p, row(mla_q_norm_g), row(mla_kv_norm_g), w_uq_p, w_ukv_p,
            row(gqk), bd, mtabs, atabs)
        o_na = _na_call(qna, kna, vna, bias, l)
        o_mla = _attn_call(_mla_attn_kernel, qm, km, vm, MLA_HEADS * MLA_V, MLA_Q_TILE, MLA_HEADS,
                           [pltpu.VMEM((S, MLA_HEADS * LANES), BF16),
                            pltpu.VMEM((MLA_HEADS, VT_ROWS, S), BF16)], "mla_attn", pipelined=False)
        o_gqa = _attn_call(_gqa_attn_kernel, qg, kg, vg, GQA_HEADS * GQA_HEAD_DIM, GQA_Q_TILE, GQA_HEADS,
                           [pltpu.VMEM((S, 2 * LANES), BF16), pltpu.VMEM((GQA_KV_HEADS, VT_ROWS, S), BF16)],
                           "gqa_attn", pipelined=True)
        x = _post_call(x, mod, l, o_na, o_mla, o_gqa, gates, wa, wb, wc, wo, row(norm_ffn_g), wgu, wd, fg,
                       final_norm=(l == L - 1))
    return x
```

```python
import functools

import numpy as np
import jax
import jax.numpy as jnp
from jax import lax
from jax.experimental import pallas as pl
from jax.experimental.pallas import tpu as pltpu

D_MODEL = 1024
GRID_W = 64
NA_HEADS = 4
NA_HEAD_DIM = 64
NA_KH = 8
NA_KW = 16
MLA_HEADS = 4
MLA_Q_LORA = 256
MLA_KV_LORA = 128
MLA_NOPE = 64
MLA_ROPE = 32
MLA_V = 64
GQA_HEADS = 8
GQA_KV_HEADS = 2
GQA_HEAD_DIM = 64
FFN_HIDDEN = 2816
ROPE_THETA = 10000.0
AXIAL_THETA = 10000.0
NORM_EPS = 1e-6
N_MOD = 6
MASK_VALUE = -1e30
LOG2E = 1.4426950408889634

LANES = 128
VMEM_LIMIT = 56 * 1024 * 1024
TOKEN_TILE = 512
IN_TOKEN_TILE = 1024
MXU_TILE = 256
ADA_COL_TILE = 1536
W_IN_PREP_TILE = 256
MLA_Q_TILE = 512
GQA_Q_TILE = 256
_FFN_SPLIT = (FFN_HIDDEN // MXU_TILE + 1) // 2 * MXU_TILE
FFN_CHUNKS = ((0, _FFN_SPLIT), (_FFN_SPLIT, FFN_HIDDEN))

C_NA = 0
C_MLA = 768
C_GQA = 1280
C_GATE = 2048
C_TOTAL = 5120

F32 = jnp.float32
BF16 = jnp.bfloat16


def _dot(a, b):
    return jnp.dot(a, b, preferred_element_type=F32)


def _dot_nt(a, b):
    return lax.dot_general(a, b, (((1,), (1,)), ((), ())), preferred_element_type=F32)


def _rms(x):
    return x * lax.rsqrt(jnp.mean(x * x, axis=-1, keepdims=True) + NORM_EPS)


def _rope_slab(x, c, s_up, s_dn):
    return x * c + pltpu.roll(x, LANES - 16, 1) * s_up + pltpu.roll(x, 16, 1) * s_dn


def _sigmoid(x):
    return 1.0 / (1.0 + jnp.exp(-x))


def _layer_spec(a, l, single_buffer=False):
    mode = pl.Buffered(1) if single_buffer else None
    return pl.BlockSpec((None,) + a.shape[1:], lambda *_: (l,) + (0,) * (a.ndim - 1), pipeline_mode=mode)


def _mod_spec(mod, l):
    return pl.BlockSpec((None, None) + mod.shape[2:], lambda b, *_: (l, b, 0, 0))


def _ada_kernel(c_ref, w_ref, b_ref, o_ref):
    c = c_ref[...]
    s = (c * _sigmoid(c)).astype(BF16)
    o_ref[...] = _dot(s, w_ref[...].astype(BF16)) + b_ref[...]


def _ada_call(c, ada_w, ada_b):
    L, D, N = ada_w.shape
    B = c.shape[0]
    tn = ADA_COL_TILE
    return pl.pallas_call(
        _ada_kernel,
        grid=(L, N // tn),
        in_specs=[pl.BlockSpec((B, D), lambda l, j: (0, 0)),
                  pl.BlockSpec((None, D, tn), lambda l, j: (l, 0, j)),
                  pl.BlockSpec((None, 1, tn), lambda l, j: (l, 0, j))],
        out_specs=pl.BlockSpec((None, B, tn), lambda l, j: (l, 0, j)),
        out_shape=jax.ShapeDtypeStruct((L, B, N), F32),
        compiler_params=pltpu.CompilerParams(dimension_semantics=("parallel", "parallel"),
                                             vmem_limit_bytes=VMEM_LIMIT),
        name="ada_mod",
    )(c, ada_w, ada_b.reshape(L, 1, N))


def _in_kernel(x_ref, mod_ref, g_ref, w_ref, qng_ref, kvng_ref, wuq_ref, wukv_ref, gqk_ref, bd_ref,
               mc_ref, mu_ref, md_ref, ac_ref, au_ref, ad_ref,
               qna_ref, kna_ref, vna_ref, qm_ref, km_ref, vm_ref, qg_ref, kg_ref, vg_ref, gate_ref):
    x = x_ref[...]
    h = _rms(x) * g_ref[...] * (1.0 + mod_ref[1:2, :]) + mod_ref[0:1, :]
    hb = h.astype(BF16)

    def proj(a, b):
        return _dot_nt(hb, w_ref[a:b, :])

    p = proj(C_NA, C_NA + 768)
    qna_ref[...] = (p[:, :256] * (NA_HEAD_DIM ** -0.5 * LOG2E)).astype(BF16)
    kna_ref[...] = p[:, 256:512].astype(BF16)
    vna_ref[...] = p[:, 512:768].astype(BF16)

    p = proj(C_MLA, C_MLA + 512)
    mc, mu, md = mc_ref[...], mu_ref[...], md_ref[...]
    nq = (_rms(p[:, :256]) * qng_ref[...]).astype(BF16)
    qb = _dot(nq, wuq_ref[...])
    mla_scale = (MLA_NOPE + MLA_ROPE) ** -0.5 * LOG2E
    for s in range(MLA_HEADS):
        sl = slice(s * LANES, (s + 1) * LANES)
        qm_ref[:, sl] = (_rope_slab(qb[:, sl], mc, mu, md) * mla_scale).astype(BF16)
    nkv = (_rms(p[:, 256:384]) * kvng_ref[...]).astype(BF16)
    kvb = _dot(nkv, wukv_ref[...])
    kpe = _rope_slab(p[:, 384:512], mc, mu, md)
    for s in range(MLA_HEADS):
        sl = slice(s * LANES, (s + 1) * LANES)
        km_ref[:, sl] = (kvb[:, sl] + kpe).astype(BF16)
    vm_ref[...] = kvb[:, 512:768].astype(BF16)

    p = proj(C_GQA, C_GQA + 768)
    ac, au, ad = ac_ref[...], au_ref[...], ad_ref[...]
    bd = bd_ref[...]
    for s in range(5):
        sl = slice(s * LANES, (s + 1) * LANES)
        v = p[:, sl]
        sq = v * v
        hi = sq.astype(BF16)
        lo = (sq - hi.astype(F32)).astype(BF16)
        msq = _dot(jnp.concatenate([hi, lo], axis=1), bd)
        y = v * lax.rsqrt(msq + NORM_EPS) * gqk_ref[:, sl]
        r = _rope_slab(y, ac, au, ad)
        if s < 4:
            qg_ref[:, sl] = (r * (GQA_HEAD_DIM ** -0.5 * LOG2E)).astype(BF16)
        else:
            kg_ref[...] = r.astype(BF16)
    vg_ref[...] = p[:, 640:768].astype(BF16)

    for j in range(3):
        g = proj(C_GATE + j * D_MODEL, C_GATE + (j + 1) * D_MODEL)
        gate_ref[:, j * D_MODEL:(j + 1) * D_MODEL] = _sigmoid(g).astype(BF16)


def _in_call(x, mod, l, g, w, qng, kvng, wuq, wukv, gqk, bd, mtabs, atabs):
    B, S, D = x.shape
    tm = IN_TOKEN_TILE
    tok = lambda w_: pl.BlockSpec((None, tm, w_), lambda b, i: (b, i, 0))
    full = lambda a: pl.BlockSpec(a.shape, lambda b, i: (0,) * a.ndim)
    tab = pl.BlockSpec((tm, LANES), lambda b, i: (i, 0))
    widths = (256, 256, 256, 512, 512, 256, 512, 128, 128, 3 * D)
    return pl.pallas_call(
        _in_kernel,
        grid=(B, S // tm),
        in_specs=[tok(D), _mod_spec(mod, l)]
        + [_layer_spec(a, l, single_buffer=True) for a in (g, w, qng, kvng, wuq, wukv, gqk)]
        + [full(bd)] + [tab] * 6,
        out_specs=[tok(w_) for w_ in widths],
        out_shape=[jax.ShapeDtypeStruct((B, S, w_), BF16) for w_ in widths],
        compiler_params=pltpu.CompilerParams(dimension_semantics=("parallel", "parallel"),
                                             vmem_limit_bytes=VMEM_LIMIT),
        name="in_proj",
    )(x, mod, g, w, qng, kvng, wuq, wukv, gqk, bd, *mtabs, *atabs)


def _na_kernel(q_ref, k_ref, v_ref, bias_ref, o_ref, *, rows):
    kh = NA_KH
    head_of_lane = lax.broadcasted_iota(jnp.int32, (GRID_W, NA_HEADS * NA_HEAD_DIM), 1) // NA_HEAD_DIM

    def body(r, carry):
        rs = jnp.clip(r - kh // 2, 0, rows - kh)
        q = q_ref[pl.ds(pl.multiple_of(r * GRID_W, GRID_W), GRID_W), :]
        kstart = pl.multiple_of(rs * GRID_W, GRID_W)
        kb = k_ref[pl.ds(kstart, kh * GRID_W), :]
        vb = v_ref[pl.ds(kstart, kh * GRID_W), :]
        qs = jnp.concatenate([jnp.where(head_of_lane == h, q, jnp.zeros_like(q)) for h in range(NA_HEADS)],
                             axis=0)
        s = _dot_nt(qs, kb) + bias_ref[r - rs]
        m = jnp.max(s, axis=-1, keepdims=True)
        p = jnp.exp2(s - m)
        l = jnp.sum(p, axis=-1, keepdims=True)
        o = _dot(p.astype(BF16), vb) / l
        out = jnp.zeros((GRID_W, NA_HEADS * NA_HEAD_DIM), F32)
        for h in range(NA_HEADS):
            out = jnp.where(head_of_lane == h, o[h * GRID_W:(h + 1) * GRID_W, :], out)
        o_ref[pl.ds(pl.multiple_of(r * GRID_W, GRID_W), GRID_W), :] = out.astype(BF16)
        return carry

    lax.fori_loop(0, rows, body, 0, unroll=32)


def _na_call(q, k, v, bias, l):
    B, S, W = q.shape
    rows = S // GRID_W
    blk = pl.BlockSpec((None, S, W), lambda b: (b, 0, 0))
    return pl.pallas_call(
        functools.partial(_na_kernel, rows=rows),
        grid=(B,),
        in_specs=[blk, blk, blk, _layer_spec(bias, l)],
        out_specs=blk,
        out_shape=jax.ShapeDtypeStruct((B, S, W), BF16),
        compiler_params=pltpu.CompilerParams(dimension_semantics=("parallel",),
                                             vmem_limit_bytes=VMEM_LIMIT),
        name="na_attn",
    )(q, k, v, bias)


L_MIN = 2.0 ** -60
MLA_BIAS_LANE = MLA_NOPE + MLA_ROPE


def _softmax_pv(q, k, v):
    s = _dot_nt(q, k)
    m = jnp.max(s, axis=-1, keepdims=True)
    p = jnp.exp2(s - m)
    l = jnp.sum(p, axis=-1, keepdims=True)
    return _dot(p.astype(BF16), v) / l


VT_ROWS = 80


def _store_vt(vt_ref, idx, vt_rows):
    hd = VT_ROWS - 16
    S = vt_rows.shape[1]
    vt_ref[idx, 0:hd, :] = vt_rows.astype(BF16)
    first = lax.broadcasted_iota(jnp.int32, (16, S), 0) == 0
    vt_ref[idx, hd:VT_ROWS, :] = jnp.where(first, 1.0, 0.0).astype(BF16)


def _row_norm(x):
    xf = x.astype(F32)
    return jnp.sqrt(jnp.sum(xf * xf, axis=-1, keepdims=True))


def _max_row_norm(x, mask=None):
    xf = x.astype(F32)
    sq = xf * xf if mask is None else jnp.where(mask, xf * xf, 0.0)
    n = jnp.max(jnp.sum(sq, axis=-1, keepdims=True), axis=0, keepdims=True)
    return jnp.broadcast_to(jnp.sqrt(n), (1, LANES))


def _mla_exact(q_ref, k_ref, v_ref, o_ref):
    v = v_ref[...]
    head_of_lane = lax.broadcasted_iota(jnp.int32, o_ref.shape, 1) // MLA_V
    out = jnp.zeros(o_ref.shape, F32)
    for h in range(MLA_HEADS):
        sl = slice(h * LANES, (h + 1) * LANES)
        out = jnp.where(head_of_lane == h, _softmax_pv(q_ref[:, sl], k_ref[:, sl], v), out)
    o_ref[...] = out.astype(BF16)


def _pipelined_attention(init, scores, values, exact, pt_refs, lmin_ref):
    i = pl.program_id(1)
    n_tiles = pl.num_programs(1) - 1

    @pl.when(i == 0)
    def _():
        init()
        lmin_ref[0] = 1.0
        scores(pt_refs[0])

    for parity in (0, 1):
        @pl.when((i > 0) & (i < n_tiles) & (i % 2 == parity))
        def _():
            lmin_ref[0] = values(pt_refs[1 - parity])
            scores(pt_refs[parity])

    for parity in (0, 1):
        @pl.when((i == n_tiles) & (i % 2 == parity))
        def _():
            lmin_ref[0] = values(pt_refs[1 - parity])

    @pl.when((i > 0) & (lmin_ref[0] < L_MIN))
    def _():
        exact()


def _serial_attention(init, scores, values, exact, pt_ref):
    @pl.when(pl.program_id(1) == 0)
    def _():
        init()

    scores(pt_ref)
    lmin = values(pt_ref)

    @pl.when(lmin < L_MIN)
    def _():
        exact()


def _unpack_attn_refs(refs, pipelined):
    if pipelined:
        q, qprev, k, v, o, kaug, vt, kmax, pta, ptb, lmin = refs
        return q, qprev, k, v, o, kaug, vt, kmax, (pta, ptb), lmin
    q, k, v, o, kaug, vt, kmax, pt = refs
    return q, q, k, v, o, kaug, vt, kmax, (pt,), None


def _run_attention(pipelined, init, scores, values, exact, pt_refs, lmin_ref):
    if pipelined:
        _pipelined_attention(init, scores, values, exact, pt_refs, lmin_ref)
    else:
        _serial_attention(init, scores, values, exact, pt_refs[0])


def _mla_attn_kernel(*refs, pipelined):
    q_ref, qprev_ref, k_ref, v_ref, o_ref, kaug_ref, vt_ref, kmax_ref, pt_refs, lmin_ref = _unpack_attn_refs(
        refs, pipelined)
    S, tq = k_ref.shape[0], q_ref.shape[0]

    def init():
        lane_s = lax.broadcasted_iota(jnp.int32, (S, LANES), 1)
        for h in range(MLA_HEADS):
            sl = slice(h * LANES, (h + 1) * LANES)
            k = k_ref[:, sl]
            kaug_ref[:, sl] = jnp.where(lane_s == MLA_BIAS_LANE, jnp.ones_like(k), k)
            kmax_ref[h:h + 1, :] = _max_row_norm(k)
        for p in range(MLA_HEADS // 2):
            vt = jnp.transpose(v_ref[:, p * LANES:(p + 1) * LANES].astype(F32))
            _store_vt(vt_ref, 2 * p, vt[0:MLA_V])
            _store_vt(vt_ref, 2 * p + 1, vt[MLA_V:2 * MLA_V])

    def scores(pt_ref):
        lane = lax.broadcasted_iota(jnp.int32, (tq, LANES), 1)
        for h in range(MLA_HEADS):
            sl = slice(h * LANES, (h + 1) * LANES)
            q = q_ref[:, sl]
            b = _row_norm(q) * kmax_ref[h:h + 1, 0:1]
            q_aug = jnp.where(lane == MLA_BIAS_LANE, (-b).astype(BF16), q)
            pt_ref[:, h * tq:(h + 1) * tq] = jnp.exp2(_dot_nt(kaug_ref[:, sl], q_aug)).astype(BF16)

    def values(pt_ref):
        lmin = None
        for p in range(MLA_HEADS // 2):
            outs = []
            for h in (2 * p, 2 * p + 1):
                ot = _dot(vt_ref[h], pt_ref[:, h * tq:(h + 1) * tq])
                l = ot[MLA_V:MLA_V + 1, :]
                outs.append(ot[0:MLA_V, :] / l)
                lmin = l if lmin is None else jnp.minimum(lmin, l)
            o_ref[:, p * LANES:(p + 1) * LANES] = jnp.transpose(jnp.concatenate(outs, axis=0)).astype(BF16)
        return jnp.min(lmin)

    _run_attention(pipelined, init, scores, values, lambda: _mla_exact(qprev_ref, k_ref, v_ref, o_ref),
                   pt_refs, lmin_ref)


def _gqa_exact(q_ref, k_ref, v_ref, o_ref):
    k = k_ref[...]
    v = v_ref[...]
    low = lax.broadcasted_iota(jnp.int32, (q_ref.shape[0], LANES), 1) < GQA_HEAD_DIM
    for j in range(GQA_HEADS // GQA_KV_HEADS):
        sl = slice(j * LANES, (j + 1) * LANES)
        q = q_ref[:, sl]
        zero = jnp.zeros_like(q)
        o_lo = _softmax_pv(jnp.where(low, q, zero), k, v)
        o_hi = _softmax_pv(jnp.where(low, zero, q), k, v)
        o_ref[:, sl] = jnp.where(low, o_lo, o_hi).astype(BF16)


def _gqa_attn_kernel(*refs, pipelined):
    q_ref, qprev_ref, k_ref, v_ref, o_ref, kaug_ref, vt_ref, kmax_ref, pt_refs, lmin_ref = _unpack_attn_refs(
        refs, pipelined)
    S, tq = k_ref.shape[0], q_ref.shape[0]
    pairs = GQA_HEADS // GQA_KV_HEADS
    n = pairs * tq

    def init():
        lane_s = lax.broadcasted_iota(jnp.int32, (S, LANES), 1)
        k = k_ref[...]
        kaug_ref[:, :LANES] = k
        kaug_ref[:, LANES:] = jnp.where(lane_s == 0, 1.0, 0.0).astype(BF16)
        vt = jnp.transpose(v_ref[...].astype(F32))
        for g in range(GQA_KV_HEADS):
            _store_vt(vt_ref, g, vt[g * GQA_HEAD_DIM:(g + 1) * GQA_HEAD_DIM])
        kmax_ref[0:1, :] = _max_row_norm(k, lane_s < GQA_HEAD_DIM)
        kmax_ref[1:2, :] = _max_row_norm(k, lane_s >= GQA_HEAD_DIM)

    def scores(pt_ref):
        lane = lax.broadcasted_iota(jnp.int32, (tq, LANES), 1)
        low = lane < GQA_HEAD_DIM
        q_aug = [[], []]
        for j in range(pairs):
            q = q_ref[:, j * LANES:(j + 1) * LANES]
            zero = jnp.zeros_like(q)
            for half in range(GQA_KV_HEADS):
                qm = jnp.where(low, q, zero) if half == 0 else jnp.where(low, zero, q)
                b = _row_norm(qm) * kmax_ref[half:half + 1, 0:1]
                bias = jnp.where(lane == 0, -b, 0.0).astype(BF16)
                q_aug[half].append(jnp.concatenate([qm, bias], axis=1))
        pt_ref[...] = jnp.exp2(_dot_nt(kaug_ref[...], jnp.concatenate(q_aug[0] + q_aug[1], axis=0))).astype(BF16)

    def values(pt_ref):
        outs, lmin = [], None
        for half in range(GQA_KV_HEADS):
            ot = _dot(vt_ref[half], pt_ref[:, half * n:(half + 1) * n])
            l = ot[GQA_HEAD_DIM:GQA_HEAD_DIM + 1, :]
            outs.append(ot[0:GQA_HEAD_DIM, :] / l)
            lmin = l if lmin is None else jnp.minimum(lmin, l)
        for j in range(pairs):
            ot = jnp.concatenate([o[:, j * tq:(j + 1) * tq] for o in outs], axis=0)
            o_ref[:, j * LANES:(j + 1) * LANES] = jnp.transpose(ot).astype(BF16)
        return jnp.min(lmin)

    _run_attention(pipelined, init, scores, values, lambda: _gqa_exact(qprev_ref, k_ref, v_ref, o_ref),
                   pt_refs, lmin_ref)


def _attn_call(kernel, q, k, v, out_width, tq, heads, scratch, name, pipelined):
    B, S, _ = q.shape
    n_tiles = S // tq
    pt = pltpu.VMEM((S, heads * tq), BF16)
    kv_specs = [pl.BlockSpec((None, S, k.shape[2]), lambda b, i: (b, 0, 0)),
                pl.BlockSpec((None, S, v.shape[2]), lambda b, i: (b, 0, 0))]
    scratch = scratch + [pltpu.VMEM((8, LANES), F32)]
    if pipelined:
        cur = lambda b, i: (b, jnp.minimum(i, n_tiles - 1), 0)
        prev = lambda b, i: (b, jnp.maximum(i - 1, 0), 0)
        grid = (B, n_tiles + 1)
        q_specs = [pl.BlockSpec((None, tq, q.shape[2]), cur), pl.BlockSpec((None, tq, q.shape[2]), prev)]
        out_spec = pl.BlockSpec((None, tq, out_width), prev)
        scratch = scratch + [pt, pt, pltpu.SMEM((1,), F32)]
    else:
        tile = lambda b, i: (b, i, 0)
        grid = (B, n_tiles)
        q_specs = [pl.BlockSpec((None, tq, q.shape[2]), tile)]
        out_spec = pl.BlockSpec((None, tq, out_width), tile)
        scratch = scratch + [pt]
    return pl.pallas_call(
        functools.partial(kernel, pipelined=pipelined),
        grid=grid,
        in_specs=q_specs + kv_specs,
        out_specs=out_spec,
        out_shape=jax.ShapeDtypeStruct((B, S, out_width), BF16),
        scratch_shapes=scratch,
        compiler_params=pltpu.CompilerParams(dimension_semantics=("parallel", "arbitrary"),
                                             vmem_limit_bytes=VMEM_LIMIT),
        name=name,
    )(*([q] * len(q_specs)), k, v)


def _post_kernel(x_ref, mod_ref, ona_ref, omla_ref, ogqa_ref, gate_ref, wa_ref, wb_ref, wc_ref, wo_ref,
                 g_ref, wgu_ref, wd_ref, fg_ref, o_ref, *, final_norm):
    D = D_MODEL
    m = (gate_ref[:, 0:D].astype(F32) * _dot(ona_ref[...], wa_ref[...])
         + gate_ref[:, D:2 * D].astype(F32) * _dot(omla_ref[...], wb_ref[...])
         + gate_ref[:, 2 * D:3 * D].astype(F32) * _dot(ogqa_ref[...], wc_ref[...]))
    x = x_ref[...] + mod_ref[2:3, :] * _dot(m.astype(BF16), wo_ref[...])

    h = _rms(x) * g_ref[...] * (1.0 + mod_ref[4:5, :]) + mod_ref[3:4, :]
    hb = h.astype(BF16)
    acc = jnp.zeros(x.shape, F32)
    for c0, c1 in FFN_CHUNKS:
        gate = _dot(hb, wgu_ref[:, c0:c1])
        up = _dot(hb, wgu_ref[:, FFN_HIDDEN + c0:FFN_HIDDEN + c1])
        act = (gate * _sigmoid(gate) * up).astype(BF16)
        acc = acc + _dot(act, wd_ref[c0:c1, :])
    y = x + mod_ref[5:6, :] * acc
    if final_norm:
        y = _rms(y) * fg_ref[...]
    o_ref[...] = y


def _post_call(x, mod, l, ona, omla, ogqa, gates, wa, wb, wc, wo, g, wgu, wd, fg, final_norm):
    B, S, D = x.shape
    tm = TOKEN_TILE
    tok = lambda a: pl.BlockSpec((None, tm, a.shape[2]), lambda b, i: (b, i, 0))
    weights = (wa, wb, wc, wo, g, wgu, wd)
    return pl.pallas_call(
        functools.partial(_post_kernel, final_norm=final_norm),
        grid=(B, S // tm),
        in_specs=[tok(x), _mod_spec(mod, l), tok(ona), tok(omla), tok(ogqa), tok(gates)]
        + [_layer_spec(a, l, single_buffer=True) for a in weights]
        + [pl.BlockSpec(fg.shape, lambda b, i: (0, 0))],
        out_specs=tok(x),
        out_shape=jax.ShapeDtypeStruct(x.shape, F32),
        compiler_params=pltpu.CompilerParams(dimension_semantics=("parallel", "parallel"),
                                             vmem_limit_bytes=VMEM_LIMIT),
        name="merge_swiglu",
    )(x, mod, ona, omla, ogqa, gates, *weights, fg)


def _rope_tables(S):
    f32 = np.float32
    t = np.arange(S, dtype=np.int32)
    pos_t = t.astype(f32)
    pos_row = (t // GRID_W).astype(f32)
    pos_col = (t % GRID_W).astype(f32)
    half = 16
    lane = np.arange(LANES)

    def tables(pos_of_lane, roped, first_half, inv_freq):
        ang = (pos_of_lane * inv_freq[lane % half][None, :]).astype(f32)
        cos, sin = np.cos(ang), np.sin(ang)
        c = np.where(roped[None, :], cos, f32(1.0))
        up = np.where((roped & first_half)[None, :], -sin, f32(0.0))
        dn = np.where((roped & ~first_half)[None, :], sin, f32(0.0))
        return tuple(jnp.asarray(a, F32) for a in (c, up, dn))

    inv_t = (f32(1.0) / (f32(ROPE_THETA) ** (np.arange(half, dtype=f32) / f32(half)))).astype(f32)
    inv_a = (f32(1.0) / (f32(AXIAL_THETA) ** (np.arange(half, dtype=f32) / f32(half)))).astype(f32)
    m_roped = (lane >= MLA_NOPE) & (lane < MLA_NOPE + MLA_ROPE)
    m_first = ((lane - MLA_NOPE) % 32) < half
    mt = tables(np.broadcast_to(pos_t[:, None], (S, LANES)), m_roped, m_first, inv_t)
    a_first = (lane % 32) < half
    use_row = (lane % GRID_W) < 32
    pos_rc = np.where(use_row[None, :], pos_row[:, None], pos_col[:, None])
    at = tables(pos_rc, np.ones(LANES, bool), a_first, inv_a)
    return mt, at


def _na_bias_kernel(ext_ref, o_ref):
    W = GRID_W
    n_off = 2 * NA_KH - 1
    col = lax.broadcasted_iota(jnp.int32, (W, LANES), 0)
    lane = lax.broadcasted_iota(jnp.int32, (W, LANES), 1)
    kc = lane % W
    ws = jnp.clip(col - NA_KW // 2, 0, W - NA_KW)
    valid = (kc >= ws) & (kc < ws + NA_KW)
    low = lane < W
    for h in range(NA_HEADS):
        lo, hi = [], []
        for ro in range(n_off):
            e = jnp.broadcast_to(ext_ref[h * n_off + ro:h * n_off + ro + 1, :], (W, LANES))
            lo.append(pltpu.roll(e, W + 1, 1, stride=1, stride_axis=0))
            hi.append(pltpu.roll(e, 1, 1, stride=1, stride_axis=0))
        for d in range(NA_KH):
            for p in range(NA_KH // 2):
                ro = 2 * p - d + NA_KH - 1
                blk = jnp.where(low, lo[ro], hi[ro + 1])
                o_ref[d, h * W:(h + 1) * W, p * LANES:(p + 1) * LANES] = jnp.where(valid, blk * LOG2E, MASK_VALUE)


def _na_bias_table(rpb):
    L, W = rpb.shape[0], GRID_W
    pad = W - NA_KW
    ext = jnp.concatenate([jnp.repeat(rpb[..., :1], pad, -1), rpb, jnp.repeat(rpb[..., -1:], pad + 1, -1)], -1)
    ext = ext.reshape(L, NA_HEADS * (2 * NA_KH - 1), 2 * W).astype(F32)
    return pl.pallas_call(
        _na_bias_kernel,
        grid=(L,),
        in_specs=[pl.BlockSpec((None,) + ext.shape[1:], lambda l: (l, 0, 0))],
        out_specs=pl.BlockSpec((None, NA_KH, NA_HEADS * W, NA_KH * W), lambda l: (l, 0, 0, 0)),
        out_shape=jax.ShapeDtypeStruct((L, NA_KH, NA_HEADS * W, NA_KH * W), F32),
        compiler_params=pltpu.CompilerParams(dimension_semantics=("parallel",), vmem_limit_bytes=VMEM_LIMIT),
        name="na_bias",
    )(ext)


W_IN_KR = 1152
W_IN_QC = 1184
W_IN_COLS = 5024


def _w_in_prep_kernel(w_ref, o_ref):
    tk = w_ref.shape[1]
    hd, pairs = GQA_HEAD_DIM, GQA_HEADS // GQA_KV_HEADS
    kr0 = C_MLA + 384 + MLA_NOPE
    o_ref[0:W_IN_KR, :] = w_ref[0:W_IN_KR, :].astype(BF16)
    o_ref[C_MLA + 384:kr0, :] = jnp.zeros((MLA_NOPE, tk), BF16)
    o_ref[kr0:kr0 + MLA_ROPE, :] = w_ref[W_IN_KR:W_IN_QC, :].astype(BF16)
    o_ref[kr0 + MLA_ROPE:C_GQA, :] = jnp.zeros((LANES - MLA_NOPE - MLA_ROPE, tk), BF16)
    for j in range(pairs):
        for half, head in enumerate((j, pairs + j)):
            dst = C_GQA + LANES * j + hd * half
            o_ref[dst:dst + hd, :] = w_ref[W_IN_QC + hd * head:W_IN_QC + hd * (head + 1), :].astype(BF16)
    o_ref[C_GQA + 512:C_TOTAL, :] = w_ref[W_IN_QC + 512:W_IN_COLS, :].astype(BF16)


def _prep_w_in(w):
    L, D, N = w.shape
    tk = W_IN_PREP_TILE
    return pl.pallas_call(
        _w_in_prep_kernel,
        grid=(L, D // tk),
        in_specs=[pl.BlockSpec((None, N, tk), lambda l, i: (l, 0, i))],
        out_specs=pl.BlockSpec((None, C_TOTAL, tk), lambda l, i: (l, 0, i)),
        out_shape=jax.ShapeDtypeStruct((L, C_TOTAL, D), BF16),
        compiler_params=pltpu.CompilerParams(dimension_semantics=("parallel", "parallel"),
                                             vmem_limit_bytes=VMEM_LIMIT),
        name="w_in_prep",
    )(jnp.swapaxes(w, 1, 2))


def _prep_w_uq(w):
    z = jnp.zeros(w.shape[:-1] + (32,), BF16)
    hd = MLA_NOPE + MLA_ROPE
    return jnp.concatenate([a for h in range(MLA_HEADS) for a in (w[..., hd * h:hd * (h + 1)].astype(BF16), z)],
                           axis=-1)


def _prep_w_ukv(w):
    z = jnp.zeros(w.shape[:-1] + (64,), BF16)
    hd = MLA_NOPE + MLA_V
    ks = [a for h in range(MLA_HEADS) for a in (w[..., hd * h:hd * h + MLA_NOPE].astype(BF16), z)]
    vs = [w[..., hd * h + MLA_NOPE:hd * (h + 1)].astype(BF16) for h in range(MLA_HEADS)]
    return jnp.concatenate(ks + vs, axis=-1)


def _prep_w_br_gqa(w):
    rows = []
    for j in range(4):
        rows += [w[:, 64 * j:64 * (j + 1)], w[:, 64 * (4 + j):64 * (5 + j)]]
    return jnp.concatenate([r.astype(BF16) for r in rows], axis=1)


def kernel(x, c, ada_w, ada_b, norm_mix_g, norm_ffn_g, w_in, na_rpb, mla_q_norm_g, mla_kv_norm_g, mla_w_uq,
           mla_w_ukv, gqa_q_norm_g, gqa_k_norm_g, w_br_na, w_br_mla, w_br_gqa, w_out, ffn_w_gate_up,
           ffn_w_down, final_norm_g):
    B, S, D = x.shape
    L = ada_w.shape[0]
    mod = _ada_call(c, ada_w, ada_b).reshape(L, B, N_MOD, D)
    mtabs, atabs = _rope_tables(S)
    lane = np.arange(LANES)
    bd = jnp.asarray(np.tile((lane[:, None] // 64 == lane[None, :] // 64) / 64.0, (2, 1)), BF16)
    row = lambda a: a.reshape(L, 1, -1)
    gqk = jnp.concatenate([jnp.tile(gqa_q_norm_g, (1, GQA_HEADS)), jnp.tile(gqa_k_norm_g, (1, GQA_KV_HEADS))], 1)
    w_in_p, w_uq_p, w_ukv_p = _prep_w_in(w_in), _prep_w_uq(mla_w_uq), _prep_w_ukv(mla_w_ukv)
    bias = _na_bias_table(na_rpb)
    wa, wb, wc, wo = w_br_na.astype(BF16), w_br_mla.astype(BF16), _prep_w_br_gqa(w_br_gqa), w_out.astype(BF16)
    wgu, wd = ffn_w_gate_up.astype(BF16), ffn_w_down.astype(BF16)
    fg = final_norm_g.reshape(1, D)
    for l in range(L):
        qna, kna, vna, qm, km, vm, qg, kg, vg, gates = _in_call(
            x, mod, l, row(norm_mix_g), w_in_p, row(mla_q_norm_g), row(mla_kv_norm_g), w_uq_p, w_ukv_p,
            row(gqk), bd, mtabs, atabs)
        o_na = _na_call(qna, kna, vna, bias, l)
        o_mla = _attn_call(_mla_attn_kernel, qm, km, vm, MLA_HEADS * MLA_V, MLA_Q_TILE, MLA_HEADS,
                           [pltpu.VMEM((S, MLA_HEADS * LANES), BF16),
                            pltpu.VMEM((MLA_HEADS, VT_ROWS, S), BF16)], "mla_attn", pipelined=False)
        o_gqa = _attn_call(_gqa_attn_kernel, qg, kg, vg, GQA_HEADS * GQA_HEAD_DIM, GQA_Q_TILE, GQA_HEADS,
                           [pltpu.VMEM((S, 2 * LANES), BF16), pltpu.VMEM((GQA_KV_HEADS, VT_ROWS, S), BF16)],
                           "gqa_attn", pipelined=True)
        x = _post_call(x, mod, l, o_na, o_mla, o_gqa, gates, wa, wb, wc, wo, row(norm_ffn_g), wgu, wd, fg,
                       final_norm=(l == L - 1))
    return x
```

```python
import functools

import numpy as np
import jax
import jax.numpy as jnp
from jax import lax
from jax.experimental import pallas as pl
from jax.experimental.pallas import tpu as pltpu

D_MODEL = 1024
GRID_W = 64
NA_HEADS = 4
NA_HEAD_DIM = 64
NA_KH = 8
NA_KW = 16
MLA_HEADS = 4
MLA_Q_LORA = 256
MLA_KV_LORA = 128
MLA_NOPE = 64
MLA_ROPE = 32
MLA_V = 64
GQA_HEADS = 8
GQA_KV_HEADS = 2
GQA_HEAD_DIM = 64
FFN_HIDDEN = 2816
ROPE_THETA = 10000.0
AXIAL_THETA = 10000.0
NORM_EPS = 1e-6
N_MOD = 6
MASK_VALUE = -1e30
LOG2E = 1.4426950408889634

LANES = 128
VMEM_LIMIT = 56 * 1024 * 1024
TOKEN_TILE = 512
IN_TOKEN_TILE = 1024
MXU_TILE = 256
ADA_COL_TILE = 1536
W_IN_PREP_TILE = 256
MLA_Q_TILE = 512
GQA_Q_TILE = 256
_FFN_SPLIT = (FFN_HIDDEN // MXU_TILE + 1) // 2 * MXU_TILE
FFN_CHUNKS = ((0, _FFN_SPLIT), (_FFN_SPLIT, FFN_HIDDEN))

C_NA = 0
C_MLA = 768
C_GQA = 1280
C_GATE = 2048
C_TOTAL = 5120

F32 = jnp.float32
BF16 = jnp.bfloat16


def _dot(a, b):
    return jnp.dot(a, b, preferred_element_type=F32)


def _dot_nt(a, b):
    return lax.dot_general(a, b, (((1,), (1,)), ((), ())), preferred_element_type=F32)


def _rms(x):
    return x * lax.rsqrt(jnp.mean(x * x, axis=-1, keepdims=True) + NORM_EPS)


def _rope_slab(x, c, s_up, s_dn):
    return x * c + pltpu.roll(x, LANES - 16, 1) * s_up + pltpu.roll(x, 16, 1) * s_dn


def _sigmoid(x):
    return 1.0 / (1.0 + jnp.exp(-x))


def _layer_spec(a, l, single_buffer=False):
    mode = pl.Buffered(1) if single_buffer else None
    return pl.BlockSpec((None,) + a.shape[1:], lambda *_: (l,) + (0,) * (a.ndim - 1), pipeline_mode=mode)


def _mod_spec(mod, l):
    return pl.BlockSpec((None, None) + mod.shape[2:], lambda b, *_: (l, b, 0, 0))


def _ada_kernel(c_ref, w_ref, b_ref, o_ref):
    c = c_ref[...]
    s = (c * _sigmoid(c)).astype(BF16)
    o_ref[...] = _dot(s, w_ref[...].astype(BF16)) + b_ref[...]


def _ada_call(c, ada_w, ada_b):
    L, D, N = ada_w.shape
    B = c.shape[0]
    tn = ADA_COL_TILE
    return pl.pallas_call(
        _ada_kernel,
        grid=(L, N // tn),
        in_specs=[pl.BlockSpec((B, D), lambda l, j: (0, 0)),
                  pl.BlockSpec((None, D, tn), lambda l, j: (l, 0, j)),
                  pl.BlockSpec((None, 1, tn), lambda l, j: (l, 0, j))],
        out_specs=pl.BlockSpec((None, B, tn), lambda l, j: (l, 0, j)),
        out_shape=jax.ShapeDtypeStruct((L, B, N), F32),
        compiler_params=pltpu.CompilerParams(dimension_semantics=("parallel", "parallel"),
                                             vmem_limit_bytes=VMEM_LIMIT),
        name="ada_mod",
    )(c, ada_w, ada_b.reshape(L, 1, N))


def _in_kernel(x_ref, mod_ref, g_ref, w_ref, qng_ref, kvng_ref, wuq_ref, wukv_ref, gqk_ref, bd_ref,
               mc_ref, mu_ref, md_ref, ac_ref, au_ref, ad_ref,
               qna_ref, kna_ref, vna_ref, qm_ref, km_ref, vm_ref, qg_ref, kg_ref, vg_ref, gate_ref):
    x = x_ref[...]
    h = _rms(x) * g_ref[...] * (1.0 + mod_ref[1:2, :]) + mod_ref[0:1, :]
    hb = h.astype(BF16)

    def proj(a, b):
        return _dot_nt(hb, w_ref[a:b, :])

    p = proj(C_NA, C_NA + 768)
    qna_ref[...] = (p[:, :256] * (NA_HEAD_DIM ** -0.5 * LOG2E)).astype(BF16)
    kna_ref[...] = p[:, 256:512].astype(BF16)
    vna_ref[...] = p[:, 512:768].astype(BF16)

    p = proj(C_MLA, C_MLA + 512)
    mc, mu, md = mc_ref[...], mu_ref[...], md_ref[...]
    nq = (_rms(p[:, :256]) * qng_ref[...]).astype(BF16)
    qb = _dot(nq, wuq_ref[...])
    mla_scale = (MLA_NOPE + MLA_ROPE) ** -0.5 * LOG2E
    for s in range(MLA_HEADS):
        sl = slice(s * LANES, (s + 1) * LANES)
        qm_ref[:, sl] = (_rope_slab(qb[:, sl], mc, mu, md) * mla_scale).astype(BF16)
    nkv = (_rms(p[:, 256:384]) * kvng_ref[...]).astype(BF16)
    kvb = _dot(nkv, wukv_ref[...])
    kpe = _rope_slab(p[:, 384:512], mc, mu, md)
    for s in range(MLA_HEADS):
        sl = slice(s * LANES, (s + 1) * LANES)
        km_ref[:, sl] = (kvb[:, sl] + kpe).astype(BF16)
    vm_ref[...] = kvb[:, 512:768].astype(BF16)

    p = proj(C_GQA, C_GQA + 768)
    ac, au, ad = ac_ref[...], au_ref[...], ad_ref[...]
    bd = bd_ref[...]
    for s in range(5):
        sl = slice(s * LANES, (s + 1) * LANES)
        v = p[:, sl]
        sq = v * v
        hi = sq.astype(BF16)
        lo = (sq - hi.astype(F32)).astype(BF16)
        msq = _dot(jnp.concatenate([hi, lo], axis=1), bd)
        y = v * lax.rsqrt(msq + NORM_EPS) * gqk_ref[:, sl]
        r = _rope_slab(y, ac, au, ad)
        if s < 4:
            qg_ref[:, sl] = (r * (GQA_HEAD_DIM ** -0.5 * LOG2E)).astype(BF16)
        else:
            kg_ref[...] = r.astype(BF16)
    vg_ref[...] = p[:, 640:768].astype(BF16)

    for j in range(3):
        g = proj(C_GATE + j * D_MODEL, C_GATE + (j + 1) * D_MODEL)
        gate_ref[:, j * D_MODEL:(j + 1) * D_MODEL] = _sigmoid(g).astype(BF16)


def _in_call(x, mod, l, g, w, qng, kvng, wuq, wukv, gqk, bd, mtabs, atabs):
    B, S, D = x.shape
    tm = IN_TOKEN_TILE
    tok = lambda w_: pl.BlockSpec((None, tm, w_), lambda b, i: (b, i, 0))
    full = lambda a: pl.BlockSpec(a.shape, lambda b, i: (0,) * a.ndim)
    tab = pl.BlockSpec((tm, LANES), lambda b, i: (i, 0))
    widths = (256, 256, 256, 512, 512, 256, 512, 128, 128, 3 * D)
    return pl.pallas_call(
        _in_kernel,
        grid=(B, S // tm),
        in_specs=[tok(D), _mod_spec(mod, l)]
        + [_layer_spec(a, l, single_buffer=True) for a in (g, w, qng, kvng, wuq, wukv, gqk)]
        + [full(bd)] + [tab] * 6,
        out_specs=[tok(w_) for w_ in widths],
        out_shape=[jax.ShapeDtypeStruct((B, S, w_), BF16) for w_ in widths],
        compiler_params=pltpu.CompilerParams(dimension_semantics=("parallel", "parallel"),
                                             vmem_limit_bytes=VMEM_LIMIT),
        name="in_proj",
    )(x, mod, g, w, qng, kvng, wuq, wukv, gqk, bd, *mtabs, *atabs)


def _na_kernel(q_ref, k_ref, v_ref, bias_ref, o_ref, *, rows):
    kh = NA_KH
    head_of_lane = lax.broadcasted_iota(jnp.int32, (GRID_W, NA_HEADS * NA_HEAD_DIM), 1) // NA_HEAD_DIM

    def body(r, carry):
        rs = jnp.clip(r - kh // 2, 0, rows - kh)
        q = q_ref[pl.ds(pl.multiple_of(r * GRID_W, GRID_W), GRID_W), :]
        kstart = pl.multiple_of(rs * GRID_W, GRID_W)
        kb = k_ref[pl.ds(kstart, kh * GRID_W), :]
        vb = v_ref[pl.ds(kstart, kh * GRID_W), :]
        qs = jnp.concatenate([jnp.where(head_of_lane == h, q, jnp.zeros_like(q)) for h in range(NA_HEADS)],
                             axis=0)
        s = _dot_nt(qs, kb) + bias_ref[r - rs]
        m = jnp.max(s, axis=-1, keepdims=True)
        p = jnp.exp2(s - m)
        l = jnp.sum(p, axis=-1, keepdims=True)
        o = _dot(p.astype(BF16), vb) / l
        out = jnp.zeros((GRID_W, NA_HEADS * NA_HEAD_DIM), F32)
        for h in range(NA_HEADS):
            out = jnp.where(head_of_lane == h, o[h * GRID_W:(h + 1) * GRID_W, :], out)
        o_ref[pl.ds(pl.multiple_of(r * GRID_W, GRID_W), GRID_W), :] = out.astype(BF16)
        return carry

    lax.fori_loop(0, rows, body, 0, unroll=32)


def _na_call(q, k, v, bias, l):
    B, S, W = q.shape
    rows = S // GRID_W
    blk = pl.BlockSpec((None, S, W), lambda b: (b, 0, 0))
    return pl.pallas_call(
        functools.partial(_na_kernel, rows=rows),
        grid=(B,),
        in_specs=[blk, blk, blk, _layer_spec(bias, l)],
        out_specs=blk,
        out_shape=jax.ShapeDtypeStruct((B, S, W), BF16),
        compiler_params=pltpu.CompilerParams(dimension_semantics=("parallel",),
                                             vmem_limit_bytes=VMEM_LIMIT),
        name="na_attn",
    )(q, k, v, bias)


L_MIN = 2.0 ** -60
MLA_BIAS_LANE = MLA_NOPE + MLA_ROPE


def _softmax_pv(q, k, v):
    s = _dot_nt(q, k)
    m = jnp.max(s, axis=-1, keepdims=True)
    p = jnp.exp2(s - m)
    l = jnp.sum(p, axis=-1, keepdims=True)
    return _dot(p.astype(BF16), v) / l


VT_ROWS = 80


def _store_vt(vt_ref, idx, vt_rows):
    hd = VT_ROWS - 16
    S = vt_rows.shape[1]
    vt_ref[idx, 0:hd, :] = vt_rows.astype(BF16)
    first = lax.broadcasted_iota(jnp.int32, (16, S), 0) == 0
    vt_ref[idx, hd:VT_ROWS, :] = jnp.where(first, 1.0, 0.0).astype(BF16)


def _row_norm(x):
    xf = x.astype(F32)
    return jnp.sqrt(jnp.sum(xf * xf, axis=-1, keepdims=True))


def _max_row_norm(x, mask=None):
    xf = x.astype(F32)
    sq = xf * xf if mask is None else jnp.where(mask, xf * xf, 0.0)
    n = jnp.max(jnp.sum(sq, axis=-1, keepdims=True), axis=0, keepdims=True)
    return jnp.broadcast_to(jnp.sqrt(n), (1, LANES))


def _mla_exact(q_ref, k_ref, v_ref, o_ref):
    v = v_ref[...]
    head_of_lane = lax.broadcasted_iota(jnp.int32, o_ref.shape, 1) // MLA_V
    out = jnp.zeros(o_ref.shape, F32)
    for h in range(MLA_HEADS):
        sl = slice(h * LANES, (h + 1) * LANES)
        out = jnp.where(head_of_lane == h, _softmax_pv(q_ref[:, sl], k_ref[:, sl], v), out)
    o_ref[...] = out.astype(BF16)


def _pipelined_attention(init, scores, values, exact, pt_refs, lmin_ref):
    i = pl.program_id(1)
    n_tiles = pl.num_programs(1) - 1

    @pl.when(i == 0)
    def _():
        init()
        lmin_ref[0] = 1.0
        scores(pt_refs[0])

    for parity in (0, 1):
        @pl.when((i > 0) & (i < n_tiles) & (i % 2 == parity))
        def _():
            lmin_ref[0] = values(pt_refs[1 - parity])
            scores(pt_refs[parity])

    for parity in (0, 1):
        @pl.when((i == n_tiles) & (i % 2 == parity))
        def _():
            lmin_ref[0] = values(pt_refs[1 - parity])

    @pl.when((i > 0) & (lmin_ref[0] < L_MIN))
    def _():
        exact()


def _serial_attention(init, scores, values, exact, pt_ref):
    @pl.when(pl.program_id(1) == 0)
    def _():
        init()

    scores(pt_ref)
    lmin = values(pt_ref)

    @pl.when(lmin < L_MIN)
    def _():
        exact()


def _unpack_attn_refs(refs, pipelined):
    if pipelined:
        q, qprev, k, v, o, kaug, vt, kmax, pta, ptb, lmin = refs
        return q, qprev, k, v, o, kaug, vt, kmax, (pta, ptb), lmin
    q, k, v, o, kaug, vt, kmax, pt = refs
    return q, q, k, v, o, kaug, vt, kmax, (pt,), None


def _run_attention(pipelined, init, scores, values, exact, pt_refs, lmin_ref):
    if pipelined:
        _pipelined_attention(init, scores, values, exact, pt_refs, lmin_ref)
    else:
        _serial_attention(init, scores, values, exact, pt_refs[0])


def _mla_attn_kernel(q_ref, k_ref, v_ref, o_ref, kaug_ref, vt_ref, kmax_ref, pta_ref, ptb_ref, lmin_ref, *, tq):
    S = k_ref.shape[0]
    n_tiles = S // tq
    pts = (pta_ref, ptb_ref)

    lane_s = lax.broadcasted_iota(jnp.int32, (S, LANES), 1)
    for h in range(MLA_HEADS):
        sl = slice(h * LANES, (h + 1) * LANES)
        k = k_ref[:, sl]
        kaug_ref[:, sl] = jnp.where(lane_s == MLA_BIAS_LANE, jnp.ones_like(k), k)
        kmax_ref[h:h + 1, :] = _max_row_norm(k)
    for p in range(MLA_HEADS // 2):
        vt = jnp.transpose(v_ref[:, p * LANES:(p + 1) * LANES].astype(F32))
        _store_vt(vt_ref, 2 * p, vt[0:MLA_V])
        _store_vt(vt_ref, 2 * p + 1, vt[MLA_V:2 * MLA_V])

    def scores(t, pt_ref):
        lane = lax.broadcasted_iota(jnp.int32, (tq, LANES), 1)
        for h in range(MLA_HEADS):
            sl = slice(h * LANES, (h + 1) * LANES)
            q = q_ref[t * tq:(t + 1) * tq, sl]
            b = _row_norm(q) * kmax_ref[h:h + 1, 0:1]
            q_aug = jnp.where(lane == MLA_BIAS_LANE, (-b).astype(BF16), q)
            pt_ref[:, h * tq:(h + 1) * tq] = jnp.exp2(_dot_nt(kaug_ref[:, sl], q_aug)).astype(BF16)

    def values(t, pt_ref):
        lmin = None
        for p in range(MLA_HEADS // 2):
            outs = []
            for h in (2 * p, 2 * p + 1):
                ot = _dot(vt_ref[h], pt_ref[:, h * tq:(h + 1) * tq])
                l = ot[MLA_V:MLA_V + 1, :]
                outs.append(ot[0:MLA_V, :] / l)
                lmin = l if lmin is None else jnp.minimum(lmin, l)
            o_ref[t * tq:(t + 1) * tq, p * LANES:(p + 1) * LANES] = jnp.transpose(
                jnp.concatenate(outs, axis=0)).astype(BF16)
        lmin_ref[t] = jnp.min(lmin)

    scores(0, pts[0])
    for t in range(1, n_tiles):
        values(t - 1, pts[(t - 1) % 2])
        scores(t, pts[t % 2])
    values(n_tiles - 1, pts[(n_tiles - 1) % 2])

    def redo(t, carry):
        @pl.when(lmin_ref[t] < L_MIN)
        def _():
            rows = pl.ds(pl.multiple_of(t * tq, tq), tq)
            _mla_exact(q_ref.at[rows, :], k_ref, v_ref, o_ref.at[rows, :])
        return carry

    lax.fori_loop(0, n_tiles, redo, 0)


def _mla_call(q, k, v, tq):
    B, S, _ = q.shape
    blk = lambda a: pl.BlockSpec((None, S, a.shape[2]), lambda b: (b, 0, 0))
    pt = pltpu.VMEM((S, MLA_HEADS * tq), BF16)
    return pl.pallas_call(
        functools.partial(_mla_attn_kernel, tq=tq),
        grid=(B,),
        in_specs=[blk(q), blk(k), blk(v)],
        out_specs=pl.BlockSpec((None, S, MLA_HEADS * MLA_V), lambda b: (b, 0, 0)),
        out_shape=jax.ShapeDtypeStruct((B, S, MLA_HEADS * MLA_V), BF16),
        scratch_shapes=[pltpu.VMEM((S, MLA_HEADS * LANES), BF16), pltpu.VMEM((MLA_HEADS, VT_ROWS, S), BF16),
                        pltpu.VMEM((8, LANES), F32), pt, pt, pltpu.SMEM((S // tq,), F32)],
        compiler_params=pltpu.CompilerParams(dimension_semantics=("parallel",), vmem_limit_bytes=VMEM_LIMIT),
        name="mla_attn",
    )(q, k, v)


def _gqa_exact(q_ref, k_ref, v_ref, o_ref):
    k = k_ref[...]
    v = v_ref[...]
    low = lax.broadcasted_iota(jnp.int32, (q_ref.shape[0], LANES), 1) < GQA_HEAD_DIM
    for j in range(GQA_HEADS // GQA_KV_HEADS):
        sl = slice(j * LANES, (j + 1) * LANES)
        q = q_ref[:, sl]
        zero = jnp.zeros_like(q)
        o_lo = _softmax_pv(jnp.where(low, q, zero), k, v)
        o_hi = _softmax_pv(jnp.where(low, zero, q), k, v)
        o_ref[:, sl] = jnp.where(low, o_lo, o_hi).astype(BF16)


def _gqa_attn_kernel(*refs, pipelined):
    q_ref, qprev_ref, k_ref, v_ref, o_ref, kaug_ref, vt_ref, kmax_ref, pt_refs, lmin_ref = _unpack_attn_refs(
        refs, pipelined)
    S, tq = k_ref.shape[0], q_ref.shape[0]
    pairs = GQA_HEADS // GQA_KV_HEADS
    n = pairs * tq

    def init():
        lane_s = lax.broadcasted_iota(jnp.int32, (S, LANES), 1)
        k = k_ref[...]
        kaug_ref[:, :LANES] = k
        kaug_ref[:, LANES:] = jnp.where(lane_s == 0, 1.0, 0.0).astype(BF16)
        vt = jnp.transpose(v_ref[...].astype(F32))
        for g in range(GQA_KV_HEADS):
            _store_vt(vt_ref, g, vt[g * GQA_HEAD_DIM:(g + 1) * GQA_HEAD_DIM])
        kmax_ref[0:1, :] = _max_row_norm(k, lane_s < GQA_HEAD_DIM)
        kmax_ref[1:2, :] = _max_row_norm(k, lane_s >= GQA_HEAD_DIM)

    def scores(pt_ref):
        lane = lax.broadcasted_iota(jnp.int32, (tq, LANES), 1)
        low = lane < GQA_HEAD_DIM
        q_aug = [[], []]
        for j in range(pairs):
            q = q_ref[:, j * LANES:(j + 1) * LANES]
            zero = jnp.zeros_like(q)
            for half in range(GQA_KV_HEADS):
                qm = jnp.where(low, q, zero) if half == 0 else jnp.where(low, zero, q)
                b = _row_norm(qm) * kmax_ref[half:half + 1, 0:1]
                bias = jnp.where(lane == 0, -b, 0.0).astype(BF16)
                q_aug[half].append(jnp.concatenate([qm, bias], axis=1))
        pt_ref[...] = jnp.exp2(_dot_nt(kaug_ref[...], jnp.concatenate(q_aug[0] + q_aug[1], axis=0))).astype(BF16)

    def values(pt_ref):
        outs, lmin = [], None
        for half in range(GQA_KV_HEADS):
            ot = _dot(vt_ref[half], pt_ref[:, half * n:(half + 1) * n])
            l = ot[GQA_HEAD_DIM:GQA_HEAD_DIM + 1, :]
            outs.append(ot[0:GQA_HEAD_DIM, :] / l)
            lmin = l if lmin is None else jnp.minimum(lmin, l)
        for j in range(pairs):
            ot = jnp.concatenate([o[:, j * tq:(j + 1) * tq] for o in outs], axis=0)
            o_ref[:, j * LANES:(j + 1) * LANES] = jnp.transpose(ot).astype(BF16)
        return jnp.min(lmin)

    _run_attention(pipelined, init, scores, values, lambda: _gqa_exact(qprev_ref, k_ref, v_ref, o_ref),
                   pt_refs, lmin_ref)


def _attn_call(kernel, q, k, v, out_width, tq, heads, scratch, name, pipelined):
    B, S, _ = q.shape
    n_tiles = S // tq
    pt = pltpu.VMEM((S, heads * tq), BF16)
    kv_specs = [pl.BlockSpec((None, S, k.shape[2]), lambda b, i: (b, 0, 0)),
                pl.BlockSpec((None, S, v.shape[2]), lambda b, i: (b, 0, 0))]
    scratch = scratch + [pltpu.VMEM((8, LANES), F32)]
    if pipelined:
        cur = lambda b, i: (b, jnp.minimum(i, n_tiles - 1), 0)
        prev = lambda b, i: (b, jnp.maximum(i - 1, 0), 0)
        grid = (B, n_tiles + 1)
        q_specs = [pl.BlockSpec((None, tq, q.shape[2]), cur), pl.BlockSpec((None, tq, q.shape[2]), prev)]
        out_spec = pl.BlockSpec((None, tq, out_width), prev)
        scratch = scratch + [pt, pt, pltpu.SMEM((1,), F32)]
    else:
        tile = lambda b, i: (b, i, 0)
        grid = (B, n_tiles)
        q_specs = [pl.BlockSpec((None, tq, q.shape[2]), tile)]
        out_spec = pl.BlockSpec((None, tq, out_width), tile)
        scratch = scratch + [pt]
    return pl.pallas_call(
        functools.partial(kernel, pipelined=pipelined),
        grid=grid,
        in_specs=q_specs + kv_specs,
        out_specs=out_spec,
        out_shape=jax.ShapeDtypeStruct((B, S, out_width), BF16),
        scratch_shapes=scratch,
        compiler_params=pltpu.CompilerParams(dimension_semantics=("parallel", "arbitrary"),
                                             vmem_limit_bytes=VMEM_LIMIT),
        name=name,
    )(*([q] * len(q_specs)), k, v)


def _post_kernel(x_ref, mod_ref, ona_ref, omla_ref, ogqa_ref, gate_ref, wa_ref, wb_ref, wc_ref, wo_ref,
                 g_ref, wgu_ref, wd_ref, fg_ref, o_ref, *, final_norm):
    D = D_MODEL
    m = (gate_ref[:, 0:D].astype(F32) * _dot(ona_ref[...], wa_ref[...])
         + gate_ref[:, D:2 * D].astype(F32) * _dot(omla_ref[...], wb_ref[...])
         + gate_ref[:, 2 * D:3 * D].astype(F32) * _dot(ogqa_ref[...], wc_ref[...]))
    x = x_ref[...] + mod_ref[2:3, :] * _dot(m.astype(BF16), wo_ref[...])

    h = _rms(x) * g_ref[...] * (1.0 + mod_ref[4:5, :]) + mod_ref[3:4, :]
    hb = h.astype(BF16)
    acc = jnp.zeros(x.shape, F32)
    for c0, c1 in FFN_CHUNKS:
        gate = _dot(hb, wgu_ref[:, c0:c1])
        up = _dot(hb, wgu_ref[:, FFN_HIDDEN + c0:FFN_HIDDEN + c1])
        act = (gate * _sigmoid(gate) * up).astype(BF16)
        acc = acc + _dot(act, wd_ref[c0:c1, :])
    y = x + mod_ref[5:6, :] * acc
    if final_norm:
        y = _rms(y) * fg_ref[...]
    o_ref[...] = y


def _post_call(x, mod, l, ona, omla, ogqa, gates, wa, wb, wc, wo, g, wgu, wd, fg, final_norm):
    B, S, D = x.shape
    tm = TOKEN_TILE
    tok = lambda a: pl.BlockSpec((None, tm, a.shape[2]), lambda b, i: (b, i, 0))
    weights = (wa, wb, wc, wo, g, wgu, wd)
    return pl.pallas_call(
        functools.partial(_post_kernel, final_norm=final_norm),
        grid=(B, S // tm),
        in_specs=[tok(x), _mod_spec(mod, l), tok(ona), tok(omla), tok(ogqa), tok(gates)]
        + [_layer_spec(a, l, single_buffer=True) for a in weights]
        + [pl.BlockSpec(fg.shape, lambda b, i: (0, 0))],
        out_specs=tok(x),
        out_shape=jax.ShapeDtypeStruct(x.shape, F32),
        compiler_params=pltpu.CompilerParams(dimension_semantics=("parallel", "parallel"),
                                             vmem_limit_bytes=VMEM_LIMIT),
        name="merge_swiglu",
    )(x, mod, ona, omla, ogqa, gates, *weights, fg)


def _rope_tables(S):
    f32 = np.float32
    t = np.arange(S, dtype=np.int32)
    pos_t = t.astype(f32)
    pos_row = (t // GRID_W).astype(f32)
    pos_col = (t % GRID_W).astype(f32)
    half = 16
    lane = np.arange(LANES)

    def tables(pos_of_lane, roped, first_half, inv_freq):
        ang = (pos_of_lane * inv_freq[lane % half][None, :]).astype(f32)
        cos, sin = np.cos(ang), np.sin(ang)
        c = np.where(roped[None, :], cos, f32(1.0))
        up = np.where((roped & first_half)[None, :], -sin, f32(0.0))
        dn = np.where((roped & ~first_half)[None, :], sin, f32(0.0))
        return tuple(jnp.asarray(a, F32) for a in (c, up, dn))

    inv_t = (f32(1.0) / (f32(ROPE_THETA) ** (np.arange(half, dtype=f32) / f32(half)))).astype(f32)
    inv_a = (f32(1.0) / (f32(AXIAL_THETA) ** (np.arange(half, dtype=f32) / f32(half)))).astype(f32)
    m_roped = (lane >= MLA_NOPE) & (lane < MLA_NOPE + MLA_ROPE)
    m_first = ((lane - MLA_NOPE) % 32) < half
    mt = tables(np.broadcast_to(pos_t[:, None], (S, LANES)), m_roped, m_first, inv_t)
    a_first = (lane % 32) < half
    use_row = (lane % GRID_W) < 32
    pos_rc = np.where(use_row[None, :], pos_row[:, None], pos_col[:, None])
    at = tables(pos_rc, np.ones(LANES, bool), a_first, inv_a)
    return mt, at


def _na_bias_kernel(ext_ref, o_ref):
    W = GRID_W
    n_off = 2 * NA_KH - 1
    col = lax.broadcasted_iota(jnp.int32, (W, LANES), 0)
    lane = lax.broadcasted_iota(jnp.int32, (W, LANES), 1)
    kc = lane % W
    ws = jnp.clip(col - NA_KW // 2, 0, W - NA_KW)
    valid = (kc >= ws) & (kc < ws + NA_KW)
    low = lane < W
    for h in range(NA_HEADS):
        lo, hi = [], []
        for ro in range(n_off):
            e = jnp.broadcast_to(ext_ref[h * n_off + ro:h * n_off + ro + 1, :], (W, LANES))
            lo.append(pltpu.roll(e, W + 1, 1, stride=1, stride_axis=0))
            hi.append(pltpu.roll(e, 1, 1, stride=1, stride_axis=0))
        for d in range(NA_KH):
            for p in range(NA_KH // 2):
                ro = 2 * p - d + NA_KH - 1
                blk = jnp.where(low, lo[ro], hi[ro + 1])
                o_ref[d, h * W:(h + 1) * W, p * LANES:(p + 1) * LANES] = jnp.where(valid, blk * LOG2E, MASK_VALUE)


def _na_bias_table(rpb):
    L, W = rpb.shape[0], GRID_W
    pad = W - NA_KW
    ext = jnp.concatenate([jnp.repeat(rpb[..., :1], pad, -1), rpb, jnp.repeat(rpb[..., -1:], pad + 1, -1)], -1)
    ext = ext.reshape(L, NA_HEADS * (2 * NA_KH - 1), 2 * W).astype(F32)
    return pl.pallas_call(
        _na_bias_kernel,
        grid=(L,),
        in_specs=[pl.BlockSpec((None,) + ext.shape[1:], lambda l: (l, 0, 0))],
        out_specs=pl.BlockSpec((None, NA_KH, NA_HEADS * W, NA_KH * W), lambda l: (l, 0, 0, 0)),
        out_shape=jax.ShapeDtypeStruct((L, NA_KH, NA_HEADS * W, NA_KH * W), F32),
        compiler_params=pltpu.CompilerParams(dimension_semantics=("parallel",), vmem_limit_bytes=VMEM_LIMIT),
        name="na_bias",
    )(ext)


W_IN_KR = 1152
W_IN_QC = 1184
W_IN_COLS = 5024


def _w_in_prep_kernel(w_ref, o_ref):
    tk = w_ref.shape[1]
    hd, pairs = GQA_HEAD_DIM, GQA_HEADS // GQA_KV_HEADS
    kr0 = C_MLA + 384 + MLA_NOPE
    o_ref[0:W_IN_KR, :] = w_ref[0:W_IN_KR, :].astype(BF16)
    o_ref[C_MLA + 384:kr0, :] = jnp.zeros((MLA_NOPE, tk), BF16)
    o_ref[kr0:kr0 + MLA_ROPE, :] = w_ref[W_IN_KR:W_IN_QC, :].astype(BF16)
    o_ref[kr0 + MLA_ROPE:C_GQA, :] = jnp.zeros((LANES - MLA_NOPE - MLA_ROPE, tk), BF16)
    for j in range(pairs):
        for half, head in enumerate((j, pairs + j)):
            dst = C_GQA + LANES * j + hd * half
            o_ref[dst:dst + hd, :] = w_ref[W_IN_QC + hd * head:W_IN_QC + hd * (head + 1), :].astype(BF16)
    o_ref[C_GQA + 512:C_TOTAL, :] = w_ref[W_IN_QC + 512:W_IN_COLS, :].astype(BF16)


def _prep_w_in(w):
    L, D, N = w.shape
    tk = W_IN_PREP_TILE
    return pl.pallas_call(
        _w_in_prep_kernel,
        grid=(L, D // tk),
        in_specs=[pl.BlockSpec((None, N, tk), lambda l, i: (l, 0, i))],
        out_specs=pl.BlockSpec((None, C_TOTAL, tk), lambda l, i: (l, 0, i)),
        out_shape=jax.ShapeDtypeStruct((L, C_TOTAL, D), BF16),
        compiler_params=pltpu.CompilerParams(dimension_semantics=("parallel", "parallel"),
                                             vmem_limit_bytes=VMEM_LIMIT),
        name="w_in_prep",
    )(jnp.swapaxes(w, 1, 2))


def _prep_w_uq(w):
    z = jnp.zeros(w.shape[:-1] + (32,), BF16)
    hd = MLA_NOPE + MLA_ROPE
    return jnp.concatenate([a for h in range(MLA_HEADS) for a in (w[..., hd * h:hd * (h + 1)].astype(BF16), z)],
                           axis=-1)


def _prep_w_ukv(w):
    z = jnp.zeros(w.shape[:-1] + (64,), BF16)
    hd = MLA_NOPE + MLA_V
    ks = [a for h in range(MLA_HEADS) for a in (w[..., hd * h:hd * h + MLA_NOPE].astype(BF16), z)]
    vs = [w[..., hd * h + MLA_NOPE:hd * (h + 1)].astype(BF16) for h in range(MLA_HEADS)]
    return jnp.concatenate(ks + vs, axis=-1)


def _prep_w_br_gqa(w):
    rows = []
    for j in range(4):
        rows += [w[:, 64 * j:64 * (j + 1)], w[:, 64 * (4 + j):64 * (5 + j)]]
    return jnp.concatenate([r.astype(BF16) for r in rows], axis=1)


def kernel(x, c, ada_w, ada_b, norm_mix_g, norm_ffn_g, w_in, na_rpb, mla_q_norm_g, mla_kv_norm_g, mla_w_uq,
           mla_w_ukv, gqa_q_norm_g, gqa_k_norm_g, w_br_na, w_br_mla, w_br_gqa, w_out, ffn_w_gate_up,
           ffn_w_down, final_norm_g):
    B, S, D = x.shape
    L = ada_w.shape[0]
    mod = _ada_call(c, ada_w, ada_b).reshape(L, B, N_MOD, D)
    mtabs, atabs = _rope_tables(S)
    lane = np.arange(LANES)
    bd = jnp.asarray(np.tile((lane[:, None] // 64 == lane[None, :] // 64) / 64.0, (2, 1)), BF16)
    row = lambda a: a.reshape(L, 1, -1)
    gqk = jnp.concatenate([jnp.tile(gqa_q_norm_g, (1, GQA_HEADS)), jnp.tile(gqa_k_norm_g, (1, GQA_KV_HEADS))], 1)
    w_in_p, w_uq_p, w_ukv_p = _prep_w_in(w_in), _prep_w_uq(mla_w_uq), _prep_w_ukv(mla_w_ukv)
    bias = _na_bias_table(na_rpb)
    wa, wb, wc, wo = w_br_na.astype(BF16), w_br_mla.astype(BF16), _prep_w_br_gqa(w_br_gqa), w_out.astype(BF16)
    wgu, wd = ffn_w_gate_up.astype(BF16), ffn_w_down.astype(BF16)
    fg = final_norm_g.reshape(1, D)
    for l in range(L):
        qna, kna, vna, qm, km, vm, qg, kg, vg, gates = _in_call(
            x, mod, l, row(norm_mix_g), w_in_p, row(mla_q_norm_g), row(mla_kv_norm_g), w_uq_p, w_ukv_p,
            row(gqk), bd, mtabs, atabs)
        o_na = _na_call(qna, kna, vna, bias, l)
        o_mla = _mla_call(qm, km, vm, MLA_Q_TILE)
        o_gqa = _attn_call(_gqa_attn_kernel, qg, kg, vg, GQA_HEADS * GQA_HEAD_DIM, GQA_Q_TILE, GQA_HEADS,
                           [pltpu.VMEM((S, 2 * LANES), BF16), pltpu.VMEM((GQA_KV_HEADS, VT_ROWS, S), BF16)],
                           "gqa_attn", pipelined=True)
        x = _post_call(x, mod, l, o_na, o_mla, o_gqa, gates, wa, wb, wc, wo, row(norm_ffn_g), wgu, wd, fg,
                       final_norm=(l == L - 1))
    return x
```
